```python
import jax, jax.numpy as jnp
from jax import lax
import numpy as np

D_MODEL = 2048
BATCH = 2
SEQ = 4096
DEPTH = 4
DEC_BATCH = 32
DEC_SEQ = 4
PAST_LEN = 16384
PAGE_SIZE = 128

N_META = 16
MIX_DIM = D_MODEL
ATT_DIM = MIX_DIM // 2
CONV_DIM = MIX_DIM - ATT_DIM
HEAD_DIM = 64
N_HEADS = ATT_DIM // HEAD_DIM
N_KV_HEADS = 4
GQA_GROUP = N_HEADS // N_KV_HEADS
KV_DIM = N_KV_HEADS * HEAD_DIM
WINDOW = 128
BLOCK = 128
CONV_WIDTH = 3
PROJ_DIM = ATT_DIM + 2 * KV_DIM + ATT_DIM + 4 * CONV_DIM
DEEPNORM_ALPHA = (2 * DEPTH) ** 0.25
DEEPNORM_BETA = (8 * DEPTH) ** -0.25
LN_EPS = 1e-5
NEG_INF = -1e30

kernel_name = "hymba_swa_sink_gated_conv_deepnorm_step"


def alibi_slopes():
    return 2.0 ** (-8.0 * jnp.arange(1, N_HEADS + 1, dtype=jnp.float32) / N_HEADS)


def layer_norm(x, g, b):
    xf = x.astype(jnp.float32)
    mu = xf.mean(-1, keepdims=True)
    var = jnp.square(xf - mu).mean(-1, keepdims=True)
    y = (xf - mu) * lax.rsqrt(var + LN_EPS) * g.astype(jnp.float32) + b.astype(jnp.float32)
    return y.astype(x.dtype)


def split_projection(x, w_in):
    n, t = x.shape[:2]
    h = jnp.einsum('btd,dp->btp', x, w_in)
    splits = [ATT_DIM, ATT_DIM + KV_DIM, ATT_DIM + 2 * KV_DIM, 2 * ATT_DIM + 2 * KV_DIM,
              2 * ATT_DIM + 2 * KV_DIM + CONV_DIM, 2 * ATT_DIM + 2 * KV_DIM + 2 * CONV_DIM,
              2 * ATT_DIM + 2 * KV_DIM + 3 * CONV_DIM]
    q, k, v, g_att, b_gate, c_gate, h_conv, g_conv = jnp.split(h, splits, axis=-1)
    q = q.reshape(n, t, N_KV_HEADS, GQA_GROUP, HEAD_DIM)
    k = k.reshape(n, t, N_KV_HEADS, HEAD_DIM)
    v = v.reshape(n, t, N_KV_HEADS, HEAD_DIM)
    u = c_gate * h_conv
    return q, k, v, g_att, b_gate, u, g_conv


def sink_attention(q, k, v, dist, mask, sinks):
    scale = HEAD_DIM ** -0.5
    s = jnp.einsum('...qkgd,...skd->...kgqs', q, k, preferred_element_type=jnp.float32) * scale
    slopes = alibi_slopes().reshape(N_KV_HEADS, GQA_GROUP, 1, 1)
    s = jnp.where(mask, s - slopes * dist.astype(jnp.float32), NEG_INF)
    sink = sinks.astype(jnp.float32).reshape(N_KV_HEADS, GQA_GROUP, 1, 1)
    m = jnp.maximum(s.max(-1, keepdims=True), sink)
    p = jnp.exp(s - m)
    denom = p.sum(-1, keepdims=True) + jnp.exp(sink - m)
    p = (p / denom).astype(v.dtype)
    return jnp.einsum('...kgqs,...skd->...qkgd', p, v)


def prompt_window_attention(q, k, v, sinks):
    b, L = q.shape[:2]
    pad = (-L) % BLOCK
    nb = (L + pad) // BLOCK
    qp = jnp.pad(q, ((0, 0), (pad, 0), (0, 0), (0, 0), (0, 0))).reshape(b, nb, BLOCK, N_KV_HEADS, GQA_GROUP, HEAD_DIM)

    def banded(t):
        tb = jnp.pad(t, ((0, 0), (pad, 0), (0, 0), (0, 0))).reshape(b, nb, BLOCK, N_KV_HEADS, HEAD_DIM)
        prev = jnp.pad(tb[:, :-1], ((0, 0), (1, 0), (0, 0), (0, 0), (0, 0)))
        return jnp.concatenate([prev, tb], axis=2)

    kk, vv = banded(k), banded(v)
    i = jnp.arange(BLOCK)[:, None]
    r = jnp.arange(2 * BLOCK)[None, :]
    dist = BLOCK + i - r
    k_pos = (jnp.arange(nb)[:, None, None] - 1) * BLOCK + r[None] - pad
    mask = (dist >= 0) & (dist < WINDOW) & (k_pos >= 0)
    mask = mask[:, None, None]
    o = sink_attention(qp, kk, vv, dist, mask, sinks)
    return o.reshape(b, nb * BLOCK, ATT_DIM)[:, pad:]


def sample_window_attention(q, k_new, v_new, cache_k, cache_v, sinks):
    n, t = q.shape[:2]
    w = cache_k.shape[1]
    kk = jnp.concatenate([cache_k.astype(k_new.dtype), k_new], axis=1)
    vv = jnp.concatenate([cache_v.astype(v_new.dtype), v_new], axis=1)
    i = jnp.arange(t)[:, None]
    r = jnp.arange(w + t)[None, :]
    dist = w + i - r
    mask = (dist >= 0) & (dist < WINDOW)
    o = sink_attention(q, kk, vv, dist, mask, sinks)
    return o.reshape(n, t, ATT_DIM), kk[:, -WINDOW:], vv[:, -WINDOW:]


def causal_conv(u, w, prev):
    t = u.shape[1]
    ext = jnp.concatenate([prev.astype(u.dtype), u], axis=1)
    y = w[0] * ext[:, 0:t]
    for j in range(1, CONV_WIDTH):
        y = y + w[j] * ext[:, j:j + t]
    return y, ext[:, -(CONV_WIDTH - 1):]


def combine(x, o_att, o_conv, g_att, g_conv, w_out, ln_g, ln_b):
    mix = jnp.concatenate([o_att * jax.nn.silu(g_att), o_conv * jax.nn.silu(g_conv)], axis=-1)
    out = jnp.einsum('btm,md->btd', mix, w_out)
    return layer_norm(DEEPNORM_ALPHA * x + out, ln_g, ln_b)


def setup_inputs(seed: int = 0) -> dict:
    key = jax.random.key(seed)
    ks = jax.random.split(key, 12)
    f32 = jnp.float32
    return {
        "x_prompt": jax.random.normal(ks[0], (BATCH, SEQ, D_MODEL), f32),
        "x_sample": jax.random.normal(ks[1], (DEC_BATCH, DEC_SEQ, D_MODEL), f32),
        "cache_k": jax.random.normal(ks[2], (DEPTH, DEC_BATCH, WINDOW, N_KV_HEADS, HEAD_DIM), f32),
        "cache_v": jax.random.normal(ks[3], (DEPTH, DEC_BATCH, WINDOW, N_KV_HEADS, HEAD_DIM), f32),
        "state_conv": jax.random.normal(ks[4], (DEPTH, DEC_BATCH, CONV_WIDTH - 1, CONV_DIM), f32),
        "meta_tokens": jax.random.normal(ks[5], (N_META, D_MODEL), f32),
        "w_in": jax.random.normal(ks[6], (DEPTH, D_MODEL, PROJ_DIM), f32) * D_MODEL ** -0.5,
        "conv_w": jax.random.normal(ks[7], (DEPTH, CONV_WIDTH, CONV_DIM), f32) * CONV_WIDTH ** -0.5,
        "sinks": jax.random.normal(ks[8], (DEPTH, N_HEADS), f32) * 0.5,
        "w_out": jax.random.normal(ks[9], (DEPTH, MIX_DIM, D_MODEL), f32) * (MIX_DIM ** -0.5 * DEEPNORM_BETA),
        "ln_g": 1.0 + 0.02 * jax.random.normal(ks[10], (DEPTH, D_MODEL), f32),
        "ln_b": 0.02 * jax.random.normal(ks[11], (DEPTH, D_MODEL), f32),
    }


def reference(x_prompt, x_sample, cache_k, cache_v, state_conv, meta_tokens,
              w_in, conv_w, sinks, w_out, ln_g, ln_b):
    b = x_prompt.shape[0]
    meta = jnp.broadcast_to(meta_tokens[None].astype(x_prompt.dtype), (b, N_META, D_MODEL))
    xp = jnp.concatenate([meta, x_prompt], axis=1)
    xs = x_sample
    kp_l, vp_l, cp_l, ks_l, vs_l, cs_l = [], [], [], [], [], []
    for l in range(DEPTH):
        q, k, v, g_a, b_g, u, g_c = split_projection(xp, w_in[l])
        o_att = prompt_window_attention(q, k, v, sinks[l])
        cy, c_state = causal_conv(u, conv_w[l], jnp.zeros((b, CONV_WIDTH - 1, CONV_DIM), u.dtype))
        xp = combine(xp, o_att, b_g * cy, g_a, g_c, w_out[l], ln_g[l], ln_b[l])
        kp_l.append(k[:, -WINDOW:])
        vp_l.append(v[:, -WINDOW:])
        cp_l.append(c_state)
        q, k, v, g_a, b_g, u, g_c = split_projection(xs, w_in[l])
        o_att, k_buf, v_buf = sample_window_attention(q, k, v, cache_k[l], cache_v[l], sinks[l])
        cy, c_state = causal_conv(u, conv_w[l], state_conv[l])
        xs = combine(xs, o_att, b_g * cy, g_a, g_c, w_out[l], ln_g[l], ln_b[l])
        ks_l.append(k_buf)
        vs_l.append(v_buf)
        cs_l.append(c_state)
    y_prompt = xp[:, N_META:]
    return (y_prompt, xs, jnp.stack(kp_l), jnp.stack(vp_l), jnp.stack(cp_l),
            jnp.stack(ks_l), jnp.stack(vs_l), jnp.stack(cs_l))
```

```python
import functools

import numpy as np
import jax
import jax.numpy as jnp
from jax import lax
from jax.experimental import pallas as pl
from jax.experimental.pallas import tpu as pltpu

F32 = jnp.float32
BF16 = jnp.bfloat16

D_MODEL = 2048
N_META = 16
ATT_DIM = 1024
CONV_DIM = 1024
HEAD_DIM = 64
N_HEADS = 16
N_KV_HEADS = 4
GQA_GROUP = N_HEADS // N_KV_HEADS
KV_DIM = N_KV_HEADS * HEAD_DIM
WINDOW = 128
PROJ_DIM = 2 * ATT_DIM + 2 * KV_DIM + 4 * CONV_DIM
LN_EPS = 1e-5
NEG_INF = -1e30
Q_SCALE = HEAD_DIM ** -0.5

C_Q = 0
C_K = ATT_DIM
C_V = C_K + KV_DIM
C_GA = C_V + KV_DIM
C_B = C_GA + ATT_DIM
C_C = C_B + CONV_DIM
C_H = C_C + CONV_DIM
C_GC = C_H + CONV_DIM

LANES = 128
HEAD_PAD = WINDOW - N_META
PROMPT_TILE = 256
SAMPLE_ROWS = 8
VMEM_LIMIT = 58 * 1024 * 1024

SLOPES = [float(np.float32(2.0 ** (-8.0 * (h + 1) / N_HEADS))) for h in range(N_HEADS)]

_NT = (((1,), (1,)), ((), ()))


def _silu(g):
    return g * (1.0 / (1.0 + jnp.exp(-g)))


def _softmax_rows(s, sink):
    m = jnp.maximum(jnp.max(s, axis=1, keepdims=True), sink)
    p = jnp.exp(s - m)
    denom = jnp.sum(p, axis=1, keepdims=True) + jnp.exp(sink - m)
    return p * (1.0 / denom)


def _layer_norm(z, g, b):
    mu = jnp.mean(z, axis=1, keepdims=True)
    zc = z - mu
    var = jnp.mean(zc * zc, axis=1, keepdims=True)
    return zc * lax.rsqrt(var + LN_EPS) * g + b


def _proj(xb, w_ref, c0, width):
    return jnp.dot(xb, w_ref[:, c0:c0 + width], preferred_element_type=F32)


def _store_kv_variants(src, dst, r0, rows):
    low = lax.broadcasted_iota(jnp.int32, (rows, LANES), 1) < HEAD_DIM
    for cc in range(KV_DIM // LANES):
        col = src[:, cc * LANES:(cc + 1) * LANES]
        swapped = pltpu.roll(col, HEAD_DIM, axis=1)
        h_even, h_odd = 2 * cc, 2 * cc + 1
        dst[2 * h_even + 0, r0:r0 + rows, :] = jnp.where(low, col, 0.0).astype(BF16)
        dst[2 * h_even + 1, r0:r0 + rows, :] = jnp.where(low, 0.0, swapped).astype(BF16)
        dst[2 * h_odd + 0, r0:r0 + rows, :] = jnp.where(low, swapped, 0.0).astype(BF16)
        dst[2 * h_odd + 1, r0:r0 + rows, :] = jnp.where(low, 0.0, col).astype(BF16)


def _prompt_rows(x, rows, kmask0, sinks_ref, win_ref, wout_ref, cw_ref, g_ref, b_ref,
                 kvar, vvar, ucar, bias_ref, mix_ref, alpha):
    xb = x.astype(BF16)
    hq = _proj(xb, win_ref, 0, C_GA)
    qb = (hq[:, C_Q:C_Q + ATT_DIM] * Q_SCALE).astype(BF16)
    kf = hq[:, C_K:C_K + KV_DIM]
    vf = hq[:, C_V:C_V + KV_DIM]
    _store_kv_variants(kf, kvar, WINDOW, rows)
    _store_kv_variants(vf, vvar, WINDOW, rows)
    gate_a = _silu(_proj(xb, win_ref, C_GA, ATT_DIM))

    for blk in range(rows // WINDOW):
        r0 = blk * WINDOW
        for c in range(ATT_DIM // LANES):
            h = c // 2
            q2 = qb[r0:r0 + WINDOW, c * LANES:(c + 1) * LANES]
            o2 = None
            for par in range(2):
                hd = 2 * c + par
                keys = kvar[2 * h + par, r0:r0 + 2 * WINDOW, :]
                s = lax.dot_general(q2, keys, _NT, preferred_element_type=F32)
                s = s + bias_ref[hd]
                if blk == 0 and kmask0 is not None:
                    s = s + kmask0
                p = _softmax_rows(s, sinks_ref[hd]).astype(BF16)
                vals = vvar[2 * h + par, r0:r0 + 2 * WINDOW, :]
                o = jnp.dot(p, vals, preferred_element_type=F32)
                o2 = o if o2 is None else o2 + o
            gate = gate_a[r0:r0 + WINDOW, c * LANES:(c + 1) * LANES]
            mix_ref[r0:r0 + WINDOW, c * LANES:(c + 1) * LANES] = (o2 * gate).astype(BF16)

    bg = _proj(xb, win_ref, C_B, CONV_DIM)
    u = _proj(xb, win_ref, C_C, CONV_DIM) * _proj(xb, win_ref, C_H, CONV_DIM)
    row = lax.broadcasted_iota(jnp.int32, (rows, CONV_DIM), 0)
    prev1 = ucar[7:8, :]
    prev2 = ucar[6:7, :]
    u1 = jnp.where(row == 0, prev1, pltpu.roll(u, 1, axis=0))
    u2 = jnp.where(row == 0, prev2, jnp.where(row == 1, prev1, pltpu.roll(u, 2, axis=0)))
    cy = cw_ref[0:1, :] * u2 + cw_ref[1:2, :] * u1 + cw_ref[2:3, :] * u
    ucar[...] = u[rows - 8:rows, :]
    gate_c = _silu(_proj(xb, win_ref, C_GC, CONV_DIM))
    mix_ref[0:rows, ATT_DIM:] = ((bg * cy) * gate_c).astype(BF16)

    out = jnp.dot(mix_ref[0:rows, :], wout_ref[...], preferred_element_type=F32)
    y = _layer_norm(alpha * x + out, g_ref[...], b_ref[...])

    for i in range(2 * N_KV_HEADS):
        kvar[i, 0:WINDOW, :] = kvar[i, rows:rows + WINDOW, :]
        vvar[i, 0:WINDOW, :] = vvar[i, rows:rows + WINDOW, :]
    return y, kf, vf, u


def _prompt_kernel(sinks_ref, x_ref, xh_ref, win_ref, wout_ref, cw_ref, g_ref, b_ref,
                   y_ref, yh_ref, kl_ref, vl_ref, cs_ref,
                   kvar, vvar, khead, vhead, ucar, uhead, bias_ref, mix_ref, *, tm, alpha):
    b = pl.program_id(0)
    j = pl.program_id(1)
    n_tiles = pl.num_programs(1)
    shared = (sinks_ref, win_ref, wout_ref, cw_ref, g_ref, b_ref, kvar, vvar, ucar, bias_ref,
              mix_ref, alpha)
    key_lane = lax.broadcasted_iota(jnp.int32, (1, 2 * WINDOW), 1)

    @pl.when((b == 0) & (j == 0))
    def _():
        qi = lax.broadcasted_iota(jnp.int32, (WINDOW, 2 * WINDOW), 0)
        kr = lax.broadcasted_iota(jnp.int32, (WINDOW, 2 * WINDOW), 1)
        dist = WINDOW + qi - kr
        visible = (dist >= 0) & (dist < WINDOW)
        distf = dist.astype(F32)
        for hd in range(N_HEADS):
            bias_ref[hd] = jnp.where(visible, -(SLOPES[hd] * distf), NEG_INF)
        kvar[:, 0:WINDOW, :] = jnp.zeros((2 * N_KV_HEADS, WINDOW, LANES), BF16)
        vvar[:, 0:WINDOW, :] = jnp.zeros((2 * N_KV_HEADS, WINDOW, LANES), BF16)
        ucar[...] = jnp.zeros(ucar.shape, F32)
        kmask = jnp.where(key_lane < WINDOW + HEAD_PAD, NEG_INF, 0.0)
        yh, _, _, _ = _prompt_rows(xh_ref[...], WINDOW, kmask, *shared)
        hrow = lax.broadcasted_iota(jnp.int32, (WINDOW, D_MODEL), 0)
        yh_ref[...] = jnp.where(hrow >= HEAD_PAD, yh, 0.0)
        khead[...] = kvar[:, 0:WINDOW, :]
        vhead[...] = vvar[:, 0:WINDOW, :]
        uhead[...] = ucar[...]

    @pl.when(j == 0)
    def _():
        kvar[:, 0:WINDOW, :] = khead[...]
        vvar[:, 0:WINDOW, :] = vhead[...]
        ucar[...] = uhead[...]

    kmask = jnp.where(key_lane < HEAD_PAD, jnp.where(j == 0, NEG_INF, 0.0), 0.0)
    y, kf, vf, u = _prompt_rows(x_ref[...], tm, kmask, *shared)
    y_ref[...] = y

    @pl.when(j == n_tiles - 1)
    def _():
        kl_ref[...] = kf[tm - WINDOW:tm, :]
        vl_ref[...] = vf[tm - WINDOW:tm, :]
        cs_ref[...] = u[tm - 8:tm, :]


def _resident(shape, n_grid):
    zeros = (0,) * len(shape)
    if n_grid == 1:
        return pl.BlockSpec(shape, lambda g: zeros, pipeline_mode=pl.Buffered(1))
    return pl.BlockSpec(shape, lambda b, j: zeros, pipeline_mode=pl.Buffered(1))


def _prompt_layer(x, xh, w_in, w_out, conv_w, sinks, ln_g, ln_b, alpha):
    batch, seq, d = x.shape
    tm = PROMPT_TILE
    n_tiles = seq // tm
    const2 = lambda b, j: (0, 0)
    per_batch = lambda b, j: (b, 0, 0)
    kernel = functools.partial(_prompt_kernel, tm=tm, alpha=alpha)
    return pl.pallas_call(
        kernel,
        grid=(batch, n_tiles),
        in_specs=[
            pl.BlockSpec(memory_space=pltpu.SMEM),
            pl.BlockSpec((None, tm, d), lambda b, j: (b, j, 0)),
            pl.BlockSpec((WINDOW, d), const2),
            _resident((d, PROJ_DIM), 2),
            _resident((ATT_DIM + CONV_DIM, d), 2),
            pl.BlockSpec((3, CONV_DIM), const2),
            pl.BlockSpec((1, d), const2),
            pl.BlockSpec((1, d), const2),
        ],
        out_specs=[
            pl.BlockSpec((None, tm, d), lambda b, j: (b, j, 0)),
            pl.BlockSpec((WINDOW, d), const2),
            pl.BlockSpec((None, WINDOW, KV_DIM), per_batch),
            pl.BlockSpec((None, WINDOW, KV_DIM), per_batch),
            pl.BlockSpec((None, 8, CONV_DIM), per_batch),
        ],
        out_shape=[
            jax.ShapeDtypeStruct((batch, seq, d), F32),
            jax.ShapeDtypeStruct((WINDOW, d), F32),
            jax.ShapeDtypeStruct((batch, WINDOW, KV_DIM), F32),
            jax.ShapeDtypeStruct((batch, WINDOW, KV_DIM), F32),
            jax.ShapeDtypeStruct((batch, 8, CONV_DIM), F32),
        ],
        scratch_shapes=[
            pltpu.VMEM((2 * N_KV_HEADS, WINDOW + tm, LANES), BF16),
            pltpu.VMEM((2 * N_KV_HEADS, WINDOW + tm, LANES), BF16),
            pltpu.VMEM((2 * N_KV_HEADS, WINDOW, LANES), BF16),
            pltpu.VMEM((2 * N_KV_HEADS, WINDOW, LANES), BF16),
            pltpu.VMEM((8, CONV_DIM), F32),
            pltpu.VMEM((8, CONV_DIM), F32),
            pltpu.VMEM((N_HEADS, WINDOW, 2 * WINDOW), F32),
            pltpu.VMEM((tm, ATT_DIM + CONV_DIM), BF16),
        ],
        compiler_params=pltpu.CompilerParams(
            dimension_semantics=("arbitrary", "arbitrary"),
            vmem_limit_bytes=VMEM_LIMIT,
        ),
        name="prompt_layer",
    )(sinks, x, xh, w_in, w_out, conv_w, ln_g, ln_b)


SEQ_PER_GROUP = 2
GROUP_ROWS = SEQ_PER_GROUP * SAMPLE_ROWS
TOK0 = SAMPLE_ROWS - 4
LOG_SAMPLE_ROWS = 3
LOG_GROUP_ROWS = 4
LOG_HEAD_DIM = 6
assert (1 << LOG_SAMPLE_ROWS, 1 << LOG_GROUP_ROWS, 1 << LOG_HEAD_DIM) == (
    SAMPLE_ROWS, GROUP_ROWS, HEAD_DIM)


def _sample_kernel(sinks_ref, x_ref, win_ref, wout_ref, cw_ref, g_ref, b_ref, ck_ref, cv_ref,
                   st_ref, y_ref, kb_ref, vb_ref, u_ref,
                   qe, knew, vnew, kk, vv, gate_a, bias_ref, sinkcol, mix_ref, *, n_rows, alpha):
    grp = pl.program_id(0)
    n_groups = pl.num_programs(0)
    n_q = N_HEADS * GROUP_ROWS
    lane_blk = lax.broadcasted_iota(jnp.int32, (n_rows, KV_DIM), 1) >> LOG_HEAD_DIM

    @pl.when(grp == 0)
    def _():
        x = x_ref[...]
        xb = x.astype(BF16)
        hq = _proj(xb, win_ref, 0, C_GA)
        q = hq[:, C_Q:C_Q + ATT_DIM] * Q_SCALE
        knew[...] = hq[:, C_K:C_K + KV_DIM]
        vnew[...] = hq[:, C_V:C_V + KV_DIM]
        for hd in range(N_HEADS):
            h, g = divmod(hd, GQA_GROUP)
            slab = q[:, h * KV_DIM:(h + 1) * KV_DIM]
            moved = pltpu.roll(slab, ((h - g) % GQA_GROUP) * HEAD_DIM, axis=1)
            qe[hd] = jnp.where(lane_blk == h, moved, 0.0).astype(BF16)
        gate_a[...] = _silu(_proj(xb, win_ref, C_GA, ATT_DIM))

        bg = _proj(xb, win_ref, C_B, CONV_DIM)
        u = _proj(xb, win_ref, C_C, CONV_DIM) * _proj(xb, win_ref, C_H, CONV_DIM)
        r8 = lax.broadcasted_iota(jnp.int32, (n_rows, CONV_DIM), 0) & (SAMPLE_ROWS - 1)
        is_state = (r8 >= TOK0 - 2) & (r8 < TOK0)
        u = jnp.where(is_state, st_ref[...], u)
        u_ref[...] = u
        cy = (cw_ref[0:1, :] * pltpu.roll(u, 2, axis=0) + cw_ref[1:2, :] * pltpu.roll(u, 1, axis=0)
              + cw_ref[2:3, :] * u)
        gate_c = _silu(_proj(xb, win_ref, C_GC, CONV_DIM))
        mix_ref[:, ATT_DIM:] = ((bg * cy) * gate_c).astype(BF16)

        qrow = lax.broadcasted_iota(jnp.int32, (n_q, 2 * WINDOW), 0)
        key = lax.broadcasted_iota(jnp.int32, (n_q, 2 * WINDOW), 1)
        q_tok = jnp.maximum((qrow & (SAMPLE_ROWS - 1)) - TOK0, 0)
        q_seq = (qrow >> LOG_SAMPLE_ROWS) & (SEQ_PER_GROUP - 1)
        new = key - WINDOW
        k_tok = (new & (SAMPLE_ROWS - 1)) - TOK0
        k_seq = new >> LOG_SAMPLE_ROWS
        cached = key < WINDOW
        dist = jnp.where(cached, WINDOW + q_tok - key, q_tok - k_tok)
        ok_new = (new >= 0) & (new < GROUP_ROWS) & (k_seq == q_seq) & (k_tok >= 0)
        visible = (dist >= 0) & (dist < WINDOW) & (cached | ok_new)
        slope = jnp.zeros((n_q, 2 * WINDOW), F32)
        sink = jnp.zeros((n_q, LANES), F32)
        srow = lax.broadcasted_iota(jnp.int32, (n_q, LANES), 0)
        for hd in range(N_HEADS):
            slope = jnp.where((qrow >> LOG_GROUP_ROWS) == hd, SLOPES[hd], slope)
            sink = jnp.where((srow >> LOG_GROUP_ROWS) == hd, sinks_ref[hd], sink)
        bias_ref[...] = jnp.where(visible, -(slope * dist.astype(F32)), NEG_INF)
        sinkcol[...] = sink
        kk[WINDOW:, :] = jnp.zeros((WINDOW, KV_DIM), BF16)
        vv[WINDOW:, :] = jnp.zeros((WINDOW, KV_DIM), BF16)

    g0 = pl.multiple_of(grp * GROUP_ROWS, GROUP_ROWS)
    w_g = jnp.concatenate([qe[hd, pl.ds(g0, GROUP_ROWS), :] for hd in range(N_HEADS)], axis=0)
    k16 = knew[pl.ds(g0, GROUP_ROWS), :]
    v16 = vnew[pl.ds(g0, GROUP_ROWS), :]
    kk[WINDOW:WINDOW + GROUP_ROWS, :] = k16.astype(BF16)
    vv[WINDOW:WINDOW + GROUP_ROWS, :] = v16.astype(BF16)
    q_seq = ((lax.broadcasted_iota(jnp.int32, (n_q, KV_DIM), 0) >> LOG_SAMPLE_ROWS)
             & (SEQ_PER_GROUP - 1))
    sub = lax.broadcasted_iota(jnp.int32, (SAMPLE_ROWS, KV_DIM), 0)
    sink = sinkcol[:, 0:1]
    o_grp = jnp.zeros((n_q, KV_DIM), F32)
    for s in range(SEQ_PER_GROUP):
        ck = ck_ref[s]
        cv = cv_ref[s]
        kk[0:WINDOW, :] = ck.astype(BF16)
        vv[0:WINDOW, :] = cv.astype(BF16)
        sc = lax.dot_general(w_g, kk[...], _NT, preferred_element_type=F32) + bias_ref[...]
        p = _softmax_rows(sc, sink).astype(BF16)
        o = jnp.dot(p, vv[...], preferred_element_type=F32)
        o_grp = jnp.where(q_seq == s, o, o_grp)
        for cache, new16, out_ref in ((ck, k16, kb_ref), (cv, v16, vb_ref)):
            shifted = pltpu.roll(cache, WINDOW - 4, axis=0)
            out_ref[s] = shifted
            new8 = new16[s * SAMPLE_ROWS:(s + 1) * SAMPLE_ROWS, :]
            out_ref[s, WINDOW - 8:WINDOW, :] = jnp.where(sub >= TOK0, new8, shifted[WINDOW - 8:, :])

    blk16 = lax.broadcasted_iota(jnp.int32, (GROUP_ROWS, KV_DIM), 1) >> LOG_HEAD_DIM
    for h in range(N_KV_HEADS):
        slab = jnp.zeros((GROUP_ROWS, KV_DIM), F32)
        for g in range(GQA_GROUP):
            hd = h * GQA_GROUP + g
            piece = jnp.where(blk16 == h, o_grp[hd * GROUP_ROWS:(hd + 1) * GROUP_ROWS, :], 0.0)
            slab = slab + pltpu.roll(piece, ((g - h) % GQA_GROUP) * HEAD_DIM, axis=1)
        gate = gate_a[pl.ds(g0, GROUP_ROWS), h * KV_DIM:(h + 1) * KV_DIM]
        mix_ref[pl.ds(g0, GROUP_ROWS), h * KV_DIM:(h + 1) * KV_DIM] = (slab * gate).astype(BF16)

    @pl.when(grp == n_groups - 1)
    def _():
        out = jnp.dot(mix_ref[...], wout_ref[...], preferred_element_type=F32)
        y_ref[...] = _layer_norm(alpha * x_ref[...] + out, g_ref[...], b_ref[...])


def _sample_layer(x8, st8, ck, cv, w_in, w_out, conv_w, sinks, ln_g, ln_b, alpha):
    n_rows, d = x8.shape
    n_seq = ck.shape[0]
    n_groups = n_seq // SEQ_PER_GROUP
    n_q = N_HEADS * GROUP_ROWS
    const2 = lambda g: (0, 0)
    cache_spec = pl.BlockSpec((SEQ_PER_GROUP, WINDOW, KV_DIM), lambda g: (g, 0, 0))
    kernel = functools.partial(_sample_kernel, n_rows=n_rows, alpha=alpha)
    return pl.pallas_call(
        kernel,
        grid=(n_groups,),
        in_specs=[
            pl.BlockSpec(memory_space=pltpu.SMEM),
            pl.BlockSpec((n_rows, d), const2),
            _resident((d, PROJ_DIM), 1),
            _resident((ATT_DIM + CONV_DIM, d), 1),
            pl.BlockSpec((3, CONV_DIM), const2),
            pl.BlockSpec((1, d), const2),
            pl.BlockSpec((1, d), const2),
            cache_spec,
            cache_spec,
            pl.BlockSpec((n_rows, CONV_DIM), const2),
        ],
        out_specs=[
            pl.BlockSpec((n_rows, d), const2),
            cache_spec,
            cache_spec,
            pl.BlockSpec((n_rows, CONV_DIM), const2),
        ],
        out_shape=[
            jax.ShapeDtypeStruct((n_rows, d), F32),
            jax.ShapeDtypeStruct(ck.shape, F32),
            jax.ShapeDtypeStruct(cv.shape, F32),
            jax.ShapeDtypeStruct((n_rows, CONV_DIM), F32),
        ],
        scratch_shapes=[
            pltpu.VMEM((N_HEADS, n_rows, KV_DIM), BF16),
            pltpu.VMEM((n_rows, KV_DIM), F32),
            pltpu.VMEM((n_rows, KV_DIM), F32),
            pltpu.VMEM((2 * WINDOW, KV_DIM), BF16),
            pltpu.VMEM((2 * WINDOW, KV_DIM), BF16),
            pltpu.VMEM((n_rows, ATT_DIM), F32),
            pltpu.VMEM((n_q, 2 * WINDOW), F32),
            pltpu.VMEM((n_q, LANES), F32),
            pltpu.VMEM((n_rows, ATT_DIM + CONV_DIM), BF16),
        ],
        compiler_params=pltpu.CompilerParams(
            dimension_semantics=("arbitrary",),
            vmem_limit_bytes=VMEM_LIMIT,
        ),
        name="sample_layer",
    )(sinks, x8, w_in, w_out, conv_w, ln_g, ln_b, ck, cv, st8)


def kernel(x_prompt, x_sample, cache_k, cache_v, state_conv, meta_tokens,
           w_in, conv_w, sinks, w_out, ln_g, ln_b):
    depth = w_in.shape[0]
    alpha = float((2 * depth) ** 0.25)
    batch, seq, d = x_prompt.shape
    n_seq, n_tok = x_sample.shape[:2]
    assert d == D_MODEL and seq % PROMPT_TILE == 0 and n_tok == SAMPLE_ROWS - TOK0
    assert meta_tokens.shape[0] == N_META and n_seq % SEQ_PER_GROUP == 0
    assert cache_k.shape[2] == WINDOW and state_conv.shape[2] == 2

    w_in_b = w_in.astype(BF16)
    w_out_b = w_out.astype(BF16)
    xp = x_prompt
    xh = jnp.concatenate([jnp.zeros((HEAD_PAD, d), F32), meta_tokens.astype(F32)], axis=0)
    xs = jnp.pad(x_sample, ((0, 0), (TOK0, 0), (0, 0))).reshape(n_seq * SAMPLE_ROWS, d)
    ck_all = cache_k.reshape(depth, n_seq, WINDOW, KV_DIM)
    cv_all = cache_v.reshape(depth, n_seq, WINDOW, KV_DIM)
    st_all = jnp.pad(state_conv, ((0, 0), (0, 0), (TOK0 - 2, SAMPLE_ROWS - TOK0), (0, 0)))
    st_all = st_all.reshape(depth, n_seq * SAMPLE_ROWS, CONV_DIM)

    kp, vp, cp, ks, vs, cs = [], [], [], [], [], []
    for l in range(depth):
        g = ln_g[l].reshape(1, d)
        bta = ln_b[l].reshape(1, d)
        xp, xh, k_last, v_last, c_last = _prompt_layer(
            xp, xh, w_in_b[l], w_out_b[l], conv_w[l], sinks[l], g, bta, alpha)
        kp.append(k_last.reshape(batch, WINDOW, N_KV_HEADS, HEAD_DIM))
        vp.append(v_last.reshape(batch, WINDOW, N_KV_HEADS, HEAD_DIM))
        cp.append(c_last[:, 6:8, :])
        xs, k_buf, v_buf, u8 = _sample_layer(
            xs, st_all[l], ck_all[l], cv_all[l], w_in_b[l], w_out_b[l], conv_w[l], sinks[l],
            g, bta, alpha)
        ks.append(k_buf.reshape(n_seq, WINDOW, N_KV_HEADS, HEAD_DIM))
        vs.append(v_buf.reshape(n_seq, WINDOW, N_KV_HEADS, HEAD_DIM))
        cs.append(u8.reshape(n_seq, SAMPLE_ROWS, CONV_DIM)[:, SAMPLE_ROWS - 2:, :])
    y_sample = xs.reshape(n_seq, SAMPLE_ROWS, d)[:, TOK0:, :]
    return (xp, y_sample, jnp.stack(kp), jnp.stack(vp), jnp.stack(cp),
            jnp.stack(ks), jnp.stack(vs), jnp.stack(cs))
```

```python
import functools

import numpy as np
import jax
import jax.numpy as jnp
from jax import lax
from jax.experimental import pallas as pl
from jax.experimental.pallas import tpu as pltpu

F32 = jnp.float32
BF16 = jnp.bfloat16

D_MODEL = 2048
N_META = 16
ATT_DIM = 1024
CONV_DIM = 1024
HEAD_DIM = 64
N_HEADS = 16
N_KV_HEADS = 4
GQA_GROUP = N_HEADS // N_KV_HEADS
KV_DIM = N_KV_HEADS * HEAD_DIM
WINDOW = 128
PROJ_DIM = 2 * ATT_DIM + 2 * KV_DIM + 4 * CONV_DIM
LN_EPS = 1e-5
NEG_INF = -1e30
Q_SCALE = HEAD_DIM ** -0.5

C_Q = 0
C_K = ATT_DIM
C_V = C_K + KV_DIM
C_GA = C_V + KV_DIM
C_B = C_GA + ATT_DIM
C_C = C_B + CONV_DIM
C_H = C_C + CONV_DIM
C_GC = C_H + CONV_DIM

LANES = 128
HEAD_PAD = WINDOW - N_META
PROMPT_TILE = 256
FILL_CHUNK = 256
SAMPLE_ROWS = 8
VMEM_LIMIT = 58 * 1024 * 1024

SLOPES = [float(np.float32(2.0 ** (-8.0 * (h + 1) / N_HEADS))) for h in range(N_HEADS)]

_NT = (((1,), (1,)), ((), ()))


def _silu(g):
    return g * (1.0 / (1.0 + jnp.exp(-g)))


def _softmax_rows(s, sink):
    m = jnp.maximum(jnp.max(s, axis=1, keepdims=True), sink)
    p = jnp.exp(s - m)
    denom = jnp.sum(p, axis=1, keepdims=True) + jnp.exp(sink - m)
    return p * (1.0 / denom)


def _softmax_cols(s, sink):
    m = jnp.maximum(jnp.max(s, axis=0, keepdims=True), sink)
    p = jnp.exp(s - m)
    denom = jnp.sum(p, axis=0, keepdims=True) + jnp.exp(sink - m)
    return p * (1.0 / denom)


def _layer_norm(z, g, b):
    mu = jnp.mean(z, axis=1, keepdims=True)
    zc = z - mu
    var = jnp.mean(zc * zc, axis=1, keepdims=True)
    return zc * lax.rsqrt(var + LN_EPS) * g + b


def _proj(xb, w_ref, c0, width):
    return jnp.dot(xb, w_ref[:, c0:c0 + width], preferred_element_type=F32)


def _store_k_variants(src, dst, r0, rows):
    low = lax.broadcasted_iota(jnp.int32, (rows, LANES), 1) < HEAD_DIM
    for cc in range(KV_DIM // LANES):
        col = src[:, cc * LANES:(cc + 1) * LANES]
        swapped = pltpu.roll(col, HEAD_DIM, axis=1)
        h_even, h_odd = 2 * cc, 2 * cc + 1
        dst[2 * h_even + 0, r0:r0 + rows, :] = jnp.where(low, col, 0.0).astype(BF16)
        dst[2 * h_even + 1, r0:r0 + rows, :] = jnp.where(low, 0.0, swapped).astype(BF16)
        dst[2 * h_odd + 0, r0:r0 + rows, :] = jnp.where(low, swapped, 0.0).astype(BF16)
        dst[2 * h_odd + 1, r0:r0 + rows, :] = jnp.where(low, 0.0, col).astype(BF16)


def _conv_chunk(xb, win_ref, cw_ref, ucar, mix_ref, rows, lo, width):
    cols = slice(lo, lo + width)
    bg = _proj(xb, win_ref, C_B + lo, width)
    u = _proj(xb, win_ref, C_C + lo, width) * _proj(xb, win_ref, C_H + lo, width)
    row = lax.broadcasted_iota(jnp.int32, (rows, width), 0)
    prev1 = ucar[7:8, cols]
    prev2 = ucar[6:7, cols]
    u1 = jnp.where(row == 0, prev1, pltpu.roll(u, 1, axis=0))
    u2 = jnp.where(row == 0, prev2, jnp.where(row == 1, prev1, pltpu.roll(u, 2, axis=0)))
    cy = cw_ref[0:1, cols] * u2 + cw_ref[1:2, cols] * u1 + cw_ref[2:3, cols] * u
    ucar[:, cols] = u[rows - 8:rows, :]
    gate_c = _silu(_proj(xb, win_ref, C_GC + lo, width))
    mix_ref[0:rows, ATT_DIM + lo:ATT_DIM + lo + width] = ((bg * cy) * gate_c).astype(BF16)


def _store_v_variants(src, dst, c0, rows):
    vt = src.T
    zeros = jnp.zeros((HEAD_DIM, rows), F32)
    for h in range(N_KV_HEADS):
        vh = vt[h * HEAD_DIM:(h + 1) * HEAD_DIM, :]
        dst[2 * h + 0, :, c0:c0 + rows] = jnp.concatenate([vh, zeros], axis=0).astype(BF16)
        dst[2 * h + 1, :, c0:c0 + rows] = jnp.concatenate([zeros, vh], axis=0).astype(BF16)


def _scores_pair(qb, kmask, sinks_ref, kvar, bias_ref, p_scr, r0, c):
    h = c // 2
    q2 = qb[r0:r0 + WINDOW, c * LANES:(c + 1) * LANES]
    for par in range(2):
        hd = 2 * c + par
        keys = kvar[2 * h + par, r0:r0 + 2 * WINDOW, :]
        s = lax.dot_general(keys, q2, _NT, preferred_element_type=F32)
        s = s + bias_ref[hd]
        if kmask is not None:
            s = s + kmask
        p_scr[hd] = _softmax_cols(s, sinks_ref[hd]).astype(BF16)


def _values_pair(gate_ref, vtvar, p_scr, mix_ref, r0, c):
    h = c // 2
    lanes = slice(c * LANES, (c + 1) * LANES)
    o_t = None
    for par in range(2):
        vals_t = vtvar[2 * h + par, :, r0:r0 + 2 * WINDOW]
        o = jnp.dot(vals_t, p_scr[2 * c + par], preferred_element_type=F32)
        o_t = o if o_t is None else o_t + o
    mix_ref[r0:r0 + WINDOW, lanes] = (o_t.T * gate_ref[r0:r0 + WINDOW, lanes]).astype(BF16)


def _gate_chunk(xb, win_ref, gate_ref, rows, lo, width):
    gate_ref[0:rows, lo:lo + width] = _silu(_proj(xb, win_ref, C_GA + lo, width))


def _prompt_rows(x, rows, kmask0, sinks_ref, win_ref, wout_ref, cw_ref, g_ref, b_ref,
                 kvar, vtvar, ucar, bias_ref, mix_ref, p_scr, gate_ref, alpha):
    xb = x.astype(BF16)
    hq = _proj(xb, win_ref, 0, C_GA)
    qb = (hq[:, C_Q:C_Q + ATT_DIM] * Q_SCALE).astype(BF16)
    kf = hq[:, C_K:C_K + KV_DIM]
    vf = hq[:, C_V:C_V + KV_DIM]
    _store_k_variants(kf, kvar, WINDOW, rows)
    _store_v_variants(vf, vtvar, WINDOW, rows)

    n_slabs = ATT_DIM // LANES
    n_blocks = rows // WINDOW
    fillers = [functools.partial(_gate_chunk, xb, win_ref, gate_ref, rows, lo, FILL_CHUNK)
               for lo in range(0, ATT_DIM, FILL_CHUNK)]
    fillers += [functools.partial(_conv_chunk, xb, win_ref, cw_ref, ucar, mix_ref, rows, lo,
                                  FILL_CHUNK) for lo in range(0, CONV_DIM, FILL_CHUNK)]
    units_per_filler = (n_blocks * n_slabs) // len(fillers)
    assert ATT_DIM // FILL_CHUNK <= n_slabs // units_per_filler
    for blk in range(n_blocks):
        r0 = blk * WINDOW
        for c in range(n_slabs):
            unit = blk * n_slabs + c
            if unit % units_per_filler == 0:
                fillers[unit // units_per_filler]()
            _scores_pair(qb, kmask0 if blk == 0 else None, sinks_ref, kvar, bias_ref, p_scr,
                         r0, c)
        for c in range(n_slabs):
            _values_pair(gate_ref, vtvar, p_scr, mix_ref, r0, c)

    out = jnp.dot(mix_ref[0:rows, :], wout_ref[...], preferred_element_type=F32)
    y = _layer_norm(alpha * x + out, g_ref[...], b_ref[...])

    for i in range(2 * N_KV_HEADS):
        kvar[i, 0:WINDOW, :] = kvar[i, rows:rows + WINDOW, :]
        vtvar[i, :, 0:WINDOW] = vtvar[i, :, rows:rows + WINDOW]
    return y, kf, vf


def _prompt_kernel(sinks_ref, x_ref, xh_ref, win_ref, wout_ref, cw_ref, g_ref, b_ref,
                   y_ref, yh_ref, kl_ref, vl_ref, cs_ref,
                   kvar, vtvar, khead, vhead, ucar, uhead, bias_ref, mix_ref, p_scr, gate_ref,
                   *, tm, alpha):
    b = pl.program_id(0)
    j = pl.program_id(1)
    n_tiles = pl.num_programs(1)
    shared = (sinks_ref, win_ref, wout_ref, cw_ref, g_ref, b_ref, kvar, vtvar, ucar, bias_ref,
              mix_ref, p_scr, gate_ref, alpha)
    key_row = lax.broadcasted_iota(jnp.int32, (2 * WINDOW, WINDOW), 0)

    @pl.when((b == 0) & (j == 0))
    def _():
        qi = lax.broadcasted_iota(jnp.int32, (2 * WINDOW, WINDOW), 1)
        dist = WINDOW + qi - key_row
        visible = (dist >= 0) & (dist < WINDOW)
        distf = dist.astype(F32)
        for hd in range(N_HEADS):
            bias_ref[hd] = jnp.where(visible, -(SLOPES[hd] * distf), NEG_INF)
        kvar[:, 0:WINDOW, :] = jnp.zeros((2 * N_KV_HEADS, WINDOW, LANES), BF16)
        vtvar[:, :, 0:WINDOW] = jnp.zeros((2 * N_KV_HEADS, LANES, WINDOW), BF16)
        ucar[...] = jnp.zeros(ucar.shape, F32)
        kmask = jnp.where(key_row < WINDOW + HEAD_PAD, NEG_INF, 0.0)
        yh, _, _ = _prompt_rows(xh_ref[...], WINDOW, kmask, *shared)
        hrow = lax.broadcasted_iota(jnp.int32, (WINDOW, D_MODEL), 0)
        yh_ref[...] = jnp.where(hrow >= HEAD_PAD, yh, 0.0)
        khead[...] = kvar[:, 0:WINDOW, :]
        vhead[...] = vtvar[:, :, 0:WINDOW]
        uhead[...] = ucar[...]

    @pl.when(j == 0)
    def _():
        kvar[:, 0:WINDOW, :] = khead[...]
        vtvar[:, :, 0:WINDOW] = vhead[...]
        ucar[...] = uhead[...]

    kmask = jnp.where(key_row < HEAD_PAD, jnp.where(j == 0, NEG_INF, 0.0), 0.0)
    y, kf, vf = _prompt_rows(x_ref[...], tm, kmask, *shared)
    y_ref[...] = y

    @pl.when(j == n_tiles - 1)
    def _():
        kl_ref[...] = kf[tm - WINDOW:tm, :]
        vl_ref[...] = vf[tm - WINDOW:tm, :]
        cs_ref[...] = ucar[...]


def _resident(shape, n_grid):
    zeros = (0,) * len(shape)
    if n_grid == 1:
        return pl.BlockSpec(shape, lambda g: zeros, pipeline_mode=pl.Buffered(1))
    return pl.BlockSpec(shape, lambda b, j: zeros, pipeline_mode=pl.Buffered(1))


def _prompt_layer(x, xh, w_in, w_out, conv_w, sinks, ln_g, ln_b, alpha):
    batch, seq, d = x.shape
    tm = PROMPT_TILE
    n_tiles = seq // tm
    const2 = lambda b, j: (0, 0)
    per_batch = lambda b, j: (b, 0, 0)
    kernel = functools.partial(_prompt_kernel, tm=tm, alpha=alpha)
    return pl.pallas_call(
        kernel,
        grid=(batch, n_tiles),
        in_specs=[
            pl.BlockSpec(memory_space=pltpu.SMEM),
            pl.BlockSpec((None, tm, d), lambda b, j: (b, j, 0)),
            pl.BlockSpec((WINDOW, d), const2),
            _resident((d, PROJ_DIM), 2),
            _resident((ATT_DIM + CONV_DIM, d), 2),
            pl.BlockSpec((3, CONV_DIM), const2),
            pl.BlockSpec((1, d), const2),
            pl.BlockSpec((1, d), const2),
        ],
        out_specs=[
            pl.BlockSpec((None, tm, d), lambda b, j: (b, j, 0)),
            pl.BlockSpec((WINDOW, d), const2),
            pl.BlockSpec((None, WINDOW, KV_DIM), per_batch),
            pl.BlockSpec((None, WINDOW, KV_DIM), per_batch),
            pl.BlockSpec((None, 8, CONV_DIM), per_batch),
        ],
        out_shape=[
            jax.ShapeDtypeStruct((batch, seq, d), F32),
            jax.ShapeDtypeStruct((WINDOW, d), F32),
            jax.ShapeDtypeStruct((batch, WINDOW, KV_DIM), F32),
            jax.ShapeDtypeStruct((batch, WINDOW, KV_DIM), F32),
            jax.ShapeDtypeStruct((batch, 8, CONV_DIM), F32),
        ],
        scratch_shapes=[
            pltpu.VMEM((2 * N_KV_HEADS, WINDOW + tm, LANES), BF16),
            pltpu.VMEM((2 * N_KV_HEADS, LANES, WINDOW + tm), BF16),
            pltpu.VMEM((2 * N_KV_HEADS, WINDOW, LANES), BF16),
            pltpu.VMEM((2 * N_KV_HEADS, LANES, WINDOW), BF16),
            pltpu.VMEM((8, CONV_DIM), F32),
            pltpu.VMEM((8, CONV_DIM), F32),
            pltpu.VMEM((N_HEADS, 2 * WINDOW, WINDOW), F32),
            pltpu.VMEM((tm, ATT_DIM + CONV_DIM), BF16),
            pltpu.VMEM((N_HEADS, 2 * WINDOW, WINDOW), BF16),
            pltpu.VMEM((tm, ATT_DIM), F32),
        ],
        compiler_params=pltpu.CompilerParams(
            dimension_semantics=("arbitrary", "arbitrary"),
            vmem_limit_bytes=VMEM_LIMIT,
        ),
        name="prompt_layer",
    )(sinks, x, xh, w_in, w_out, conv_w, ln_g, ln_b)


SEQ_PER_GROUP = 2
GROUP_ROWS = SEQ_PER_GROUP * SAMPLE_ROWS
TOK0 = SAMPLE_ROWS - 4
LOG_SAMPLE_ROWS = 3
LOG_GROUP_ROWS = 4
LOG_HEAD_DIM = 6
assert (1 << LOG_SAMPLE_ROWS, 1 << LOG_GROUP_ROWS, 1 << LOG_HEAD_DIM) == (
    SAMPLE_ROWS, GROUP_ROWS, HEAD_DIM)


def _sample_kernel(sinks_ref, x_ref, win_ref, wout_ref, cw_ref, g_ref, b_ref, ck_ref, cv_ref,
                   st_ref, y_ref, kb_ref, vb_ref, u_ref,
                   qe, knew, vnew, kk, vv, gate_a, bias_ref, sinkcol, mix_ref, *, n_rows, alpha):
    grp = pl.program_id(0)
    n_groups = pl.num_programs(0)
    n_q = N_HEADS * GROUP_ROWS
    lane_blk = lax.broadcasted_iota(jnp.int32, (n_rows, KV_DIM), 1) >> LOG_HEAD_DIM

    @pl.when(grp == 0)
    def _():
        x = x_ref[...]
        xb = x.astype(BF16)
        hq = _proj(xb, win_ref, 0, C_GA)
        q = hq[:, C_Q:C_Q + ATT_DIM] * Q_SCALE
        knew[...] = hq[:, C_K:C_K + KV_DIM]
        vnew[...] = hq[:, C_V:C_V + KV_DIM]
        for hd in range(N_HEADS):
            h, g = divmod(hd, GQA_GROUP)
            slab = q[:, h * KV_DIM:(h + 1) * KV_DIM]
            moved = pltpu.roll(slab, ((h - g) % GQA_GROUP) * HEAD_DIM, axis=1)
            qe[hd] = jnp.where(lane_blk == h, moved, 0.0).astype(BF16)
        gate_a[...] = _silu(_proj(xb, win_ref, C_GA, ATT_DIM))

        bg = _proj(xb, win_ref, C_B, CONV_DIM)
        u = _proj(xb, win_ref, C_C, CONV_DIM) * _proj(xb, win_ref, C_H, CONV_DIM)
        r8 = lax.broadcasted_iota(jnp.int32, (n_rows, CONV_DIM), 0) & (SAMPLE_ROWS - 1)
        is_state = (r8 >= TOK0 - 2) & (r8 < TOK0)
        u = jnp.where(is_state, st_ref[...], u)
        u_ref[...] = u
        cy = (cw_ref[0:1, :] * pltpu.roll(u, 2, axis=0) + cw_ref[1:2, :] * pltpu.roll(u, 1, axis=0)
              + cw_ref[2:3, :] * u)
        gate_c = _silu(_proj(xb, win_ref, C_GC, CONV_DIM))
        mix_ref[:, ATT_DIM:] = ((bg * cy) * gate_c).astype(BF16)

        qrow = lax.broadcasted_iota(jnp.int32, (n_q, 2 * WINDOW), 0)
        key = lax.broadcasted_iota(jnp.int32, (n_q, 2 * WINDOW), 1)
        q_tok = jnp.maximum((qrow & (SAMPLE_ROWS - 1)) - TOK0, 0)
        q_seq = (qrow >> LOG_SAMPLE_ROWS) & (SEQ_PER_GROUP - 1)
        new = key - WINDOW
        k_tok = (new & (SAMPLE_ROWS - 1)) - TOK0
        k_seq = new >> LOG_SAMPLE_ROWS
        cached = key < WINDOW
        dist = jnp.where(cached, WINDOW + q_tok - key, q_tok - k_tok)
        ok_new = (new >= 0) & (new < GROUP_ROWS) & (k_seq == q_seq) & (k_tok >= 0)
        visible = (dist >= 0) & (dist < WINDOW) & (cached | ok_new)
        slope = jnp.zeros((n_q, 2 * WINDOW), F32)
        sink = jnp.zeros((n_q, LANES), F32)
        srow = lax.broadcasted_iota(jnp.int32, (n_q, LANES), 0)
        for hd in range(N_HEADS):
            slope = jnp.where((qrow >> LOG_GROUP_ROWS) == hd, SLOPES[hd], slope)
            sink = jnp.where((srow >> LOG_GROUP_ROWS) == hd, sinks_ref[hd], sink)
        bias_ref[...] = jnp.where(visible, -(slope * dist.astype(F32)), NEG_INF)
        sinkcol[...] = sink
        kk[WINDOW:, :] = jnp.zeros((WINDOW, KV_DIM), BF16)
        vv[WINDOW:, :] = jnp.zeros((WINDOW, KV_DIM), BF16)

    g0 = pl.multiple_of(grp * GROUP_ROWS, GROUP_ROWS)
    w_g = jnp.concatenate([qe[hd, pl.ds(g0, GROUP_ROWS), :] for hd in range(N_HEADS)], axis=0)
    k16 = knew[pl.ds(g0, GROUP_ROWS), :]
    v16 = vnew[pl.ds(g0, GROUP_ROWS), :]
    kk[WINDOW:WINDOW + GROUP_ROWS, :] = k16.astype(BF16)
    vv[WINDOW:WINDOW + GROUP_ROWS, :] = v16.astype(BF16)
    q_seq = ((lax.broadcasted_iota(jnp.int32, (n_q, KV_DIM), 0) >> LOG_SAMPLE_ROWS)
             & (SEQ_PER_GROUP - 1))
    sub = lax.broadcasted_iota(jnp.int32, (SAMPLE_ROWS, KV_DIM), 0)
    sink = sinkcol[:, 0:1]
    o_grp = jnp.zeros((n_q, KV_DIM), F32)
    for s in range(SEQ_PER_GROUP):
        ck = ck_ref[s]
        cv = cv_ref[s]
        kk[0:WINDOW, :] = ck.astype(BF16)
        vv[0:WINDOW, :] = cv.astype(BF16)
        sc = lax.dot_general(w_g, kk[...], _NT, preferred_element_type=F32) + bias_ref[...]
        p = _softmax_rows(sc, sink).astype(BF16)
        o = jnp.dot(p, vv[...], preferred_element_type=F32)
        o_grp = jnp.where(q_seq == s, o, o_grp)
        for cache, new16, out_ref in ((ck, k16, kb_ref), (cv, v16, vb_ref)):
            shifted = pltpu.roll(cache, WINDOW - 4, axis=0)
            out_ref[s] = shifted
            new8 = new16[s * SAMPLE_ROWS:(s + 1) * SAMPLE_ROWS, :]
            out_ref[s, WINDOW - 8:WINDOW, :] = jnp.where(sub >= TOK0, new8, shifted[WINDOW - 8:, :])

    blk16 = lax.broadcasted_iota(jnp.int32, (GROUP_ROWS, KV_DIM), 1) >> LOG_HEAD_DIM
    for h in range(N_KV_HEADS):
        slab = jnp.zeros((GROUP_ROWS, KV_DIM), F32)
        for g in range(GQA_GROUP):
            hd = h * GQA_GROUP + g
            piece = jnp.where(blk16 == h, o_grp[hd * GROUP_ROWS:(hd + 1) * GROUP_ROWS, :], 0.0)
            slab = slab + pltpu.roll(piece, ((g - h) % GQA_GROUP) * HEAD_DIM, axis=1)
        gate = gate_a[pl.ds(g0, GROUP_ROWS), h * KV_DIM:(h + 1) * KV_DIM]
        mix_ref[pl.ds(g0, GROUP_ROWS), h * KV_DIM:(h + 1) * KV_DIM] = (slab * gate).astype(BF16)

    @pl.when(grp == n_groups - 1)
    def _():
        out = jnp.dot(mix_ref[...], wout_ref[...], preferred_element_type=F32)
        y_ref[...] = _layer_norm(alpha * x_ref[...] + out, g_ref[...], b_ref[...])


def _sample_layer(x8, st8, ck, cv, w_in, w_out, conv_w, sinks, ln_g, ln_b, alpha):
    n_rows, d = x8.shape
    n_seq = ck.shape[0]
    n_groups = n_seq // SEQ_PER_GROUP
    n_q = N_HEADS * GROUP_ROWS
    const2 = lambda g: (0, 0)
    cache_spec = pl.BlockSpec((SEQ_PER_GROUP, WINDOW, KV_DIM), lambda g: (g, 0, 0))
    kernel = functools.partial(_sample_kernel, n_rows=n_rows, alpha=alpha)
    return pl.pallas_call(
        kernel,
        grid=(n_groups,),
        in_specs=[
            pl.BlockSpec(memory_space=pltpu.SMEM),
            pl.BlockSpec((n_rows, d), const2),
            _resident((d, PROJ_DIM), 1),
            _resident((ATT_DIM + CONV_DIM, d), 1),
            pl.BlockSpec((3, CONV_DIM), const2),
            pl.BlockSpec((1, d), const2),
            pl.BlockSpec((1, d), const2),
            cache_spec,
            cache_spec,
            pl.BlockSpec((n_rows, CONV_DIM), const2),
        ],
        out_specs=[
            pl.BlockSpec((n_rows, d), const2),
            cache_spec,
            cache_spec,
            pl.BlockSpec((n_rows, CONV_DIM), const2),
        ],
        out_shape=[
            jax.ShapeDtypeStruct((n_rows, d), F32),
            jax.ShapeDtypeStruct(ck.shape, F32),
            jax.ShapeDtypeStruct(cv.shape, F32),
            jax.ShapeDtypeStruct((n_rows, CONV_DIM), F32),
        ],
        scratch_shapes=[
            pltpu.VMEM((N_HEADS, n_rows, KV_DIM), BF16),
            pltpu.VMEM((n_rows, KV_DIM), F32),
            pltpu.VMEM((n_rows, KV_DIM), F32),
            pltpu.VMEM((2 * WINDOW, KV_DIM), BF16),
            pltpu.VMEM((2 * WINDOW, KV_DIM), BF16),
            pltpu.VMEM((n_rows, ATT_DIM), F32),
            pltpu.VMEM((n_q, 2 * WINDOW), F32),
            pltpu.VMEM((n_q, LANES), F32),
            pltpu.VMEM((n_rows, ATT_DIM + CONV_DIM), BF16),
        ],
        compiler_params=pltpu.CompilerParams(
            dimension_semantics=("arbitrary",),
            vmem_limit_bytes=VMEM_LIMIT,
        ),
        name="sample_layer",
    )(sinks, x8, w_in, w_out, conv_w, ln_g, ln_b, ck, cv, st8)


def kernel(x_prompt, x_sample, cache_k, cache_v, state_conv, meta_tokens,
           w_in, conv_w, sinks, w_out, ln_g, ln_b):
    depth = w_in.shape[0]
    alpha = float((2 * depth) ** 0.25)
    batch, seq, d = x_prompt.shape
    n_seq, n_tok = x_sample.shape[:2]
    assert d == D_MODEL and seq % PROMPT_TILE == 0 and n_tok == SAMPLE_ROWS - TOK0
    assert meta_tokens.shape[0] == N_META and n_seq % SEQ_PER_GROUP == 0
    assert cache_k.shape[2] == WINDOW and state_conv.shape[2] == 2

    w_in_b = w_in.astype(BF16)
    w_out_b = w_out.astype(BF16)
    xp = x_prompt
    xh = jnp.concatenate([jnp.zeros((HEAD_PAD, d), F32), meta_tokens.astype(F32)], axis=0)
    xs = jnp.pad(x_sample, ((0, 0), (TOK0, 0), (0, 0))).reshape(n_seq * SAMPLE_ROWS, d)
    ck_all = cache_k.reshape(depth, n_seq, WINDOW, KV_DIM)
    cv_all = cache_v.reshape(depth, n_seq, WINDOW, KV_DIM)
    st_all = jnp.pad(state_conv, ((0, 0), (0, 0), (TOK0 - 2, SAMPLE_ROWS - TOK0), (0, 0)))
    st_all = st_all.reshape(depth, n_seq * SAMPLE_ROWS, CONV_DIM)

    kp, vp, cp, ks, vs, cs = [], [], [], [], [], []
    for l in range(depth):
        g = ln_g[l].reshape(1, d)
        bta = ln_b[l].reshape(1, d)
        xp, xh, k_last, v_last, c_last = _prompt_layer(
            xp, xh, w_in_b[l], w_out_b[l], conv_w[l], sinks[l], g, bta, alpha)
        kp.append(k_last.reshape(batch, WINDOW, N_KV_HEADS, HEAD_DIM))
        vp.append(v_last.reshape(batch, WINDOW, N_KV_HEADS, HEAD_DIM))
        cp.append(c_last[:, 6:8, :])
        xs, k_buf, v_buf, u8 = _sample_layer(
            xs, st_all[l], ck_all[l], cv_all[l], w_in_b[l], w_out_b[l], conv_w[l], sinks[l],
            g, bta, alpha)
        ks.append(k_buf.reshape(n_seq, WINDOW, N_KV_HEADS, HEAD_DIM))
        vs.append(v_buf.reshape(n_seq, WINDOW, N_KV_HEADS, HEAD_DIM))
        cs.append(u8.reshape(n_seq, SAMPLE_ROWS, CONV_DIM)[:, SAMPLE_ROWS - 2:, :])
    y_sample = xs.reshape(n_seq, SAMPLE_ROWS, d)[:, TOK0:, :]
    return (xp, y_sample, jnp.stack(kp), jnp.stack(vp), jnp.stack(cp),
            jnp.stack(ks), jnp.stack(vs), jnp.stack(cs))
```

```python
import functools

import numpy as np
import jax
import jax.numpy as jnp
from jax import lax
from jax.experimental import pallas as pl
from jax.experimental.pallas import tpu as pltpu

F32 = jnp.float32
BF16 = jnp.bfloat16

D_MODEL = 2048
N_META = 16
ATT_DIM = 1024
CONV_DIM = 1024
HEAD_DIM = 64
N_HEADS = 16
N_KV_HEADS = 4
GQA_GROUP = N_HEADS // N_KV_HEADS
KV_DIM = N_KV_HEADS * HEAD_DIM
WINDOW = 128
PROJ_DIM = 2 * ATT_DIM + 2 * KV_DIM + 4 * CONV_DIM
LN_EPS = 1e-5
NEG_INF = -1e30
Q_SCALE = HEAD_DIM ** -0.5

C_Q = 0
C_K = ATT_DIM
C_V = C_K + KV_DIM
C_GA = C_V + KV_DIM
C_B = C_GA + ATT_DIM
C_C = C_B + CONV_DIM
C_H = C_C + CONV_DIM
C_GC = C_H + CONV_DIM

LANES = 128
HEAD_PAD = WINDOW - N_META
PROMPT_TILE = 256
FILL_CHUNK = 256
W_CHUNK = 256
W_STAGE = 128
W_SLOTS = 2
SAMPLE_ROWS = 8
VMEM_LIMIT = 58 * 1024 * 1024

SLOPES = [float(np.float32(2.0 ** (-8.0 * (h + 1) / N_HEADS))) for h in range(N_HEADS)]

_NT = (((1,), (1,)), ((), ()))


def _silu(g):
    return g * (1.0 / (1.0 + jnp.exp(-g)))


def _softmax_rows(s, sink):
    m = jnp.maximum(jnp.max(s, axis=1, keepdims=True), sink)
    p = jnp.exp(s - m)
    denom = jnp.sum(p, axis=1, keepdims=True) + jnp.exp(sink - m)
    return p * (1.0 / denom)


def _softmax_cols(s, sink):
    m = jnp.maximum(jnp.max(s, axis=0, keepdims=True), sink)
    p = jnp.exp(s - m)
    denom = jnp.sum(p, axis=0, keepdims=True) + jnp.exp(sink - m)
    return p * (1.0 / denom)


def _layer_norm(z, g, b):
    mu = jnp.mean(z, axis=1, keepdims=True)
    zc = z - mu
    var = jnp.mean(zc * zc, axis=1, keepdims=True)
    return zc * lax.rsqrt(var + LN_EPS) * g + b


def _proj(xb, w_ref, c0, width):
    first, n = c0 // W_CHUNK, width // W_CHUNK
    assert first * W_CHUNK == c0 and n * W_CHUNK == width
    parts = [jnp.dot(xb, w_ref[first + i], preferred_element_type=F32) for i in range(n)]
    return parts[0] if n == 1 else jnp.concatenate(parts, axis=1)


def _load_weights(layer, win_hbm, wout_hbm, win_ref, wout_ref, stage, sem, dyn_zero):
    chunks = [(win_hbm, win_ref, c0) for c0 in range(0, PROJ_DIM, W_STAGE)]
    chunks += [(wout_hbm, wout_ref, c0) for c0 in range(0, D_MODEL, W_STAGE)]

    def copy(i):
        src, _, c0 = chunks[i]
        slot = i % W_SLOTS
        return pltpu.make_async_copy(src.at[layer, :, pl.ds(c0, W_STAGE)], stage.at[slot],
                                     sem.at[slot])

    for i in range(W_SLOTS):
        copy(i).start()
    for i, (_, dst, c0) in enumerate(chunks):
        copy(i).wait()
        lane0 = c0 % W_CHUNK
        dst[c0 // W_CHUNK + dyn_zero, :, lane0:lane0 + W_STAGE] = stage[i % W_SLOTS].astype(BF16)
        if i + W_SLOTS < len(chunks):
            copy(i + W_SLOTS).start()


def _store_k_variants(src, dst, r0, rows):
    low = lax.broadcasted_iota(jnp.int32, (rows, LANES), 1) < HEAD_DIM
    for cc in range(KV_DIM // LANES):
        col = src[:, cc * LANES:(cc + 1) * LANES]
        swapped = pltpu.roll(col, HEAD_DIM, axis=1)
        h_even, h_odd = 2 * cc, 2 * cc + 1
        dst[2 * h_even + 0, r0:r0 + rows, :] = jnp.where(low, col, 0.0).astype(BF16)
        dst[2 * h_even + 1, r0:r0 + rows, :] = jnp.where(low, 0.0, swapped).astype(BF16)
        dst[2 * h_odd + 0, r0:r0 + rows, :] = jnp.where(low, swapped, 0.0).astype(BF16)
        dst[2 * h_odd + 1, r0:r0 + rows, :] = jnp.where(low, 0.0, col).astype(BF16)


def _conv_chunk(xb, win_ref, cw_ref, ucar, mix_ref, rows, lo, width):
    cols = slice(lo, lo + width)
    bg = _proj(xb, win_ref, C_B + lo, width)
    u = _proj(xb, win_ref, C_C + lo, width) * _proj(xb, win_ref, C_H + lo, width)
    row = lax.broadcasted_iota(jnp.int32, (rows, width), 0)
    prev1 = ucar[7:8, cols]
    prev2 = ucar[6:7, cols]
    u1 = jnp.where(row == 0, prev1, pltpu.roll(u, 1, axis=0))
    u2 = jnp.where(row == 0, prev2, jnp.where(row == 1, prev1, pltpu.roll(u, 2, axis=0)))
    cy = cw_ref[0:1, cols] * u2 + cw_ref[1:2, cols] * u1 + cw_ref[2:3, cols] * u
    ucar[:, cols] = u[rows - 8:rows, :]
    gate_c = _silu(_proj(xb, win_ref, C_GC + lo, width))
    mix_ref[0:rows, ATT_DIM + lo:ATT_DIM + lo + width] = ((bg * cy) * gate_c).astype(BF16)


def _store_v_variants(src, dst, c0, rows):
    vt = src.T
    zeros = jnp.zeros((HEAD_DIM, rows), F32)
    for h in range(N_KV_HEADS):
        vh = vt[h * HEAD_DIM:(h + 1) * HEAD_DIM, :]
        dst[2 * h + 0, :, c0:c0 + rows] = jnp.concatenate([vh, zeros], axis=0).astype(BF16)
        dst[2 * h + 1, :, c0:c0 + rows] = jnp.concatenate([zeros, vh], axis=0).astype(BF16)


def _scores_pair(layer, qb, kmask, sinks_ref, kvar, bias_ref, p_scr, r0, c):
    h = c // 2
    q2 = qb[r0:r0 + WINDOW, c * LANES:(c + 1) * LANES]
    for par in range(2):
        hd = 2 * c + par
        keys = kvar[2 * h + par, r0:r0 + 2 * WINDOW, :]
        s = lax.dot_general(keys, q2, _NT, preferred_element_type=F32)
        s = s + SLOPES[hd] * bias_ref[...]
        if kmask is not None:
            s = s + kmask
        p_scr[hd] = _softmax_cols(s, sinks_ref[layer, hd]).astype(BF16)


def _values_pair(gate_ref, vtvar, p_scr, mix_ref, r0, c):
    h = c // 2
    lanes = slice(c * LANES, (c + 1) * LANES)
    o_t = None
    for par in range(2):
        vals_t = vtvar[2 * h + par, :, r0:r0 + 2 * WINDOW]
        o = jnp.dot(vals_t, p_scr[2 * c + par], preferred_element_type=F32)
        o_t = o if o_t is None else o_t + o
    mix_ref[r0:r0 + WINDOW, lanes] = (o_t.T * gate_ref[r0:r0 + WINDOW, lanes]).astype(BF16)


def _gate_chunk(xb, win_ref, gate_ref, rows, lo, width):
    gate_ref[0:rows, lo:lo + width] = _silu(_proj(xb, win_ref, C_GA + lo, width))


def _prompt_rows(x, rows, kmask0, layer, sinks_ref, win_ref, wout_ref, cw_ref, g_ref, b_ref,
                 kvar, vtvar, ucar, bias_ref, mix_ref, p_scr, gate_ref, alpha):
    xb = x.astype(BF16)
    hq = _proj(xb, win_ref, 0, C_GA)
    qb = (hq[:, C_Q:C_Q + ATT_DIM] * Q_SCALE).astype(BF16)
    kf = hq[:, C_K:C_K + KV_DIM]
    vf = hq[:, C_V:C_V + KV_DIM]
    _store_k_variants(kf, kvar, WINDOW, rows)
    _store_v_variants(vf, vtvar, WINDOW, rows)

    n_slabs = ATT_DIM // LANES
    n_blocks = rows // WINDOW
    fillers = [functools.partial(_gate_chunk, xb, win_ref, gate_ref, rows, lo, FILL_CHUNK)
               for lo in range(0, ATT_DIM, FILL_CHUNK)]
    fillers += [functools.partial(_conv_chunk, xb, win_ref, cw_ref, ucar, mix_ref, rows, lo,
                                  FILL_CHUNK) for lo in range(0, CONV_DIM, FILL_CHUNK)]
    units_per_filler = (n_blocks * n_slabs) // len(fillers)
    assert ATT_DIM // FILL_CHUNK <= n_slabs // units_per_filler
    for blk in range(n_blocks):
        r0 = blk * WINDOW
        for c in range(n_slabs):
            unit = blk * n_slabs + c
            if unit % units_per_filler == 0:
                fillers[unit // units_per_filler]()
            _scores_pair(layer, qb, kmask0 if blk == 0 else None, sinks_ref, kvar, bias_ref,
                         p_scr, r0, c)
        for c in range(n_slabs):
            _values_pair(gate_ref, vtvar, p_scr, mix_ref, r0, c)

    out = _proj(mix_ref[0:rows, :], wout_ref, 0, D_MODEL)
    y = _layer_norm(alpha * x + out, g_ref[...], b_ref[...])

    for i in range(2 * N_KV_HEADS):
        kvar[i, 0:WINDOW, :] = kvar[i, rows:rows + WINDOW, :]
        vtvar[i, :, 0:WINDOW] = vtvar[i, :, rows:rows + WINDOW]
    return y, kf, vf


def _prompt_kernel(sinks_ref, x_ref, xh_ref, win_hbm, wout_hbm, cw_ref, g_ref, b_ref,
                   y_ref, yh_ref, kl_ref, vl_ref, cs_ref,
                   win_ref, wout_ref, wstage, wsem,
                   kvar, vtvar, khead, vhead, ucar, uhead, bias_ref, mix_ref, p_scr, gate_ref,
                   *, layer, tm, alpha):
    b = pl.program_id(0)
    j = pl.program_id(1)
    n_tiles = pl.num_programs(1)
    shared = (layer, sinks_ref, win_ref, wout_ref, cw_ref, g_ref, b_ref, kvar, vtvar, ucar,
              bias_ref, mix_ref, p_scr, gate_ref, alpha)
    key_row = lax.broadcasted_iota(jnp.int32, (2 * WINDOW, WINDOW), 0)

    @pl.when((b == 0) & (j == 0))
    def _():
        _load_weights(layer, win_hbm, wout_hbm, win_ref, wout_ref, wstage, wsem, j)
        qi = lax.broadcasted_iota(jnp.int32, (2 * WINDOW, WINDOW), 1)
        dist = WINDOW + qi - key_row
        visible = (dist >= 0) & (dist < WINDOW)
        bias_ref[...] = jnp.where(visible, -dist.astype(F32), NEG_INF)
        kvar[:, 0:WINDOW, :] = jnp.zeros((2 * N_KV_HEADS, WINDOW, LANES), BF16)
        vtvar[:, :, 0:WINDOW] = jnp.zeros((2 * N_KV_HEADS, LANES, WINDOW), BF16)
        ucar[...] = jnp.zeros(ucar.shape, F32)
        kmask = jnp.where(key_row < WINDOW + HEAD_PAD, NEG_INF, 0.0)
        yh, _, _ = _prompt_rows(xh_ref[...], WINDOW, kmask, *shared)
        hrow = lax.broadcasted_iota(jnp.int32, (WINDOW, D_MODEL), 0)
        yh_ref[...] = jnp.where(hrow >= HEAD_PAD, yh, 0.0)
        khead[...] = kvar[:, 0:WINDOW, :]
        vhead[...] = vtvar[:, :, 0:WINDOW]
        uhead[...] = ucar[...]

    @pl.when(j == 0)
    def _():
        kvar[:, 0:WINDOW, :] = khead[...]
        vtvar[:, :, 0:WINDOW] = vhead[...]
        ucar[...] = uhead[...]

    kmask = jnp.where(key_row < HEAD_PAD, jnp.where(j == 0, NEG_INF, 0.0), 0.0)
    y, kf, vf = _prompt_rows(x_ref[...], tm, kmask, *shared)
    y_ref[...] = y

    @pl.when(j == n_tiles - 1)
    def _():
        kl_ref[...] = kf[tm - WINDOW:tm, :]
        vl_ref[...] = vf[tm - WINDOW:tm, :]
        cs_ref[...] = ucar[...]


def _weight_scratch():
    return [
        pltpu.VMEM((PROJ_DIM // W_CHUNK, D_MODEL, W_CHUNK), BF16),
        pltpu.VMEM((D_MODEL // W_CHUNK, ATT_DIM + CONV_DIM, W_CHUNK), BF16),
        pltpu.VMEM((W_SLOTS, D_MODEL, W_STAGE), F32),
        pltpu.SemaphoreType.DMA((W_SLOTS,)),
    ]


def _prompt_layer(layer, x, xh, w_in, w_out, conv_w, sinks, ln_g, ln_b, alpha):
    batch, seq, d = x.shape
    tm = PROMPT_TILE
    n_tiles = seq // tm
    const2 = lambda b, j: (0, 0)
    this_layer = lambda b, j: (layer, 0, 0)
    per_batch = lambda b, j: (b, 0, 0)
    kernel = functools.partial(_prompt_kernel, layer=layer, tm=tm, alpha=alpha)
    return pl.pallas_call(
        kernel,
        grid=(batch, n_tiles),
        in_specs=[
            pl.BlockSpec(memory_space=pltpu.SMEM),
            pl.BlockSpec((None, tm, d), lambda b, j: (b, j, 0)),
            pl.BlockSpec((WINDOW, d), const2, pipeline_mode=pl.Buffered(1)),
            pl.BlockSpec(memory_space=pl.ANY),
            pl.BlockSpec(memory_space=pl.ANY),
            pl.BlockSpec((None, 3, CONV_DIM), this_layer),
            pl.BlockSpec((None, 1, d), this_layer),
            pl.BlockSpec((None, 1, d), this_layer),
        ],
        out_specs=[
            pl.BlockSpec((None, tm, d), lambda b, j: (b, j, 0)),
            pl.BlockSpec((WINDOW, d), const2),
            pl.BlockSpec((None, WINDOW, KV_DIM), per_batch),
            pl.BlockSpec((None, WINDOW, KV_DIM), per_batch),
            pl.BlockSpec((None, 8, CONV_DIM), per_batch),
        ],
        out_shape=[
            jax.ShapeDtypeStruct((batch, seq, d), F32),
            jax.ShapeDtypeStruct((WINDOW, d), F32),
            jax.ShapeDtypeStruct((batch, WINDOW, KV_DIM), F32),
            jax.ShapeDtypeStruct((batch, WINDOW, KV_DIM), F32),
            jax.ShapeDtypeStruct((batch, 8, CONV_DIM), F32),
        ],
        scratch_shapes=_weight_scratch() + [
            pltpu.VMEM((2 * N_KV_HEADS, WINDOW + tm, LANES), BF16),
            pltpu.VMEM((2 * N_KV_HEADS, LANES, WINDOW + tm), BF16),
            pltpu.VMEM((2 * N_KV_HEADS, WINDOW, LANES), BF16),
            pltpu.VMEM((2 * N_KV_HEADS, LANES, WINDOW), BF16),
            pltpu.VMEM((8, CONV_DIM), F32),
            pltpu.VMEM((8, CONV_DIM), F32),
            pltpu.VMEM((2 * WINDOW, WINDOW), F32),
            pltpu.VMEM((tm, ATT_DIM + CONV_DIM), BF16),
            pltpu.VMEM((N_HEADS, 2 * WINDOW, WINDOW), BF16),
            pltpu.VMEM((tm, ATT_DIM), F32),
        ],
        compiler_params=pltpu.CompilerParams(
            dimension_semantics=("arbitrary", "arbitrary"),
            vmem_limit_bytes=VMEM_LIMIT,
        ),
        name="prompt_layer",
    )(sinks, x, xh, w_in, w_out, conv_w, ln_g, ln_b)


SEQ_PER_GROUP = 2
GROUP_ROWS = SEQ_PER_GROUP * SAMPLE_ROWS
TOK0 = SAMPLE_ROWS - 4
LOG_SAMPLE_ROWS = 3
LOG_GROUP_ROWS = 4
LOG_HEAD_DIM = 6
assert (1 << LOG_SAMPLE_ROWS, 1 << LOG_GROUP_ROWS, 1 << LOG_HEAD_DIM) == (
    SAMPLE_ROWS, GROUP_ROWS, HEAD_DIM)


def _sample_kernel(sinks_ref, x_ref, win_hbm, wout_hbm, cw_ref, g_ref, b_ref, ck_ref, cv_ref,
                   st_ref, y_ref, kb_ref, vb_ref, u_ref,
                   win_ref, wout_ref, wstage, wsem,
                   qe, knew, vnew, kk, vv, gate_a, bias_ref, sinkcol, mix_ref,
                   *, layer, n_rows, alpha):
    grp = pl.program_id(0)
    n_groups = pl.num_programs(0)
    n_q = N_HEADS * GROUP_ROWS
    lane_blk = lax.broadcasted_iota(jnp.int32, (n_rows, KV_DIM), 1) >> LOG_HEAD_DIM

    @pl.when(grp == 0)
    def _():
        _load_weights(layer, win_hbm, wout_hbm, win_ref, wout_ref, wstage, wsem, grp)
        x = x_ref[...]
        xb = x.astype(BF16)
        hq = _proj(xb, win_ref, 0, C_GA)
        q = hq[:, C_Q:C_Q + ATT_DIM] * Q_SCALE
        knew[...] = hq[:, C_K:C_K + KV_DIM]
        vnew[...] = hq[:, C_V:C_V + KV_DIM]
        for hd in range(N_HEADS):
            h, g = divmod(hd, GQA_GROUP)
            slab = q[:, h * KV_DIM:(h + 1) * KV_DIM]
            moved = pltpu.roll(slab, ((h - g) % GQA_GROUP) * HEAD_DIM, axis=1)
            qe[hd] = jnp.where(lane_blk == h, moved, 0.0).astype(BF16)
        gate_a[...] = _silu(_proj(xb, win_ref, C_GA, ATT_DIM))

        bg = _proj(xb, win_ref, C_B, CONV_DIM)
        u = _proj(xb, win_ref, C_C, CONV_DIM) * _proj(xb, win_ref, C_H, CONV_DIM)
        r8 = lax.broadcasted_iota(jnp.int32, (n_rows, CONV_DIM), 0) & (SAMPLE_ROWS - 1)
        is_state = (r8 >= TOK0 - 2) & (r8 < TOK0)
        u = jnp.where(is_state, st_ref[...], u)
        u_ref[...] = u
        cy = (cw_ref[0:1, :] * pltpu.roll(u, 2, axis=0) + cw_ref[1:2, :] * pltpu.roll(u, 1, axis=0)
              + cw_ref[2:3, :] * u)
        gate_c = _silu(_proj(xb, win_ref, C_GC, CONV_DIM))
        mix_ref[:, ATT_DIM:] = ((bg * cy) * gate_c).astype(BF16)

        qrow = lax.broadcasted_iota(jnp.int32, (n_q, 2 * WINDOW), 0)
        key = lax.broadcasted_iota(jnp.int32, (n_q, 2 * WINDOW), 1)
        q_tok = jnp.maximum((qrow & (SAMPLE_ROWS - 1)) - TOK0, 0)
        q_seq = (qrow >> LOG_SAMPLE_ROWS) & (SEQ_PER_GROUP - 1)
        new = key - WINDOW
        k_tok = (new & (SAMPLE_ROWS - 1)) - TOK0
        k_seq = new >> LOG_SAMPLE_ROWS
        cached = key < WINDOW
        dist = jnp.where(cached, WINDOW + q_tok - key, q_tok - k_tok)
        ok_new = (new >= 0) & (new < GROUP_ROWS) & (k_seq == q_seq) & (k_tok >= 0)
        visible = (dist >= 0) & (dist < WINDOW) & (cached | ok_new)
        slope = jnp.zeros((n_q, 2 * WINDOW), F32)
        sink = jnp.zeros((n_q, LANES), F32)
        srow = lax.broadcasted_iota(jnp.int32, (n_q, LANES), 0)
        for hd in range(N_HEADS):
            slope = jnp.where((qrow >> LOG_GROUP_ROWS) == hd, SLOPES[hd], slope)
            sink = jnp.where((srow >> LOG_GROUP_ROWS) == hd, sinks_ref[layer, hd], sink)
        bias_ref[...] = jnp.where(visible, -(slope * dist.astype(F32)), NEG_INF)
        sinkcol[...] = sink
        kk[WINDOW:, :] = jnp.zeros((WINDOW, KV_DIM), BF16)
        vv[WINDOW:, :] = jnp.zeros((WINDOW, KV_DIM), BF16)

    g0 = pl.multiple_of(grp * GROUP_ROWS, GROUP_ROWS)
    w_g = jnp.concatenate([qe[hd, pl.ds(g0, GROUP_ROWS), :] for hd in range(N_HEADS)], axis=0)
    k16 = knew[pl.ds(g0, GROUP_ROWS), :]
    v16 = vnew[pl.ds(g0, GROUP_ROWS), :]
    kk[WINDOW:WINDOW + GROUP_ROWS, :] = k16.astype(BF16)
    vv[WINDOW:WINDOW + GROUP_ROWS, :] = v16.astype(BF16)
    q_seq = ((lax.broadcasted_iota(jnp.int32, (n_q, KV_DIM), 0) >> LOG_SAMPLE_ROWS)
             & (SEQ_PER_GROUP - 1))
    sub = lax.broadcasted_iota(jnp.int32, (SAMPLE_ROWS, KV_DIM), 0)
    sink = sinkcol[:, 0:1]
    o_grp = jnp.zeros((n_q, KV_DIM), F32)
    for s in range(SEQ_PER_GROUP):
        ck = ck_ref[s]
        cv = cv_ref[s]
        kk[0:WINDOW, :] = ck.astype(BF16)
        vv[0:WINDOW, :] = cv.astype(BF16)
        sc = lax.dot_general(w_g, kk[...], _NT, preferred_element_type=F32) + bias_ref[...]
        p = _softmax_rows(sc, sink).astype(BF16)
        o = jnp.dot(p, vv[...], preferred_element_type=F32)
        o_grp = jnp.where(q_seq == s, o, o_grp)
        for cache, new16, out_ref in ((ck, k16, kb_ref), (cv, v16, vb_ref)):
            shifted = pltpu.roll(cache, WINDOW - 4, axis=0)
            out_ref[s] = shifted
            new8 = new16[s * SAMPLE_ROWS:(s + 1) * SAMPLE_ROWS, :]
            out_ref[s, WINDOW - 8:WINDOW, :] = jnp.where(sub >= TOK0, new8, shifted[WINDOW - 8:, :])

    blk16 = lax.broadcasted_iota(jnp.int32, (GROUP_ROWS, KV_DIM), 1) >> LOG_HEAD_DIM
    for h in range(N_KV_HEADS):
        slab = jnp.zeros((GROUP_ROWS, KV_DIM), F32)
        for g in range(GQA_GROUP):
            hd = h * GQA_GROUP + g
            piece = jnp.where(blk16 == h, o_grp[hd * GROUP_ROWS:(hd + 1) * GROUP_ROWS, :], 0.0)
            slab = slab + pltpu.roll(piece, ((g - h) % GQA_GROUP) * HEAD_DIM, axis=1)
        gate = gate_a[pl.ds(g0, GROUP_ROWS), h * KV_DIM:(h + 1) * KV_DIM]
        mix_ref[pl.ds(g0, GROUP_ROWS), h * KV_DIM:(h + 1) * KV_DIM] = (slab * gate).astype(BF16)

    @pl.when(grp == n_groups - 1)
    def _():
        out = _proj(mix_ref[...], wout_ref, 0, D_MODEL)
        y_ref[...] = _layer_norm(alpha * x_ref[...] + out, g_ref[...], b_ref[...])


def _sample_layer(layer, x8, st8, ck, cv, w_in, w_out, conv_w, sinks, ln_g, ln_b, alpha):
    n_rows, d = x8.shape
    n_seq = ck.shape[0]
    this_layer = lambda g: (layer, 0, 0)
    n_groups = n_seq // SEQ_PER_GROUP
    n_q = N_HEADS * GROUP_ROWS
    const2 = lambda g: (0, 0)
    cache_spec = pl.BlockSpec((SEQ_PER_GROUP, WINDOW, KV_DIM), lambda g: (g, 0, 0))
    kernel = functools.partial(_sample_kernel, layer=layer, n_rows=n_rows, alpha=alpha)
    return pl.pallas_call(
        kernel,
        grid=(n_groups,),
        in_specs=[
            pl.BlockSpec(memory_space=pltpu.SMEM),
            pl.BlockSpec((n_rows, d), const2),
            pl.BlockSpec(memory_space=pl.ANY),
            pl.BlockSpec(memory_space=pl.ANY),
            pl.BlockSpec((None, 3, CONV_DIM), this_layer),
            pl.BlockSpec((None, 1, d), this_layer),
            pl.BlockSpec((None, 1, d), this_layer),
            cache_spec,
            cache_spec,
            pl.BlockSpec((n_rows, CONV_DIM), const2),
        ],
        out_specs=[
            pl.BlockSpec((n_rows, d), const2),
            cache_spec,
            cache_spec,
            pl.BlockSpec((n_rows, CONV_DIM), const2),
        ],
        out_shape=[
            jax.ShapeDtypeStruct((n_rows, d), F32),
            jax.ShapeDtypeStruct(ck.shape, F32),
            jax.ShapeDtypeStruct(cv.shape, F32),
            jax.ShapeDtypeStruct((n_rows, CONV_DIM), F32),
        ],
        scratch_shapes=_weight_scratch() + [
            pltpu.VMEM((N_HEADS, n_rows, KV_DIM), BF16),
            pltpu.VMEM((n_rows, KV_DIM), F32),
            pltpu.VMEM((n_rows, KV_DIM), F32),
            pltpu.VMEM((2 * WINDOW, KV_DIM), BF16),
            pltpu.VMEM((2 * WINDOW, KV_DIM), BF16),
            pltpu.VMEM((n_rows, ATT_DIM), F32),
            pltpu.VMEM((n_q, 2 * WINDOW), F32),
            pltpu.VMEM((n_q, LANES), F32),
            pltpu.VMEM((n_rows, ATT_DIM + CONV_DIM), BF16),
        ],
        compiler_params=pltpu.CompilerParams(
            dimension_semantics=("arbitrary",),
            vmem_limit_bytes=VMEM_LIMIT,
        ),
        name="sample_layer",
    )(sinks, x8, w_in, w_out, conv_w, ln_g, ln_b, ck, cv, st8)


def kernel(x_prompt, x_sample, cache_k, cache_v, state_conv, meta_tokens,
           w_in, conv_w, sinks, w_out, ln_g, ln_b):
    depth = w_in.shape[0]
    alpha = float((2 * depth) ** 0.25)
    batch, seq, d = x_prompt.shape
    n_seq, n_tok = x_sample.shape[:2]
    assert d == D_MODEL and seq % PROMPT_TILE == 0 and n_tok == SAMPLE_ROWS - TOK0
    assert meta_tokens.shape[0] == N_META and n_seq % SEQ_PER_GROUP == 0
    assert cache_k.shape[2] == WINDOW and state_conv.shape[2] == 2

    ln_g3 = ln_g.reshape(depth, 1, d)
    ln_b3 = ln_b.reshape(depth, 1, d)
    xp = x_prompt
    xh = jnp.concatenate([jnp.zeros((HEAD_PAD, d), F32), meta_tokens.astype(F32)], axis=0)
    xs = jnp.pad(x_sample, ((0, 0), (TOK0, 0), (0, 0))).reshape(n_seq * SAMPLE_ROWS, d)
    ck_all = cache_k.reshape(depth, n_seq, WINDOW, KV_DIM)
    cv_all = cache_v.reshape(depth, n_seq, WINDOW, KV_DIM)
    st_all = jnp.pad(state_conv, ((0, 0), (0, 0), (TOK0 - 2, SAMPLE_ROWS - TOK0), (0, 0)))
    st_all = st_all.reshape(depth, n_seq * SAMPLE_ROWS, CONV_DIM)

    kp, vp, cp, ks, vs, cs = [], [], [], [], [], []
    for l in range(depth):
        xp, xh, k_last, v_last, c_last = _prompt_layer(
            l, xp, xh, w_in, w_out, conv_w, sinks, ln_g3, ln_b3, alpha)
        kp.append(k_last.reshape(batch, WINDOW, N_KV_HEADS, HEAD_DIM))
        vp.append(v_last.reshape(batch, WINDOW, N_KV_HEADS, HEAD_DIM))
        cp.append(c_last[:, 6:8, :])
        xs, k_buf, v_buf, u8 = _sample_layer(
            l, xs, st_all[l], ck_all[l], cv_all[l], w_in, w_out, conv_w, sinks, ln_g3, ln_b3,
            alpha)
        ks.append(k_buf.reshape(n_seq, WINDOW, N_KV_HEADS, HEAD_DIM))
        vs.append(v_buf.reshape(n_seq, WINDOW, N_KV_HEADS, HEAD_DIM))
        cs.append(u8.reshape(n_seq, SAMPLE_ROWS, CONV_DIM)[:, SAMPLE_ROWS - 2:, :])
    y_sample = xs.reshape(n_seq, SAMPLE_ROWS, d)[:, TOK0:, :]
    return (xp, y_sample, jnp.stack(kp), jnp.stack(vp), jnp.stack(cp),
            jnp.stack(ks), jnp.stack(vs), jnp.stack(cs))
```

```python
import functools

import numpy as np
import jax
import jax.numpy as jnp
from jax import lax
from jax.experimental import pallas as pl
from jax.experimental.pallas import tpu as pltpu

F32 = jnp.float32
BF16 = jnp.bfloat16

D_MODEL = 2048
N_META = 16
ATT_DIM = 1024
CONV_DIM = 1024
HEAD_DIM = 64
N_HEADS = 16
N_KV_HEADS = 4
GQA_GROUP = N_HEADS // N_KV_HEADS
KV_DIM = N_KV_HEADS * HEAD_DIM
WINDOW = 128
PROJ_DIM = 2 * ATT_DIM + 2 * KV_DIM + 4 * CONV_DIM
LN_EPS = 1e-5
NEG_INF = -1e30
Q_SCALE = HEAD_DIM ** -0.5

C_Q = 0
C_K = ATT_DIM
C_V = C_K + KV_DIM
C_GA = C_V + KV_DIM
C_B = C_GA + ATT_DIM
C_C = C_B + CONV_DIM
C_H = C_C + CONV_DIM
C_GC = C_H + CONV_DIM

LANES = 128
HEAD_PAD = WINDOW - N_META
PROMPT_TILE = 256
FILL_CHUNK = 256
SAMPLE_ROWS = 8
VMEM_LIMIT = 58 * 1024 * 1024

SLOPES = [float(np.float32(2.0 ** (-8.0 * (h + 1) / N_HEADS))) for h in range(N_HEADS)]

_NT = (((1,), (1,)), ((), ()))


def _silu(g):
    return g * (1.0 / (1.0 + jnp.exp(-g)))


def _softmax_rows(s, sink):
    m = jnp.maximum(jnp.max(s, axis=1, keepdims=True), sink)
    p = jnp.exp(s - m)
    denom = jnp.sum(p, axis=1, keepdims=True) + jnp.exp(sink - m)
    return p * (1.0 / denom)


def _softmax_cols(s, sink):
    m = jnp.maximum(jnp.max(s, axis=0, keepdims=True), sink)
    p = jnp.exp(s - m)
    denom = jnp.sum(p, axis=0, keepdims=True) + jnp.exp(sink - m)
    return p * (1.0 / denom)


def _layer_norm(z, g, b):
    mu = jnp.mean(z, axis=1, keepdims=True)
    zc = z - mu
    var = jnp.mean(zc * zc, axis=1, keepdims=True)
    return zc * lax.rsqrt(var + LN_EPS) * g + b


def _proj(xb, w_ref, c0, width):
    return jnp.dot(xb, w_ref[:, c0:c0 + width], preferred_element_type=F32)


def _store_k_variants(src, dst, r0, rows):
    low = lax.broadcasted_iota(jnp.int32, (rows, LANES), 1) < HEAD_DIM
    for cc in range(KV_DIM // LANES):
        col = src[:, cc * LANES:(cc + 1) * LANES]
        swapped = pltpu.roll(col, HEAD_DIM, axis=1)
        h_even, h_odd = 2 * cc, 2 * cc + 1
        dst[2 * h_even + 0, r0:r0 + rows, :] = jnp.where(low, col, 0.0).astype(BF16)
        dst[2 * h_even + 1, r0:r0 + rows, :] = jnp.where(low, 0.0, swapped).astype(BF16)
        dst[2 * h_odd + 0, r0:r0 + rows, :] = jnp.where(low, swapped, 0.0).astype(BF16)
        dst[2 * h_odd + 1, r0:r0 + rows, :] = jnp.where(low, 0.0, col).astype(BF16)


def _conv_chunk(xb, win_ref, cw_ref, ucar, mix_ref, rows, lo, width):
    cols = slice(lo, lo + width)
    bg = _proj(xb, win_ref, C_B + lo, width)
    u = _proj(xb, win_ref, C_C + lo, width) * _proj(xb, win_ref, C_H + lo, width)
    row = lax.broadcasted_iota(jnp.int32, (rows, width), 0)
    prev1 = ucar[7:8, cols]
    prev2 = ucar[6:7, cols]
    u1 = jnp.where(row == 0, prev1, pltpu.roll(u, 1, axis=0))
    u2 = jnp.where(row == 0, prev2, jnp.where(row == 1, prev1, pltpu.roll(u, 2, axis=0)))
    cy = cw_ref[0:1, cols] * u2 + cw_ref[1:2, cols] * u1 + cw_ref[2:3, cols] * u
    ucar[:, cols] = u[rows - 8:rows, :]
    gate_c = _silu(_proj(xb, win_ref, C_GC + lo, width))
    mix_ref[0:rows, ATT_DIM + lo:ATT_DIM + lo + width] = ((bg * cy) * gate_c).astype(BF16)


def _store_v_variants(src, dst, c0, rows):
    vt = src.T
    zeros = jnp.zeros((HEAD_DIM, rows), F32)
    for h in range(N_KV_HEADS):
        vh = vt[h * HEAD_DIM:(h + 1) * HEAD_DIM, :]
        dst[2 * h + 0, :, c0:c0 + rows] = jnp.concatenate([vh, zeros], axis=0).astype(BF16)
        dst[2 * h + 1, :, c0:c0 + rows] = jnp.concatenate([zeros, vh], axis=0).astype(BF16)


def _scores_pair(layer, qb, kmask, sinks_ref, kvar, bias_ref, p_scr, r0, c):
    h = c // 2
    q2 = qb[r0:r0 + WINDOW, c * LANES:(c + 1) * LANES]
    for par in range(2):
        hd = 2 * c + par
        keys = kvar[2 * h + par, r0:r0 + 2 * WINDOW, :]
        s = lax.dot_general(keys, q2, _NT, preferred_element_type=F32)
        s = s + SLOPES[hd] * bias_ref[...]
        if kmask is not None:
            s = s + kmask
        p_scr[hd] = _softmax_cols(s, sinks_ref[layer, hd]).astype(BF16)


def _values_pair(gate_ref, vtvar, p_scr, mix_ref, r0, c):
    h = c // 2
    lanes = slice(c * LANES, (c + 1) * LANES)
    o_t = None
    for par in range(2):
        vals_t = vtvar[2 * h + par, :, r0:r0 + 2 * WINDOW]
        o = jnp.dot(vals_t, p_scr[2 * c + par], preferred_element_type=F32)
        o_t = o if o_t is None else o_t + o
    mix_ref[r0:r0 + WINDOW, lanes] = (o_t.T * gate_ref[r0:r0 + WINDOW, lanes]).astype(BF16)


def _gate_chunk(xb, win_ref, gate_ref, rows, lo, width):
    gate_ref[0:rows, lo:lo + width] = _silu(_proj(xb, win_ref, C_GA + lo, width))


def _prompt_rows(x, rows, kmask0, layer, sinks_ref, win_ref, wout_ref, cw_ref, g_ref, b_ref,
                 kvar, vtvar, ucar, bias_ref, mix_ref, p_scr, gate_ref, alpha):
    xb = x.astype(BF16)
    hq = _proj(xb, win_ref, 0, C_GA)
    qb = (hq[:, C_Q:C_Q + ATT_DIM] * Q_SCALE).astype(BF16)
    kf = hq[:, C_K:C_K + KV_DIM]
    vf = hq[:, C_V:C_V + KV_DIM]
    _store_k_variants(kf, kvar, WINDOW, rows)
    _store_v_variants(vf, vtvar, WINDOW, rows)

    n_slabs = ATT_DIM // LANES
    n_blocks = rows // WINDOW
    fillers = [functools.partial(_gate_chunk, xb, win_ref, gate_ref, rows, lo, FILL_CHUNK)
               for lo in range(0, ATT_DIM, FILL_CHUNK)]
    fillers += [functools.partial(_conv_chunk, xb, win_ref, cw_ref, ucar, mix_ref, rows, lo,
                                  FILL_CHUNK) for lo in range(0, CONV_DIM, FILL_CHUNK)]
    units_per_filler = (n_blocks * n_slabs) // len(fillers)
    assert ATT_DIM // FILL_CHUNK <= n_slabs // units_per_filler
    for blk in range(n_blocks):
        r0 = blk * WINDOW
        for c in range(n_slabs):
            unit = blk * n_slabs + c
            if unit % units_per_filler == 0:
                fillers[unit // units_per_filler]()
            _scores_pair(layer, qb, kmask0 if blk == 0 else None, sinks_ref, kvar, bias_ref,
                         p_scr, r0, c)
        for c in range(n_slabs):
            _values_pair(gate_ref, vtvar, p_scr, mix_ref, r0, c)

    out = jnp.dot(mix_ref[0:rows, :], wout_ref[...], preferred_element_type=F32)
    y = _layer_norm(alpha * x + out, g_ref[...], b_ref[...])

    for i in range(2 * N_KV_HEADS):
        kvar[i, 0:WINDOW, :] = kvar[i, rows:rows + WINDOW, :]
        vtvar[i, :, 0:WINDOW] = vtvar[i, :, rows:rows + WINDOW]
    return y, kf, vf


def _prompt_kernel(sinks_ref, x_ref, xh_ref, win_ref, wout_ref, cw_ref, g_ref, b_ref,
                   y_ref, yh_ref, kl_ref, vl_ref, cs_ref,
                   kvar, vtvar, khead, vhead, ucar, uhead, bias_ref, mix_ref, p_scr, gate_ref,
                   *, layer, tm, alpha):
    b = pl.program_id(0)
    j = pl.program_id(1)
    n_tiles = pl.num_programs(1)
    shared = (layer, sinks_ref, win_ref, wout_ref, cw_ref, g_ref, b_ref, kvar, vtvar, ucar,
              bias_ref, mix_ref, p_scr, gate_ref, alpha)
    key_row = lax.broadcasted_iota(jnp.int32, (2 * WINDOW, WINDOW), 0)

    @pl.when((b == 0) & (j == 0))
    def _():
        qi = lax.broadcasted_iota(jnp.int32, (2 * WINDOW, WINDOW), 1)
        dist = WINDOW + qi - key_row
        visible = (dist >= 0) & (dist < WINDOW)
        bias_ref[...] = jnp.where(visible, -dist.astype(F32), NEG_INF)
        kvar[:, 0:WINDOW, :] = jnp.zeros((2 * N_KV_HEADS, WINDOW, LANES), BF16)
        vtvar[:, :, 0:WINDOW] = jnp.zeros((2 * N_KV_HEADS, LANES, WINDOW), BF16)
        ucar[...] = jnp.zeros(ucar.shape, F32)
        kmask = jnp.where(key_row < WINDOW + HEAD_PAD, NEG_INF, 0.0)
        yh, _, _ = _prompt_rows(xh_ref[...], WINDOW, kmask, *shared)
        hrow = lax.broadcasted_iota(jnp.int32, (WINDOW, D_MODEL), 0)
        yh_ref[...] = jnp.where(hrow >= HEAD_PAD, yh, 0.0)
        khead[...] = kvar[:, 0:WINDOW, :]
        vhead[...] = vtvar[:, :, 0:WINDOW]
        uhead[...] = ucar[...]

    @pl.when(j == 0)
    def _():
        kvar[:, 0:WINDOW, :] = khead[...]
        vtvar[:, :, 0:WINDOW] = vhead[...]
        ucar[...] = uhead[...]

    kmask = jnp.where(key_row < HEAD_PAD, jnp.where(j == 0, NEG_INF, 0.0), 0.0)
    y, kf, vf = _prompt_rows(x_ref[...], tm, kmask, *shared)
    y_ref[...] = y

    @pl.when(j == n_tiles - 1)
    def _():
        kl_ref[...] = kf[tm - WINDOW:tm, :]
        vl_ref[...] = vf[tm - WINDOW:tm, :]
        cs_ref[...] = ucar[...]


def _resident(shape, index_map):
    return pl.BlockSpec(shape, index_map, pipeline_mode=pl.Buffered(1))


def _prompt_layer(layer, x, xh, w_in, w_out, conv_w, sinks, ln_g, ln_b, alpha):
    batch, seq, d = x.shape
    tm = PROMPT_TILE
    n_tiles = seq // tm
    const2 = lambda b, j: (0, 0)
    this_layer = lambda b, j: (layer, 0, 0)
    per_batch = lambda b, j: (b, 0, 0)
    kernel = functools.partial(_prompt_kernel, layer=layer, tm=tm, alpha=alpha)
    return pl.pallas_call(
        kernel,
        grid=(batch, n_tiles),
        in_specs=[
            pl.BlockSpec(memory_space=pltpu.SMEM),
            pl.BlockSpec((None, tm, d), lambda b, j: (b, j, 0)),
            _resident((WINDOW, d), const2),
            _resident((None, d, PROJ_DIM), this_layer),
            _resident((None, ATT_DIM + CONV_DIM, d), this_layer),
            pl.BlockSpec((None, 3, CONV_DIM), this_layer),
            pl.BlockSpec((None, 1, d), this_layer),
            pl.BlockSpec((None, 1, d), this_layer),
        ],
        out_specs=[
            pl.BlockSpec((None, tm, d), lambda b, j: (b, j, 0)),
            pl.BlockSpec((WINDOW, d), const2),
            pl.BlockSpec((None, WINDOW, KV_DIM), per_batch),
            pl.BlockSpec((None, WINDOW, KV_DIM), per_batch),
            pl.BlockSpec((None, 8, CONV_DIM), per_batch),
        ],
        out_shape=[
            jax.ShapeDtypeStruct((batch, seq, d), F32),
            jax.ShapeDtypeStruct((WINDOW, d), F32),
            jax.ShapeDtypeStruct((batch, WINDOW, KV_DIM), F32),
            jax.ShapeDtypeStruct((batch, WINDOW, KV_DIM), F32),
            jax.ShapeDtypeStruct((batch, 8, CONV_DIM), F32),
        ],
        scratch_shapes=[
            pltpu.VMEM((2 * N_KV_HEADS, WINDOW + tm, LANES), BF16),
            pltpu.VMEM((2 * N_KV_HEADS, LANES, WINDOW + tm), BF16),
            pltpu.VMEM((2 * N_KV_HEADS, WINDOW, LANES), BF16),
            pltpu.VMEM((2 * N_KV_HEADS, LANES, WINDOW), BF16),
            pltpu.VMEM((8, CONV_DIM), F32),
            pltpu.VMEM((8, CONV_DIM), F32),
            pltpu.VMEM((2 * WINDOW, WINDOW), F32),
            pltpu.VMEM((tm, ATT_DIM + CONV_DIM), BF16),
            pltpu.VMEM((N_HEADS, 2 * WINDOW, WINDOW), BF16),
            pltpu.VMEM((tm, ATT_DIM), F32),
        ],
        compiler_params=pltpu.CompilerParams(
            dimension_semantics=("arbitrary", "arbitrary"),
            vmem_limit_bytes=VMEM_LIMIT,
        ),
        name="prompt_layer",
    )(sinks, x, xh, w_in, w_out, conv_w, ln_g, ln_b)


SEQ_PER_GROUP = 2
GROUP_ROWS = SEQ_PER_GROUP * SAMPLE_ROWS
N_NEW = 4
TOK0 = SAMPLE_ROWS - N_NEW
LOG_SAMPLE_ROWS = 3
LOG_GROUP_ROWS = 4
LOG_HEAD_DIM = 6
assert (1 << LOG_SAMPLE_ROWS, 1 << LOG_GROUP_ROWS, 1 << LOG_HEAD_DIM) == (
    SAMPLE_ROWS, GROUP_ROWS, HEAD_DIM)


def _sample_kernel(sinks_ref, x_ref, win_ref, wout_ref, cw_ref, g_ref, b_ref, ck_ref, cv_ref,
                   st_ref, kb_prev, vb_prev, y_ref, kb_ref, vb_ref, u_ref,
                   qe, knew, vnew, gate_a, bias_ref, sinkcol, mix_ref, *, layer, n_rows, alpha):
    del kb_prev, vb_prev
    grp = pl.program_id(0)
    n_groups = pl.num_programs(0)
    n_q = N_HEADS * GROUP_ROWS
    lane_blk = lax.broadcasted_iota(jnp.int32, (n_rows, KV_DIM), 1) >> LOG_HEAD_DIM

    @pl.when(grp == 0)
    def _():
        x = x_ref[...]
        xb = x.astype(BF16)
        hq = _proj(xb, win_ref, 0, C_GA)
        q = hq[:, C_Q:C_Q + ATT_DIM] * Q_SCALE
        knew[...] = hq[:, C_K:C_K + KV_DIM]
        vnew[...] = hq[:, C_V:C_V + KV_DIM]
        for hd in range(N_HEADS):
            h, g = divmod(hd, GQA_GROUP)
            slab = q[:, h * KV_DIM:(h + 1) * KV_DIM]
            moved = pltpu.roll(slab, ((h - g) % GQA_GROUP) * HEAD_DIM, axis=1)
            qe[hd] = jnp.where(lane_blk == h, moved, 0.0).astype(BF16)
        gate_a[...] = _silu(_proj(xb, win_ref, C_GA, ATT_DIM))

        bg = _proj(xb, win_ref, C_B, CONV_DIM)
        u = _proj(xb, win_ref, C_C, CONV_DIM) * _proj(xb, win_ref, C_H, CONV_DIM)
        r8 = lax.broadcasted_iota(jnp.int32, (n_rows, CONV_DIM), 0) & (SAMPLE_ROWS - 1)
        is_state = (r8 >= TOK0 - 2) & (r8 < TOK0)
        u = jnp.where(is_state, st_ref[...], u)
        u_ref[...] = u
        cy = (cw_ref[0:1, :] * pltpu.roll(u, 2, axis=0) + cw_ref[1:2, :] * pltpu.roll(u, 1, axis=0)
              + cw_ref[2:3, :] * u)
        gate_c = _silu(_proj(xb, win_ref, C_GC, CONV_DIM))
        mix_ref[:, ATT_DIM:] = ((bg * cy) * gate_c).astype(BF16)

        qrow = lax.broadcasted_iota(jnp.int32, (n_q, 2 * WINDOW), 0)
        key = lax.broadcasted_iota(jnp.int32, (n_q, 2 * WINDOW), 1)
        q_tok = jnp.maximum((qrow & (SAMPLE_ROWS - 1)) - TOK0, 0)
        q_seq = (qrow >> LOG_SAMPLE_ROWS) & (SEQ_PER_GROUP - 1)
        new = key - WINDOW
        k_tok = (new & (SAMPLE_ROWS - 1)) - TOK0
        k_seq = new >> LOG_SAMPLE_ROWS
        cached = key < WINDOW
        dist = jnp.where(cached, WINDOW + q_tok - key, q_tok - k_tok)
        ok_new = (new >= 0) & (new < GROUP_ROWS) & (k_seq == q_seq) & (k_tok >= 0)
        visible = (dist >= 0) & (dist < WINDOW) & (cached | ok_new)
        slope = jnp.zeros((n_q, 2 * WINDOW), F32)
        sink = jnp.zeros((n_q, LANES), F32)
        srow = lax.broadcasted_iota(jnp.int32, (n_q, LANES), 0)
        for hd in range(N_HEADS):
            slope = jnp.where((qrow >> LOG_GROUP_ROWS) == hd, SLOPES[hd], slope)
            sink = jnp.where((srow >> LOG_GROUP_ROWS) == hd, sinks_ref[layer, hd], sink)
        bias_ref[...] = jnp.where(visible, -(slope * dist.astype(F32)), NEG_INF)
        sinkcol[...] = sink

    g0 = pl.multiple_of(grp * GROUP_ROWS, GROUP_ROWS)
    w_g = jnp.concatenate([qe[hd, pl.ds(g0, GROUP_ROWS), :] for hd in range(N_HEADS)], axis=0)
    pad_rows = jnp.zeros((WINDOW - GROUP_ROWS, KV_DIM), F32)
    k_new = jnp.concatenate([knew[pl.ds(g0, GROUP_ROWS), :], pad_rows], axis=0)
    v_new = jnp.concatenate([vnew[pl.ds(g0, GROUP_ROWS), :], pad_rows], axis=0)
    k_new_b = k_new.astype(BF16)
    v_new_b = v_new.astype(BF16)
    k_new_t = k_new.T
    v_new_t = v_new.T
    q_seq = ((lax.broadcasted_iota(jnp.int32, (n_q, KV_DIM), 0) >> LOG_SAMPLE_ROWS)
             & (SEQ_PER_GROUP - 1))
    newest = lax.broadcasted_iota(jnp.int32, (KV_DIM, WINDOW), 1) >= WINDOW - N_NEW
    sink = sinkcol[:, 0:1]
    o_grp = jnp.zeros((n_q, KV_DIM), F32)
    for s in range(SEQ_PER_GROUP):
        ck_t = ck_ref[s]
        cv_t = cv_ref[s]
        sc = jnp.concatenate(
            [jnp.dot(w_g, ck_t.astype(BF16), preferred_element_type=F32),
             lax.dot_general(w_g, k_new_b, _NT, preferred_element_type=F32)], axis=1)
        p = _softmax_rows(sc + bias_ref[...], sink).astype(BF16)
        o = (lax.dot_general(p[:, :WINDOW], cv_t.astype(BF16), _NT, preferred_element_type=F32)
             + jnp.dot(p[:, WINDOW:], v_new_b, preferred_element_type=F32))
        o_grp = jnp.where(q_seq == s, o, o_grp)
        to_tail = WINDOW - N_NEW - (s * SAMPLE_ROWS + TOK0)
        for cache_t, new_t, out_ref in ((ck_t, k_new_t, kb_ref), (cv_t, v_new_t, vb_ref)):
            out_ref[s] = jnp.where(newest, pltpu.roll(new_t, to_tail, axis=1),
                                   pltpu.roll(cache_t, WINDOW - N_NEW, axis=1))

    blk16 = lax.broadcasted_iota(jnp.int32, (GROUP_ROWS, KV_DIM), 1) >> LOG_HEAD_DIM
    for h in range(N_KV_HEADS):
        slab = jnp.zeros((GROUP_ROWS, KV_DIM), F32)
        for g in range(GQA_GROUP):
            hd = h * GQA_GROUP + g
            piece = jnp.where(blk16 == h, o_grp[hd * GROUP_ROWS:(hd + 1) * GROUP_ROWS, :], 0.0)
            slab = slab + pltpu.roll(piece, ((g - h) % GQA_GROUP) * HEAD_DIM, axis=1)
        gate = gate_a[pl.ds(g0, GROUP_ROWS), h * KV_DIM:(h + 1) * KV_DIM]
        mix_ref[pl.ds(g0, GROUP_ROWS), h * KV_DIM:(h + 1) * KV_DIM] = (slab * gate).astype(BF16)

    @pl.when(grp == n_groups - 1)
    def _():
        out = jnp.dot(mix_ref[...], wout_ref[...], preferred_element_type=F32)
        y_ref[...] = _layer_norm(alpha * x_ref[...] + out, g_ref[...], b_ref[...])


def _sample_layer(layer, x8, st_all, ck_all, cv_all, kb_all, vb_all, w_in, w_out, conv_w, sinks,
                  ln_g, ln_b, alpha):
    n_rows, d = x8.shape
    depth, n_seq = ck_all.shape[:2]
    this_layer = lambda g: (layer, 0, 0)
    n_groups = n_seq // SEQ_PER_GROUP
    n_q = N_HEADS * GROUP_ROWS
    const2 = lambda g: (0, 0)
    cache_spec = pl.BlockSpec((None, SEQ_PER_GROUP, KV_DIM, WINDOW), lambda g: (layer, g, 0, 0))
    kernel = functools.partial(_sample_kernel, layer=layer, n_rows=n_rows, alpha=alpha)
    operands = [sinks, x8, w_in, w_out, conv_w, ln_g, ln_b, ck_all, cv_all, st_all, kb_all, vb_all]
    in_specs = [
        pl.BlockSpec(memory_space=pltpu.SMEM),
        pl.BlockSpec((n_rows, d), const2),
        _resident((None, d, PROJ_DIM), this_layer),
        _resident((None, ATT_DIM + CONV_DIM, d), this_layer),
        pl.BlockSpec((None, 3, CONV_DIM), this_layer),
        pl.BlockSpec((None, 1, d), this_layer),
        pl.BlockSpec((None, 1, d), this_layer),
        cache_spec,
        cache_spec,
        pl.BlockSpec((None, n_rows, CONV_DIM), this_layer),
        pl.BlockSpec(memory_space=pl.ANY),
        pl.BlockSpec(memory_space=pl.ANY),
    ]
    aliases = {len(operands) - 2: 1, len(operands) - 1: 2} if layer > 0 else {}
    return pl.pallas_call(
        kernel,
        grid=(n_groups,),
        in_specs=in_specs,
        out_specs=[
            pl.BlockSpec((n_rows, d), const2),
            cache_spec,
            cache_spec,
            pl.BlockSpec((n_rows, CONV_DIM), const2),
        ],
        out_shape=[
            jax.ShapeDtypeStruct((n_rows, d), F32),
            jax.ShapeDtypeStruct(ck_all.shape, F32),
            jax.ShapeDtypeStruct(cv_all.shape, F32),
            jax.ShapeDtypeStruct((n_rows, CONV_DIM), F32),
        ],
        input_output_aliases=aliases,
        scratch_shapes=[
            pltpu.VMEM((N_HEADS, n_rows, KV_DIM), BF16),
            pltpu.VMEM((n_rows, KV_DIM), F32),
            pltpu.VMEM((n_rows, KV_DIM), F32),
            pltpu.VMEM((n_rows, ATT_DIM), F32),
            pltpu.VMEM((n_q, 2 * WINDOW), F32),
            pltpu.VMEM((n_q, LANES), F32),
            pltpu.VMEM((n_rows, ATT_DIM + CONV_DIM), BF16),
        ],
        compiler_params=pltpu.CompilerParams(
            dimension_semantics=("arbitrary",),
            vmem_limit_bytes=VMEM_LIMIT,
        ),
        name="sample_layer",
    )(*operands)


def kernel(x_prompt, x_sample, cache_k, cache_v, state_conv, meta_tokens,
           w_in, conv_w, sinks, w_out, ln_g, ln_b):
    depth = w_in.shape[0]
    alpha = float((2 * depth) ** 0.25)
    batch, seq, d = x_prompt.shape
    n_seq, n_tok = x_sample.shape[:2]
    assert d == D_MODEL and seq % PROMPT_TILE == 0 and n_tok == SAMPLE_ROWS - TOK0
    assert meta_tokens.shape[0] == N_META and n_seq % SEQ_PER_GROUP == 0
    assert cache_k.shape[2] == WINDOW and state_conv.shape[2] == 2

    w_in = w_in.astype(BF16)
    w_out = w_out.astype(BF16)
    ln_g3 = ln_g.reshape(depth, 1, d)
    ln_b3 = ln_b.reshape(depth, 1, d)
    xp = x_prompt
    xh = jnp.concatenate([jnp.zeros((HEAD_PAD, d), F32), meta_tokens.astype(F32)], axis=0)
    xs = jnp.pad(x_sample, ((0, 0), (TOK0, 0), (0, 0))).reshape(n_seq * SAMPLE_ROWS, d)
    ck_all = jnp.transpose(cache_k, (0, 1, 3, 4, 2)).reshape(depth, n_seq, KV_DIM, WINDOW)
    cv_all = jnp.transpose(cache_v, (0, 1, 3, 4, 2)).reshape(depth, n_seq, KV_DIM, WINDOW)
    st_all = jnp.pad(state_conv, ((0, 0), (0, 0), (TOK0 - 2, SAMPLE_ROWS - TOK0), (0, 0)))
    st_all = st_all.reshape(depth, n_seq * SAMPLE_ROWS, CONV_DIM)

    kp, vp, cp, cs = [], [], [], []
    kb_all, vb_all = ck_all, cv_all
    for l in range(depth):
        xp, xh, k_last, v_last, c_last = _prompt_layer(
            l, xp, xh, w_in, w_out, conv_w, sinks, ln_g3, ln_b3, alpha)
        kp.append(k_last.reshape(batch, WINDOW, N_KV_HEADS, HEAD_DIM))
        vp.append(v_last.reshape(batch, WINDOW, N_KV_HEADS, HEAD_DIM))
        cp.append(c_last[:, 6:8, :])
        xs, kb_all, vb_all, u8 = _sample_layer(
            l, xs, st_all, ck_all, cv_all, kb_all, vb_all, w_in, w_out, conv_w, sinks, ln_g3,
            ln_b3, alpha)
        cs.append(u8.reshape(n_seq, SAMPLE_ROWS, CONV_DIM)[:, SAMPLE_ROWS - 2:, :])
    y_sample = xs.reshape(n_seq, SAMPLE_ROWS, d)[:, TOK0:, :]
    kv_shape = (depth, n_seq, N_KV_HEADS, HEAD_DIM, WINDOW)
    k_sample = jnp.transpose(kb_all.reshape(kv_shape), (0, 1, 4, 2, 3))
    v_sample = jnp.transpose(vb_all.reshape(kv_shape), (0, 1, 4, 2, 3))
    return (xp, y_sample, jnp.stack(kp), jnp.stack(vp), jnp.stack(cp),
            k_sample, v_sample, jnp.stack(cs))
```

```python
import functools

import numpy as np
import jax
import jax.numpy as jnp
from jax import lax
from jax.experimental import pallas as pl
from jax.experimental.pallas import tpu as pltpu

F32 = jnp.float32
BF16 = jnp.bfloat16

D_MODEL = 2048
N_META = 16
ATT_DIM = 1024
CONV_DIM = 1024
HEAD_DIM = 64
N_HEADS = 16
N_KV_HEADS = 4
GQA_GROUP = N_HEADS // N_KV_HEADS
KV_DIM = N_KV_HEADS * HEAD_DIM
WINDOW = 128
PROJ_DIM = 2 * ATT_DIM + 2 * KV_DIM + 4 * CONV_DIM
LN_EPS = 1e-5
NEG_INF = -1e30
Q_SCALE = HEAD_DIM ** -0.5

C_Q = 0
C_K = ATT_DIM
C_V = C_K + KV_DIM
C_GA = C_V + KV_DIM
C_B = C_GA + ATT_DIM
C_C = C_B + CONV_DIM
C_H = C_C + CONV_DIM
C_GC = C_H + CONV_DIM

LANES = 128
HEAD_PAD = WINDOW - N_META
PROMPT_TILE = 256
FILL_CHUNK = 256
SAMPLE_ROWS = 8
VMEM_LIMIT = 58 * 1024 * 1024

SLOPES = [float(np.float32(2.0 ** (-8.0 * (h + 1) / N_HEADS))) for h in range(N_HEADS)]

_NT = (((1,), (1,)), ((), ()))


def _silu(g):
    return g * (1.0 / (1.0 + jnp.exp(-g)))


def _softmax_rows(s, sink):
    m = jnp.maximum(jnp.max(s, axis=1, keepdims=True), sink)
    p = jnp.exp(s - m)
    denom = jnp.sum(p, axis=1, keepdims=True) + jnp.exp(sink - m)
    return p * (1.0 / denom)


def _softmax_cols(s, sink):
    m = jnp.maximum(jnp.max(s, axis=0, keepdims=True), sink)
    p = jnp.exp(s - m)
    denom = jnp.sum(p, axis=0, keepdims=True) + jnp.exp(sink - m)
    return p * (1.0 / denom)


def _layer_norm(z, g, b):
    mu = jnp.mean(z, axis=1, keepdims=True)
    zc = z - mu
    var = jnp.mean(zc * zc, axis=1, keepdims=True)
    return zc * lax.rsqrt(var + LN_EPS) * g + b


def _proj(xb, w_ref, c0, width):
    return jnp.dot(xb, w_ref[:, c0:c0 + width], preferred_element_type=F32)


def _store_k_variants(src, dst, r0, rows):
    low = lax.broadcasted_iota(jnp.int32, (rows, LANES), 1) < HEAD_DIM
    for cc in range(KV_DIM // LANES):
        col = src[:, cc * LANES:(cc + 1) * LANES]
        swapped = pltpu.roll(col, HEAD_DIM, axis=1)
        h_even, h_odd = 2 * cc, 2 * cc + 1
        dst[2 * h_even + 0, r0:r0 + rows, :] = jnp.where(low, col, 0.0).astype(BF16)
        dst[2 * h_even + 1, r0:r0 + rows, :] = jnp.where(low, 0.0, swapped).astype(BF16)
        dst[2 * h_odd + 0, r0:r0 + rows, :] = jnp.where(low, swapped, 0.0).astype(BF16)
        dst[2 * h_odd + 1, r0:r0 + rows, :] = jnp.where(low, 0.0, col).astype(BF16)


def _conv_chunk(xb, win_ref, cw_ref, ucar, mix_ref, rows, lo, width):
    cols = slice(lo, lo + width)
    bg = _proj(xb, win_ref, C_B + lo, width)
    u = _proj(xb, win_ref, C_C + lo, width) * _proj(xb, win_ref, C_H + lo, width)
    row = lax.broadcasted_iota(jnp.int32, (rows, width), 0)
    prev1 = ucar[7:8, cols]
    prev2 = ucar[6:7, cols]
    u1 = jnp.where(row == 0, prev1, pltpu.roll(u, 1, axis=0))
    u2 = jnp.where(row == 0, prev2, jnp.where(row == 1, prev1, pltpu.roll(u, 2, axis=0)))
    cy = cw_ref[0:1, cols] * u2 + cw_ref[1:2, cols] * u1 + cw_ref[2:3, cols] * u
    ucar[:, cols] = u[rows - 8:rows, :]
    gate_c = _silu(_proj(xb, win_ref, C_GC + lo, width))
    mix_ref[0:rows, ATT_DIM + lo:ATT_DIM + lo + width] = ((bg * cy) * gate_c).astype(BF16)


def _store_v_variants(src, dst, c0, rows):
    vt = src.T
    zeros = jnp.zeros((HEAD_DIM, rows), F32)
    for h in range(N_KV_HEADS):
        vh = vt[h * HEAD_DIM:(h + 1) * HEAD_DIM, :]
        dst[2 * h + 0, :, c0:c0 + rows] = jnp.concatenate([vh, zeros], axis=0).astype(BF16)
        dst[2 * h + 1, :, c0:c0 + rows] = jnp.concatenate([zeros, vh], axis=0).astype(BF16)


def _scores_pair(layer, qb, kmask, sinks_ref, kvar, bias_ref, p_scr, r0, c):
    h = c // 2
    q2 = qb[r0:r0 + WINDOW, c * LANES:(c + 1) * LANES]
    for par in range(2):
        hd = 2 * c + par
        keys = kvar[2 * h + par, r0:r0 + 2 * WINDOW, :]
        s = lax.dot_general(keys, q2, _NT, preferred_element_type=F32)
        s = s + SLOPES[hd] * bias_ref[...]
        if kmask is not None:
            s = s + kmask
        p_scr[hd] = _softmax_cols(s, sinks_ref[layer, hd]).astype(BF16)


def _values_pair(gate_ref, vtvar, p_scr, mix_ref, r0, c):
    h = c // 2
    lanes = slice(c * LANES, (c + 1) * LANES)
    o_t = None
    for par in range(2):
        vals_t = vtvar[2 * h + par, :, r0:r0 + 2 * WINDOW]
        o = jnp.dot(vals_t, p_scr[2 * c + par], preferred_element_type=F32)
        o_t = o if o_t is None else o_t + o
    mix_ref[r0:r0 + WINDOW, lanes] = (o_t.T * gate_ref[r0:r0 + WINDOW, lanes]).astype(BF16)


def _gate_chunk(xb, win_ref, gate_ref, rows, lo, width):
    gate_ref[0:rows, lo:lo + width] = _silu(_proj(xb, win_ref, C_GA + lo, width))


def _prompt_rows(x, rows, kmask0, after_qkv, layer, sinks_ref, win_ref, wout_ref, cw_ref,
                 kvar, vtvar, ucar, bias_ref, mix_ref, p_scr, gate_ref, alpha):
    xb = x.astype(BF16)
    hq = _proj(xb, win_ref, 0, C_GA)
    after_qkv()
    qb = (hq[:, C_Q:C_Q + ATT_DIM] * Q_SCALE).astype(BF16)
    kf = hq[:, C_K:C_K + KV_DIM]
    vf = hq[:, C_V:C_V + KV_DIM]
    _store_k_variants(kf, kvar, WINDOW, rows)
    _store_v_variants(vf, vtvar, WINDOW, rows)

    n_slabs = ATT_DIM // LANES
    n_blocks = rows // WINDOW
    fillers = [functools.partial(_gate_chunk, xb, win_ref, gate_ref, rows, lo, FILL_CHUNK)
               for lo in range(0, ATT_DIM, FILL_CHUNK)]
    fillers += [functools.partial(_conv_chunk, xb, win_ref, cw_ref, ucar, mix_ref, rows, lo,
                                  FILL_CHUNK) for lo in range(0, CONV_DIM, FILL_CHUNK)]
    units_per_filler = (n_blocks * n_slabs) // len(fillers)
    assert ATT_DIM // FILL_CHUNK <= n_slabs // units_per_filler
    for blk in range(n_blocks):
        r0 = blk * WINDOW
        for c in range(n_slabs):
            unit = blk * n_slabs + c
            if unit % units_per_filler == 0:
                fillers[unit // units_per_filler]()
            _scores_pair(layer, qb, kmask0 if blk == 0 else None, sinks_ref, kvar, bias_ref,
                         p_scr, r0, c)
        for c in range(n_slabs):
            _values_pair(gate_ref, vtvar, p_scr, mix_ref, r0, c)

    out = jnp.dot(mix_ref[0:rows, :], wout_ref[...], preferred_element_type=F32)
    z = alpha * x + out

    for i in range(2 * N_KV_HEADS):
        kvar[i, 0:WINDOW, :] = kvar[i, rows:rows + WINDOW, :]
        vtvar[i, :, 0:WINDOW] = vtvar[i, :, rows:rows + WINDOW]
    return z, kf, vf


def _prompt_kernel(sinks_ref, x_ref, xh_ref, win_ref, wout_ref, cw_ref, g_ref, b_ref,
                   y_ref, yh_ref, kl_ref, vl_ref, cs_ref,
                   kvar, vtvar, khead, vhead, ucar, uhead, bias_ref, mix_ref, p_scr, gate_ref,
                   z_scr, *, layer, tm, n_tiles, alpha):
    t = pl.program_id(0)
    last = pl.num_programs(0) - 1
    j = t % n_tiles
    shared = (layer, sinks_ref, win_ref, wout_ref, cw_ref, kvar, vtvar, ucar, bias_ref, mix_ref,
              p_scr, gate_ref, alpha)
    key_row = lax.broadcasted_iota(jnp.int32, (2 * WINDOW, WINDOW), 0)

    @pl.when(t == 0)
    def _():
        qi = lax.broadcasted_iota(jnp.int32, (2 * WINDOW, WINDOW), 1)
        dist = WINDOW + qi - key_row
        visible = (dist >= 0) & (dist < WINDOW)
        bias_ref[...] = jnp.where(visible, -dist.astype(F32), NEG_INF)
        z_scr[...] = jnp.zeros(z_scr.shape, F32)
        kvar[:, 0:WINDOW, :] = jnp.zeros((2 * N_KV_HEADS, WINDOW, LANES), BF16)
        vtvar[:, :, 0:WINDOW] = jnp.zeros((2 * N_KV_HEADS, LANES, WINDOW), BF16)
        ucar[...] = jnp.zeros(ucar.shape, F32)
        kmask = jnp.where(key_row < WINDOW + HEAD_PAD, NEG_INF, 0.0)
        xh = jnp.concatenate([jnp.zeros((HEAD_PAD, D_MODEL), F32), xh_ref[...]], axis=0)
        zh, _, _ = _prompt_rows(xh, WINDOW, kmask, lambda: None, *shared)
        yh_ref[...] = _layer_norm(zh, g_ref[...], b_ref[...])[HEAD_PAD:, :]
        khead[...] = kvar[:, 0:WINDOW, :]
        vhead[...] = vtvar[:, :, 0:WINDOW]
        uhead[...] = ucar[...]

    def norm_previous_tile():
        y_ref[...] = _layer_norm(z_scr[...], g_ref[...], b_ref[...])

    @pl.when(t < last)
    def _():
        @pl.when(j == 0)
        def _():
            kvar[:, 0:WINDOW, :] = khead[...]
            vtvar[:, :, 0:WINDOW] = vhead[...]
            ucar[...] = uhead[...]

        kmask = jnp.where(key_row < HEAD_PAD, jnp.where(j == 0, NEG_INF, 0.0), 0.0)
        z, kf, vf = _prompt_rows(x_ref[...], tm, kmask, norm_previous_tile, *shared)
        z_scr[...] = z

        @pl.when(j == n_tiles - 1)
        def _():
            kl_ref[...] = kf[tm - WINDOW:tm, :]
            vl_ref[...] = vf[tm - WINDOW:tm, :]
            cs_ref[...] = ucar[...]

    @pl.when(t == last)
    def _():
        norm_previous_tile()


def _resident(shape, index_map):
    return pl.BlockSpec(shape, index_map, pipeline_mode=pl.Buffered(1))


def _prompt_layer(layer, x, xh, w_in, w_out, conv_w, sinks, ln_g, ln_b, alpha):
    batch, seq, d = x.shape
    tm = PROMPT_TILE
    n_tiles = seq // tm
    total = batch * n_tiles
    const2 = lambda t: (0, 0)
    this_layer = lambda t: (layer, 0, 0)

    def tile_block(t):
        t = jnp.minimum(t, total - 1)
        return (t // n_tiles, t % n_tiles, 0)

    def prev_tile_block(t):
        return tile_block(jnp.maximum(t - 1, 0))

    per_batch = lambda t: (jnp.minimum(t, total - 1) // n_tiles, 0, 0)
    kernel = functools.partial(_prompt_kernel, layer=layer, tm=tm, n_tiles=n_tiles, alpha=alpha)
    return pl.pallas_call(
        kernel,
        grid=(total + 1,),
        in_specs=[
            pl.BlockSpec(memory_space=pltpu.SMEM),
            pl.BlockSpec((None, tm, d), tile_block),
            _resident((N_META, d), const2),
            _resident((None, d, PROJ_DIM), this_layer),
            _resident((None, ATT_DIM + CONV_DIM, d), this_layer),
            pl.BlockSpec((None, 3, CONV_DIM), this_layer),
            pl.BlockSpec((None, 1, d), this_layer),
            pl.BlockSpec((None, 1, d), this_layer),
        ],
        out_specs=[
            pl.BlockSpec((None, tm, d), prev_tile_block),
            pl.BlockSpec((N_META, d), const2),
            pl.BlockSpec((None, WINDOW, KV_DIM), per_batch),
            pl.BlockSpec((None, WINDOW, KV_DIM), per_batch),
            pl.BlockSpec((None, 8, CONV_DIM), per_batch),
        ],
        out_shape=[
            jax.ShapeDtypeStruct((batch, seq, d), F32),
            jax.ShapeDtypeStruct((N_META, d), F32),
            jax.ShapeDtypeStruct((batch, WINDOW, KV_DIM), F32),
            jax.ShapeDtypeStruct((batch, WINDOW, KV_DIM), F32),
            jax.ShapeDtypeStruct((batch, 8, CONV_DIM), F32),
        ],
        scratch_shapes=[
            pltpu.VMEM((2 * N_KV_HEADS, WINDOW + tm, LANES), BF16),
            pltpu.VMEM((2 * N_KV_HEADS, LANES, WINDOW + tm), BF16),
            pltpu.VMEM((2 * N_KV_HEADS, WINDOW, LANES), BF16),
            pltpu.VMEM((2 * N_KV_HEADS, LANES, WINDOW), BF16),
            pltpu.VMEM((8, CONV_DIM), F32),
            pltpu.VMEM((8, CONV_DIM), F32),
            pltpu.VMEM((2 * WINDOW, WINDOW), F32),
            pltpu.VMEM((tm, ATT_DIM + CONV_DIM), BF16),
            pltpu.VMEM((N_HEADS, 2 * WINDOW, WINDOW), BF16),
            pltpu.VMEM((tm, ATT_DIM), F32),
            pltpu.VMEM((tm, d), F32),
        ],
        compiler_params=pltpu.CompilerParams(
            dimension_semantics=("arbitrary",),
            vmem_limit_bytes=VMEM_LIMIT,
        ),
        name="prompt_layer",
    )(sinks, x, xh, w_in, w_out, conv_w, ln_g, ln_b)


SEQ_PER_GROUP = 2
GROUP_ROWS = SEQ_PER_GROUP * SAMPLE_ROWS
GROUPS_PER_STEP = 2
SEQ_PER_STEP = SEQ_PER_GROUP * GROUPS_PER_STEP
N_NEW = 4
TOK0 = SAMPLE_ROWS - N_NEW
LOG_SAMPLE_ROWS = 3
LOG_GROUP_ROWS = 4
LOG_HEAD_DIM = 6
assert (1 << LOG_SAMPLE_ROWS, 1 << LOG_GROUP_ROWS, 1 << LOG_HEAD_DIM) == (
    SAMPLE_ROWS, GROUP_ROWS, HEAD_DIM)


def _sample_kernel(sinks_ref, x_ref, win_ref, wout_ref, cw_ref, g_ref, b_ref, ck_ref, cv_ref,
                   st_ref, kb_prev, vb_prev, y_ref, kb_ref, vb_ref, u_ref,
                   qe, knew, vnew, gate_a, bias_ref, sinkcol, mix_ref, *, layer, n_rows, alpha):
    del kb_prev, vb_prev
    grp = pl.program_id(0)
    n_steps = pl.num_programs(0)
    n_q = N_HEADS * GROUP_ROWS
    lane_blk = lax.broadcasted_iota(jnp.int32, (n_rows, KV_DIM), 1) >> LOG_HEAD_DIM

    @pl.when(grp == 0)
    def _():
        x = x_ref[...]
        xb = x.astype(BF16)
        hq = _proj(xb, win_ref, 0, C_GA)
        q = hq[:, C_Q:C_Q + ATT_DIM] * Q_SCALE
        knew[...] = hq[:, C_K:C_K + KV_DIM]
        vnew[...] = hq[:, C_V:C_V + KV_DIM]
        for hd in range(N_HEADS):
            h, g = divmod(hd, GQA_GROUP)
            slab = q[:, h * KV_DIM:(h + 1) * KV_DIM]
            moved = pltpu.roll(slab, ((h - g) % GQA_GROUP) * HEAD_DIM, axis=1)
            qe[hd] = jnp.where(lane_blk == h, moved, 0.0).astype(BF16)
        gate_a[...] = _silu(_proj(xb, win_ref, C_GA, ATT_DIM))

        bg = _proj(xb, win_ref, C_B, CONV_DIM)
        u = _proj(xb, win_ref, C_C, CONV_DIM) * _proj(xb, win_ref, C_H, CONV_DIM)
        r8 = lax.broadcasted_iota(jnp.int32, (n_rows, CONV_DIM), 0) & (SAMPLE_ROWS - 1)
        is_state = (r8 >= TOK0 - 2) & (r8 < TOK0)
        u = jnp.where(is_state, st_ref[...], u)
        u_ref[...] = u
        cy = (cw_ref[0:1, :] * pltpu.roll(u, 2, axis=0) + cw_ref[1:2, :] * pltpu.roll(u, 1, axis=0)
              + cw_ref[2:3, :] * u)
        gate_c = _silu(_proj(xb, win_ref, C_GC, CONV_DIM))
        mix_ref[:, ATT_DIM:] = ((bg * cy) * gate_c).astype(BF16)

        qrow = lax.broadcasted_iota(jnp.int32, (n_q, 2 * WINDOW), 0)
        key = lax.broadcasted_iota(jnp.int32, (n_q, 2 * WINDOW), 1)
        q_tok = jnp.maximum((qrow & (SAMPLE_ROWS - 1)) - TOK0, 0)
        q_seq = (qrow >> LOG_SAMPLE_ROWS) & (SEQ_PER_GROUP - 1)
        new = key - WINDOW
        k_tok = (new & (SAMPLE_ROWS - 1)) - TOK0
        k_seq = new >> LOG_SAMPLE_ROWS
        cached = key < WINDOW
        dist = jnp.where(cached, WINDOW + q_tok - key, q_tok - k_tok)
        ok_new = (new >= 0) & (new < GROUP_ROWS) & (k_seq == q_seq) & (k_tok >= 0)
        visible = (dist >= 0) & (dist < WINDOW) & (cached | ok_new)
        slope = jnp.zeros((n_q, 2 * WINDOW), F32)
        sink = jnp.zeros((n_q, LANES), F32)
        srow = lax.broadcasted_iota(jnp.int32, (n_q, LANES), 0)
        for hd in range(N_HEADS):
            slope = jnp.where((qrow >> LOG_GROUP_ROWS) == hd, SLOPES[hd], slope)
            sink = jnp.where((srow >> LOG_GROUP_ROWS) == hd, sinks_ref[layer, hd], sink)
        bias_ref[...] = jnp.where(visible, -(slope * dist.astype(F32)), NEG_INF)
        sinkcol[...] = sink

    q_seq = ((lax.broadcasted_iota(jnp.int32, (n_q, KV_DIM), 0) >> LOG_SAMPLE_ROWS)
             & (SEQ_PER_GROUP - 1))
    newest = lax.broadcasted_iota(jnp.int32, (KV_DIM, WINDOW), 1) >= WINDOW - N_NEW
    blk16 = lax.broadcasted_iota(jnp.int32, (GROUP_ROWS, KV_DIM), 1) >> LOG_HEAD_DIM
    pad_rows = jnp.zeros((WINDOW - GROUP_ROWS, KV_DIM), F32)
    sink = sinkcol[:, 0:1]

    probs, v_new_bs, row0s = [], [], []
    for gi in range(GROUPS_PER_STEP):
        g0 = pl.multiple_of((grp * GROUPS_PER_STEP + gi) * GROUP_ROWS, GROUP_ROWS)
        row0s.append(g0)
        w_g = jnp.concatenate([qe[hd, pl.ds(g0, GROUP_ROWS), :] for hd in range(N_HEADS)], axis=0)
        k_new = jnp.concatenate([knew[pl.ds(g0, GROUP_ROWS), :], pad_rows], axis=0)
        v_new = jnp.concatenate([vnew[pl.ds(g0, GROUP_ROWS), :], pad_rows], axis=0)
        k_new_b = k_new.astype(BF16)
        v_new_bs.append(v_new.astype(BF16))
        k_new_t = k_new.T
        v_new_t = v_new.T
        for s in range(SEQ_PER_GROUP):
            n = gi * SEQ_PER_GROUP + s
            ck_t = ck_ref[n]
            sc = jnp.concatenate(
                [jnp.dot(w_g, ck_t.astype(BF16), preferred_element_type=F32),
                 lax.dot_general(w_g, k_new_b, _NT, preferred_element_type=F32)], axis=1)
            probs.append(_softmax_rows(sc + bias_ref[...], sink).astype(BF16))
            to_tail = WINDOW - N_NEW - (s * SAMPLE_ROWS + TOK0)
            for cache_t, new_t, out_ref in ((ck_t, k_new_t, kb_ref), (cv_ref[n], v_new_t, vb_ref)):
                out_ref[n] = jnp.where(newest, pltpu.roll(new_t, to_tail, axis=1),
                                       pltpu.roll(cache_t, WINDOW - N_NEW, axis=1))

    for gi in range(GROUPS_PER_STEP):
        o_grp = jnp.zeros((n_q, KV_DIM), F32)
        for s in range(SEQ_PER_GROUP):
            n = gi * SEQ_PER_GROUP + s
            p = probs[n]
            o = (lax.dot_general(p[:, :WINDOW], cv_ref[n].astype(BF16), _NT,
                                 preferred_element_type=F32)
                 + jnp.dot(p[:, WINDOW:], v_new_bs[gi], preferred_element_type=F32))
            o_grp = jnp.where(q_seq == s, o, o_grp)
        for h in range(N_KV_HEADS):
            slab = jnp.zeros((GROUP_ROWS, KV_DIM), F32)
            for g in range(GQA_GROUP):
                hd = h * GQA_GROUP + g
                piece = jnp.where(blk16 == h, o_grp[hd * GROUP_ROWS:(hd + 1) * GROUP_ROWS, :], 0.0)
                slab = slab + pltpu.roll(piece, ((g - h) % GQA_GROUP) * HEAD_DIM, axis=1)
            rows = pl.ds(row0s[gi], GROUP_ROWS)
            cols = slice(h * KV_DIM, (h + 1) * KV_DIM)
            mix_ref[rows, cols] = (slab * gate_a[rows, cols]).astype(BF16)

    @pl.when(grp == n_steps - 1)
    def _():
        out = jnp.dot(mix_ref[...], wout_ref[...], preferred_element_type=F32)
        y_ref[...] = _layer_norm(alpha * x_ref[...] + out, g_ref[...], b_ref[...])


def _sample_layer(layer, x8, st_all, ck_all, cv_all, kb_all, vb_all, w_in, w_out, conv_w, sinks,
                  ln_g, ln_b, alpha):
    n_rows, d = x8.shape
    depth, n_seq = ck_all.shape[:2]
    this_layer = lambda g: (layer, 0, 0)
    n_steps = n_seq // SEQ_PER_STEP
    n_q = N_HEADS * GROUP_ROWS
    const2 = lambda g: (0, 0)
    cache_spec = pl.BlockSpec((None, SEQ_PER_STEP, KV_DIM, WINDOW), lambda g: (layer, g, 0, 0))
    kernel = functools.partial(_sample_kernel, layer=layer, n_rows=n_rows, alpha=alpha)
    operands = [sinks, x8, w_in, w_out, conv_w, ln_g, ln_b, ck_all, cv_all, st_all, kb_all, vb_all]
    in_specs = [
        pl.BlockSpec(memory_space=pltpu.SMEM),
        _resident((n_rows, d), const2),
        _resident((None, d, PROJ_DIM), this_layer),
        _resident((None, ATT_DIM + CONV_DIM, d), this_layer),
        pl.BlockSpec((None, 3, CONV_DIM), this_layer),
        pl.BlockSpec((None, 1, d), this_layer),
        pl.BlockSpec((None, 1, d), this_layer),
        cache_spec,
        cache_spec,
        _resident((None, n_rows, CONV_DIM), this_layer),
        pl.BlockSpec(memory_space=pl.ANY),
        pl.BlockSpec(memory_space=pl.ANY),
    ]
    aliases = {len(operands) - 2: 1, len(operands) - 1: 2} if layer > 0 else {}
    return pl.pallas_call(
        kernel,
        grid=(n_steps,),
        in_specs=in_specs,
        out_specs=[
            pl.BlockSpec((n_rows, d), const2),
            cache_spec,
            cache_spec,
            pl.BlockSpec((n_rows, CONV_DIM), const2),
        ],
        out_shape=[
            jax.ShapeDtypeStruct((n_rows, d), F32),
            jax.ShapeDtypeStruct(ck_all.shape, F32),
            jax.ShapeDtypeStruct(cv_all.shape, F32),
            jax.ShapeDtypeStruct((n_rows, CONV_DIM), F32),
        ],
        input_output_aliases=aliases,
        scratch_shapes=[
            pltpu.VMEM((N_HEADS, n_rows, KV_DIM), BF16),
            pltpu.VMEM((n_rows, KV_DIM), F32),
            pltpu.VMEM((n_rows, KV_DIM), F32),
            pltpu.VMEM((n_rows, ATT_DIM), F32),
            pltpu.VMEM((n_q, 2 * WINDOW), F32),
            pltpu.VMEM((n_q, LANES), F32),
            pltpu.VMEM((n_rows, ATT_DIM + CONV_DIM), BF16),
        ],
        compiler_params=pltpu.CompilerParams(
            dimension_semantics=("arbitrary",),
            vmem_limit_bytes=VMEM_LIMIT,
        ),
        name="sample_layer",
    )(*operands)


def kernel(x_prompt, x_sample, cache_k, cache_v, state_conv, meta_tokens,
           w_in, conv_w, sinks, w_out, ln_g, ln_b):
    depth = w_in.shape[0]
    alpha = float((2 * depth) ** 0.25)
    batch, seq, d = x_prompt.shape
    n_seq, n_tok = x_sample.shape[:2]
    assert d == D_MODEL and seq % PROMPT_TILE == 0 and n_tok == SAMPLE_ROWS - TOK0
    assert meta_tokens.shape[0] == N_META and n_seq % SEQ_PER_STEP == 0
    assert cache_k.shape[2] == WINDOW and state_conv.shape[2] == 2

    w_in = w_in.astype(BF16)
    w_out = w_out.astype(BF16)
    ln_g3 = ln_g.reshape(depth, 1, d)
    ln_b3 = ln_b.reshape(depth, 1, d)
    xp = x_prompt
    xh = meta_tokens.astype(F32)
    xs = jnp.pad(x_sample, ((0, 0), (TOK0, 0), (0, 0))).reshape(n_seq * SAMPLE_ROWS, d)
    ck_all = jnp.transpose(cache_k, (0, 1, 3, 4, 2)).reshape(depth, n_seq, KV_DIM, WINDOW)
    cv_all = jnp.transpose(cache_v, (0, 1, 3, 4, 2)).reshape(depth, n_seq, KV_DIM, WINDOW)
    st_all = jnp.pad(state_conv, ((0, 0), (0, 0), (TOK0 - 2, SAMPLE_ROWS - TOK0), (0, 0)))
    st_all = st_all.reshape(depth, n_seq * SAMPLE_ROWS, CONV_DIM)

    kp, vp, cp, cs = [], [], [], []
    kb_all, vb_all = ck_all, cv_all
    for l in range(depth):
        xp, xh, k_last, v_last, c_last = _prompt_layer(
            l, xp, xh, w_in, w_out, conv_w, sinks, ln_g3, ln_b3, alpha)
        kp.append(k_last.reshape(batch, WINDOW, N_KV_HEADS, HEAD_DIM))
        vp.append(v_last.reshape(batch, WINDOW, N_KV_HEADS, HEAD_DIM))
        cp.append(c_last[:, 6:8, :])
        xs, kb_all, vb_all, u8 = _sample_layer(
            l, xs, st_all, ck_all, cv_all, kb_all, vb_all, w_in, w_out, conv_w, sinks, ln_g3,
            ln_b3, alpha)
        cs.append(u8.reshape(n_seq, SAMPLE_ROWS, CONV_DIM)[:, SAMPLE_ROWS - 2:, :])
    y_sample = xs.reshape(n_seq, SAMPLE_ROWS, d)[:, TOK0:, :]
    kv_shape = (depth, n_seq, N_KV_HEADS, HEAD_DIM, WINDOW)
    k_sample = jnp.transpose(kb_all.reshape(kv_shape), (0, 1, 4, 2, 3))
    v_sample = jnp.transpose(vb_all.reshape(kv_shape), (0, 1, 4, 2, 3))
    return (xp, y_sample, jnp.stack(kp), jnp.stack(vp), jnp.stack(cp),
            k_sample, v_sample, jnp.stack(cs))
```

```python
import functools

import numpy as np
import jax
import jax.numpy as jnp
from jax import lax
from jax.experimental import pallas as pl
from jax.experimental.pallas import tpu as pltpu

F32 = jnp.float32
BF16 = jnp.bfloat16

D_MODEL = 2048
N_META = 16
ATT_DIM = 1024
CONV_DIM = 1024
HEAD_DIM = 64
N_HEADS = 16
N_KV_HEADS = 4
GQA_GROUP = N_HEADS // N_KV_HEADS
KV_DIM = N_KV_HEADS * HEAD_DIM
WINDOW = 128
PROJ_DIM = 2 * ATT_DIM + 2 * KV_DIM + 4 * CONV_DIM
LN_EPS = 1e-5
NEG_INF = -1e30
Q_SCALE = HEAD_DIM ** -0.5

C_Q = 0
C_K = ATT_DIM
C_V = C_K + KV_DIM
C_GA = C_V + KV_DIM
C_B = C_GA + ATT_DIM
C_C = C_B + CONV_DIM
C_H = C_C + CONV_DIM
C_GC = C_H + CONV_DIM

LANES = 128
HEAD_PAD = WINDOW - N_META
PROMPT_TILE = 256
FILL_CHUNK = 256
SAMPLE_ROWS = 8
VMEM_LIMIT = 58 * 1024 * 1024

SLOPES = [float(np.float32(2.0 ** (-8.0 * (h + 1) / N_HEADS))) for h in range(N_HEADS)]

_NT = (((1,), (1,)), ((), ()))


def _silu(g):
    return g * (1.0 / (1.0 + jnp.exp(-g)))


def _softmax_rows(s, sink):
    m = jnp.maximum(jnp.max(s, axis=1, keepdims=True), sink)
    p = jnp.exp(s - m)
    denom = jnp.sum(p, axis=1, keepdims=True) + jnp.exp(sink - m)
    return p * (1.0 / denom)


def _softmax_cols(s, sink):
    m = jnp.maximum(jnp.max(s, axis=0, keepdims=True), sink)
    p = jnp.exp(s - m)
    denom = jnp.sum(p, axis=0, keepdims=True) + jnp.exp(sink - m)
    return p * (1.0 / denom)


def _layer_norm(z, g, b):
    mu = jnp.mean(z, axis=1, keepdims=True)
    zc = z - mu
    var = jnp.mean(zc * zc, axis=1, keepdims=True)
    return zc * lax.rsqrt(var + LN_EPS) * g + b


def _proj(xb, w_ref, c0, width):
    return jnp.dot(xb, w_ref[:, c0:c0 + width], preferred_element_type=F32)


def _store_k_variants(src, dst, r0, rows):
    low = lax.broadcasted_iota(jnp.int32, (rows, LANES), 1) < HEAD_DIM
    for cc in range(KV_DIM // LANES):
        col = src[:, cc * LANES:(cc + 1) * LANES]
        swapped = pltpu.roll(col, HEAD_DIM, axis=1)
        h_even, h_odd = 2 * cc, 2 * cc + 1
        dst[2 * h_even + 0, r0:r0 + rows, :] = jnp.where(low, col, 0.0).astype(BF16)
        dst[2 * h_even + 1, r0:r0 + rows, :] = jnp.where(low, 0.0, swapped).astype(BF16)
        dst[2 * h_odd + 0, r0:r0 + rows, :] = jnp.where(low, swapped, 0.0).astype(BF16)
        dst[2 * h_odd + 1, r0:r0 + rows, :] = jnp.where(low, 0.0, col).astype(BF16)


def _conv_chunk(xb, win_ref, cw_ref, ucar, mix_ref, rows, lo, width):
    cols = slice(lo, lo + width)
    bg = _proj(xb, win_ref, C_B + lo, width)
    u = _proj(xb, win_ref, C_C + lo, width) * _proj(xb, win_ref, C_H + lo, width)
    row = lax.broadcasted_iota(jnp.int32, (rows, width), 0)
    prev1 = ucar[7:8, cols]
    prev2 = ucar[6:7, cols]
    u1 = jnp.where(row == 0, prev1, pltpu.roll(u, 1, axis=0))
    u2 = jnp.where(row == 0, prev2, jnp.where(row == 1, prev1, pltpu.roll(u, 2, axis=0)))
    cy = cw_ref[0:1, cols] * u2 + cw_ref[1:2, cols] * u1 + cw_ref[2:3, cols] * u
    ucar[:, cols] = u[rows - 8:rows, :]
    gate_c = _silu(_proj(xb, win_ref, C_GC + lo, width))
    mix_ref[0:rows, ATT_DIM + lo:ATT_DIM + lo + width] = ((bg * cy) * gate_c).astype(BF16)


def _store_v_variants(src, dst, c0, rows):
    vt = src.T
    zeros = jnp.zeros((HEAD_DIM, rows), F32)
    for h in range(N_KV_HEADS):
        vh = vt[h * HEAD_DIM:(h + 1) * HEAD_DIM, :]
        dst[2 * h + 0, :, c0:c0 + rows] = jnp.concatenate([vh, zeros], axis=0).astype(BF16)
        dst[2 * h + 1, :, c0:c0 + rows] = jnp.concatenate([zeros, vh], axis=0).astype(BF16)


def _scores_pair(layer, qb, kmask, sinks_ref, kvar, bias_ref, p_scr, r0, c):
    h = c // 2
    q2 = qb[r0:r0 + WINDOW, c * LANES:(c + 1) * LANES]
    for par in range(2):
        hd = 2 * c + par
        keys = kvar[2 * h + par, r0:r0 + 2 * WINDOW, :]
        s = lax.dot_general(keys, q2, _NT, preferred_element_type=F32)
        s = s + SLOPES[hd] * bias_ref[...]
        if kmask is not None:
            s = s + kmask
        p_scr[hd] = _softmax_cols(s, sinks_ref[layer, hd]).astype(BF16)


def _values_pair(gate_ref, vtvar, p_scr, mix_ref, r0, c):
    h = c // 2
    lanes = slice(c * LANES, (c + 1) * LANES)
    o_t = None
    for par in range(2):
        vals_t = vtvar[2 * h + par, :, r0:r0 + 2 * WINDOW]
        o = jnp.dot(vals_t, p_scr[2 * c + par], preferred_element_type=F32)
        o_t = o if o_t is None else o_t + o
    mix_ref[r0:r0 + WINDOW, lanes] = (o_t.T * gate_ref[r0:r0 + WINDOW, lanes]).astype(BF16)


def _gate_chunk(xb, win_ref, gate_ref, rows, lo, width):
    gate_ref[0:rows, lo:lo + width] = _silu(_proj(xb, win_ref, C_GA + lo, width))


def _prompt_rows(x, rows, kmask0, after_qkv, layer, sinks_ref, win_ref, wout_ref, cw_ref,
                 kvar, vtvar, ucar, bias_ref, mix_ref, p_scr, gate_ref, alpha):
    xb = x.astype(BF16)
    hq = _proj(xb, win_ref, 0, C_GA)
    after_qkv()
    qb = (hq[:, C_Q:C_Q + ATT_DIM] * Q_SCALE).astype(BF16)
    kf = hq[:, C_K:C_K + KV_DIM]
    vf = hq[:, C_V:C_V + KV_DIM]
    _store_k_variants(kf, kvar, WINDOW, rows)
    _store_v_variants(vf, vtvar, WINDOW, rows)

    n_slabs = ATT_DIM // LANES
    n_blocks = rows // WINDOW
    fillers = [functools.partial(_gate_chunk, xb, win_ref, gate_ref, rows, lo, FILL_CHUNK)
               for lo in range(0, ATT_DIM, FILL_CHUNK)]
    fillers += [functools.partial(_conv_chunk, xb, win_ref, cw_ref, ucar, mix_ref, rows, lo,
                                  FILL_CHUNK) for lo in range(0, CONV_DIM, FILL_CHUNK)]
    units_per_filler = (n_blocks * n_slabs) // len(fillers)
    assert ATT_DIM // FILL_CHUNK <= n_slabs // units_per_filler
    for blk in range(n_blocks):
        r0 = blk * WINDOW
        for c in range(n_slabs):
            unit = blk * n_slabs + c
            if unit % units_per_filler == 0:
                fillers[unit // units_per_filler]()
            _scores_pair(layer, qb, kmask0 if blk == 0 else None, sinks_ref, kvar, bias_ref,
                         p_scr, r0, c)
        for c in range(n_slabs):
            _values_pair(gate_ref, vtvar, p_scr, mix_ref, r0, c)

    out = jnp.dot(mix_ref[0:rows, :], wout_ref[...], preferred_element_type=F32)
    z = alpha * x + out

    for i in range(2 * N_KV_HEADS):
        kvar[i, 0:WINDOW, :] = kvar[i, rows:rows + WINDOW, :]
        vtvar[i, :, 0:WINDOW] = vtvar[i, :, rows:rows + WINDOW]
    return z, kf, vf


def _cast_next_weights(t, last, next_layer, win_f32, wout_f32, win_next, wout_next,
                       stage_in, stage_out, sem):
    rows = stage_in[0].shape[0]
    srcs = (win_f32, wout_f32)
    dsts = (win_next, wout_next)

    def read(k, i):
        return pltpu.make_async_copy(srcs[i].at[next_layer, pl.ds(k * rows, rows), :],
                                     stage_in[i], sem.at[i])

    def write(k, i):
        return pltpu.make_async_copy(stage_out[i], dsts[i].at[pl.ds(k * rows, rows), :],
                                     sem.at[2 + i])

    @pl.when(t >= 1)
    def _():
        for i in range(2):
            read(t - 1, i).wait()

        @pl.when(t >= 2)
        def _():
            for i in range(2):
                write(t - 2, i).wait()

        for i in range(2):
            stage_out[i][...] = stage_in[i][...].astype(BF16)
            write(t - 1, i).start()

    @pl.when(t < last)
    def _():
        for i in range(2):
            read(t, i).start()

    @pl.when(t == last)
    def _():
        for i in range(2):
            write(t - 1, i).wait()


def _prompt_kernel_casting(sinks_ref, x_ref, xh_ref, win_ref, wout_ref, cw_ref, g_ref, b_ref,
                           win_f32, wout_f32,
                           y_ref, yh_ref, kl_ref, vl_ref, cs_ref, win_next, wout_next,
                           *scratch, layer, **static):
    cast_in_a, cast_in_b, cast_out_a, cast_out_b, cast_sem = scratch[-5:]
    _cast_next_weights(pl.program_id(0), pl.num_programs(0) - 1, layer + 1, win_f32, wout_f32,
                       win_next, wout_next, (cast_in_a, cast_in_b), (cast_out_a, cast_out_b),
                       cast_sem)
    _prompt_kernel(sinks_ref, x_ref, xh_ref, win_ref, wout_ref, cw_ref, g_ref, b_ref,
                   y_ref, yh_ref, kl_ref, vl_ref, cs_ref, *scratch[:-5], layer=layer, **static)


def _prompt_kernel(sinks_ref, x_ref, xh_ref, win_ref, wout_ref, cw_ref, g_ref, b_ref,
                   y_ref, yh_ref, kl_ref, vl_ref, cs_ref,
                   kvar, vtvar, khead, vhead, ucar, uhead, bias_ref, mix_ref, p_scr, gate_ref,
                   z_scr, *, layer, tm, n_tiles, alpha):
    t = pl.program_id(0)
    last = pl.num_programs(0) - 1
    j = t % n_tiles
    shared = (layer, sinks_ref, win_ref, wout_ref, cw_ref, kvar, vtvar, ucar, bias_ref, mix_ref,
              p_scr, gate_ref, alpha)
    key_row = lax.broadcasted_iota(jnp.int32, (2 * WINDOW, WINDOW), 0)

    @pl.when(t == 0)
    def _():
        qi = lax.broadcasted_iota(jnp.int32, (2 * WINDOW, WINDOW), 1)
        dist = WINDOW + qi - key_row
        visible = (dist >= 0) & (dist < WINDOW)
        bias_ref[...] = jnp.where(visible, -dist.astype(F32), NEG_INF)
        z_scr[...] = jnp.zeros(z_scr.shape, F32)
        kvar[:, 0:WINDOW, :] = jnp.zeros((2 * N_KV_HEADS, WINDOW, LANES), BF16)
        vtvar[:, :, 0:WINDOW] = jnp.zeros((2 * N_KV_HEADS, LANES, WINDOW), BF16)
        ucar[...] = jnp.zeros(ucar.shape, F32)
        kmask = jnp.where(key_row < WINDOW + HEAD_PAD, NEG_INF, 0.0)
        xh = jnp.concatenate([jnp.zeros((HEAD_PAD, D_MODEL), F32), xh_ref[...]], axis=0)
        zh, _, _ = _prompt_rows(xh, WINDOW, kmask, lambda: None, *shared)
        yh_ref[...] = _layer_norm(zh, g_ref[...], b_ref[...])[HEAD_PAD:, :]
        khead[...] = kvar[:, 0:WINDOW, :]
        vhead[...] = vtvar[:, :, 0:WINDOW]
        uhead[...] = ucar[...]

    def norm_previous_tile():
        y_ref[...] = _layer_norm(z_scr[...], g_ref[...], b_ref[...])

    @pl.when(t < last)
    def _():
        @pl.when(j == 0)
        def _():
            kvar[:, 0:WINDOW, :] = khead[...]
            vtvar[:, :, 0:WINDOW] = vhead[...]
            ucar[...] = uhead[...]

        kmask = jnp.where(key_row < HEAD_PAD, jnp.where(j == 0, NEG_INF, 0.0), 0.0)
        z, kf, vf = _prompt_rows(x_ref[...], tm, kmask, norm_previous_tile, *shared)
        z_scr[...] = z

        @pl.when(j == n_tiles - 1)
        def _():
            kl_ref[...] = kf[tm - WINDOW:tm, :]
            vl_ref[...] = vf[tm - WINDOW:tm, :]
            cs_ref[...] = ucar[...]

    @pl.when(t == last)
    def _():
        norm_previous_tile()


def _resident(shape, index_map):
    return pl.BlockSpec(shape, index_map, pipeline_mode=pl.Buffered(1))


def _prompt_layer(layer, x, xh, w_in_b, w_out_b, w_in, w_out, conv_w, sinks, ln_g, ln_b, alpha):
    batch, seq, d = x.shape
    depth = w_in.shape[0]
    tm = PROMPT_TILE
    n_tiles = seq // tm
    total = batch * n_tiles
    const2 = lambda t: (0, 0)
    this_layer = lambda t: (layer, 0, 0)

    def tile_block(t):
        t = jnp.minimum(t, total - 1)
        return (t // n_tiles, t % n_tiles, 0)

    def prev_tile_block(t):
        return tile_block(jnp.maximum(t - 1, 0))

    per_batch = lambda t: (jnp.minimum(t, total - 1) // n_tiles, 0, 0)
    static = dict(layer=layer, tm=tm, n_tiles=n_tiles, alpha=alpha)
    operands = [sinks, x, xh, w_in_b, w_out_b, conv_w, ln_g, ln_b]
    in_specs = [
        pl.BlockSpec(memory_space=pltpu.SMEM),
        pl.BlockSpec((None, tm, d), tile_block),
        _resident((N_META, d), const2),
        _resident((d, PROJ_DIM), const2),
        _resident((ATT_DIM + CONV_DIM, d), const2),
        pl.BlockSpec((None, 3, CONV_DIM), this_layer),
        pl.BlockSpec((None, 1, d), this_layer),
        pl.BlockSpec((None, 1, d), this_layer),
    ]
    out_specs = [
        pl.BlockSpec((None, tm, d), prev_tile_block),
        pl.BlockSpec((N_META, d), const2),
        pl.BlockSpec((None, WINDOW, KV_DIM), per_batch),
        pl.BlockSpec((None, WINDOW, KV_DIM), per_batch),
        pl.BlockSpec((None, 8, CONV_DIM), per_batch),
    ]
    out_shape = [
        jax.ShapeDtypeStruct((batch, seq, d), F32),
        jax.ShapeDtypeStruct((N_META, d), F32),
        jax.ShapeDtypeStruct((batch, WINDOW, KV_DIM), F32),
        jax.ShapeDtypeStruct((batch, WINDOW, KV_DIM), F32),
        jax.ShapeDtypeStruct((batch, 8, CONV_DIM), F32),
    ]
    scratch_shapes = [
        pltpu.VMEM((2 * N_KV_HEADS, WINDOW + tm, LANES), BF16),
        pltpu.VMEM((2 * N_KV_HEADS, LANES, WINDOW + tm), BF16),
        pltpu.VMEM((2 * N_KV_HEADS, WINDOW, LANES), BF16),
        pltpu.VMEM((2 * N_KV_HEADS, LANES, WINDOW), BF16),
        pltpu.VMEM((8, CONV_DIM), F32),
        pltpu.VMEM((8, CONV_DIM), F32),
        pltpu.VMEM((2 * WINDOW, WINDOW), F32),
        pltpu.VMEM((tm, ATT_DIM + CONV_DIM), BF16),
        pltpu.VMEM((N_HEADS, 2 * WINDOW, WINDOW), BF16),
        pltpu.VMEM((tm, ATT_DIM), F32),
        pltpu.VMEM((tm, d), F32),
    ]
    kernel = _prompt_kernel
    if layer + 1 < depth:
        cast_rows = d // total
        assert cast_rows * total == d and cast_rows % 16 == 0
        kernel = _prompt_kernel_casting
        operands += [w_in, w_out]
        in_specs += [pl.BlockSpec(memory_space=pl.ANY)] * 2
        out_specs += [pl.BlockSpec(memory_space=pl.ANY)] * 2
        out_shape += [jax.ShapeDtypeStruct(w_in_b.shape, BF16),
                      jax.ShapeDtypeStruct(w_out_b.shape, BF16)]
        scratch_shapes += [
            pltpu.VMEM((cast_rows, PROJ_DIM), F32),
            pltpu.VMEM((cast_rows, d), F32),
            pltpu.VMEM((cast_rows, PROJ_DIM), BF16),
            pltpu.VMEM((cast_rows, d), BF16),
            pltpu.SemaphoreType.DMA((4,)),
        ]
    return pl.pallas_call(
        functools.partial(kernel, **static),
        grid=(total + 1,),
        in_specs=in_specs,
        out_specs=out_specs,
        out_shape=out_shape,
        scratch_shapes=scratch_shapes,
        compiler_params=pltpu.CompilerParams(
            dimension_semantics=("arbitrary",),
            vmem_limit_bytes=VMEM_LIMIT,
        ),
        name="prompt_layer",
    )(*operands)


SEQ_PER_GROUP = 2
GROUP_ROWS = SEQ_PER_GROUP * SAMPLE_ROWS
GROUPS_PER_STEP = 2
SEQ_PER_STEP = SEQ_PER_GROUP * GROUPS_PER_STEP
N_NEW = 4
TOK0 = SAMPLE_ROWS - N_NEW
LOG_SAMPLE_ROWS = 3
LOG_GROUP_ROWS = 4
LOG_HEAD_DIM = 6
assert (1 << LOG_SAMPLE_ROWS, 1 << LOG_GROUP_ROWS, 1 << LOG_HEAD_DIM) == (
    SAMPLE_ROWS, GROUP_ROWS, HEAD_DIM)


def _sample_kernel(sinks_ref, x_ref, win_ref, wout_ref, cw_ref, g_ref, b_ref, ck_ref, cv_ref,
                   st_ref, kb_prev, vb_prev, y_ref, kb_ref, vb_ref, u_ref,
                   qe, knew, vnew, gate_a, bias_ref, sinkcol, mix_ref, *, layer, n_rows, alpha):
    del kb_prev, vb_prev
    grp = pl.program_id(0)
    n_steps = pl.num_programs(0)
    n_q = N_HEADS * GROUP_ROWS
    lane_blk = lax.broadcasted_iota(jnp.int32, (n_rows, KV_DIM), 1) >> LOG_HEAD_DIM

    @pl.when(grp == 0)
    def _():
        x = x_ref[...]
        xb = x.astype(BF16)
        hq = _proj(xb, win_ref, 0, C_GA)
        q = hq[:, C_Q:C_Q + ATT_DIM] * Q_SCALE
        knew[...] = hq[:, C_K:C_K + KV_DIM]
        vnew[...] = hq[:, C_V:C_V + KV_DIM]
        for hd in range(N_HEADS):
            h, g = divmod(hd, GQA_GROUP)
            slab = q[:, h * KV_DIM:(h + 1) * KV_DIM]
            moved = pltpu.roll(slab, ((h - g) % GQA_GROUP) * HEAD_DIM, axis=1)
            qe[hd] = jnp.where(lane_blk == h, moved, 0.0).astype(BF16)
        gate_a[...] = _silu(_proj(xb, win_ref, C_GA, ATT_DIM))

        bg = _proj(xb, win_ref, C_B, CONV_DIM)
        u = _proj(xb, win_ref, C_C, CONV_DIM) * _proj(xb, win_ref, C_H, CONV_DIM)
        r8 = lax.broadcasted_iota(jnp.int32, (n_rows, CONV_DIM), 0) & (SAMPLE_ROWS - 1)
        is_state = (r8 >= TOK0 - 2) & (r8 < TOK0)
        u = jnp.where(is_state, st_ref[...], u)
        u_ref[...] = u
        cy = (cw_ref[0:1, :] * pltpu.roll(u, 2, axis=0) + cw_ref[1:2, :] * pltpu.roll(u, 1, axis=0)
              + cw_ref[2:3, :] * u)
        gate_c = _silu(_proj(xb, win_ref, C_GC, CONV_DIM))
        mix_ref[:, ATT_DIM:] = ((bg * cy) * gate_c).astype(BF16)

        qrow = lax.broadcasted_iota(jnp.int32, (n_q, 2 * WINDOW), 0)
        key = lax.broadcasted_iota(jnp.int32, (n_q, 2 * WINDOW), 1)
        q_tok = jnp.maximum((qrow & (SAMPLE_ROWS - 1)) - TOK0, 0)
        q_seq = (qrow >> LOG_SAMPLE_ROWS) & (SEQ_PER_GROUP - 1)
        new = key - WINDOW
        k_tok = (new & (SAMPLE_ROWS - 1)) - TOK0
        k_seq = new >> LOG_SAMPLE_ROWS
        cached = key < WINDOW
        dist = jnp.where(cached, WINDOW + q_tok - key, q_tok - k_tok)
        ok_new = (new >= 0) & (new < GROUP_ROWS) & (k_seq == q_seq) & (k_tok >= 0)
        visible = (dist >= 0) & (dist < WINDOW) & (cached | ok_new)
        slope = jnp.zeros((n_q, 2 * WINDOW), F32)
        sink = jnp.zeros((n_q, LANES), F32)
        srow = lax.broadcasted_iota(jnp.int32, (n_q, LANES), 0)
        for hd in range(N_HEADS):
            slope = jnp.where((qrow >> LOG_GROUP_ROWS) == hd, SLOPES[hd], slope)
            sink = jnp.where((srow >> LOG_GROUP_ROWS) == hd, sinks_ref[layer, hd], sink)
        bias_ref[...] = jnp.where(visible, -(slope * dist.astype(F32)), NEG_INF)
        sinkcol[...] = sink

    q_seq = ((lax.broadcasted_iota(jnp.int32, (n_q, KV_DIM), 0) >> LOG_SAMPLE_ROWS)
             & (SEQ_PER_GROUP - 1))
    newest = lax.broadcasted_iota(jnp.int32, (KV_DIM, WINDOW), 1) >= WINDOW - N_NEW
    blk16 = lax.broadcasted_iota(jnp.int32, (GROUP_ROWS, KV_DIM), 1) >> LOG_HEAD_DIM
    pad_rows = jnp.zeros((WINDOW - GROUP_ROWS, KV_DIM), F32)
    sink = sinkcol[:, 0:1]

    probs, v_new_bs, row0s = [], [], []
    for gi in range(GROUPS_PER_STEP):
        g0 = pl.multiple_of((grp * GROUPS_PER_STEP + gi) * GROUP_ROWS, GROUP_ROWS)
        row0s.append(g0)
        w_g = jnp.concatenate([qe[hd, pl.ds(g0, GROUP_ROWS), :] for hd in range(N_HEADS)], axis=0)
        k_new = jnp.concatenate([knew[pl.ds(g0, GROUP_ROWS), :], pad_rows], axis=0)
        v_new = jnp.concatenate([vnew[pl.ds(g0, GROUP_ROWS), :], pad_rows], axis=0)
        k_new_b = k_new.astype(BF16)
        v_new_bs.append(v_new.astype(BF16))
        k_new_t = k_new.T
        v_new_t = v_new.T
        for s in range(SEQ_PER_GROUP):
            n = gi * SEQ_PER_GROUP + s
            ck_t = ck_ref[n]
            sc = jnp.concatenate(
                [jnp.dot(w_g, ck_t.astype(BF16), preferred_element_type=F32),
                 lax.dot_general(w_g, k_new_b, _NT, preferred_element_type=F32)], axis=1)
            probs.append(_softmax_rows(sc + bias_ref[...], sink).astype(BF16))
            to_tail = WINDOW - N_NEW - (s * SAMPLE_ROWS + TOK0)
            for cache_t, new_t, out_ref in ((ck_t, k_new_t, kb_ref), (cv_ref[n], v_new_t, vb_ref)):
                out_ref[n] = jnp.where(newest, pltpu.roll(new_t, to_tail, axis=1),
                                       pltpu.roll(cache_t, WINDOW - N_NEW, axis=1))

    for gi in range(GROUPS_PER_STEP):
        o_grp = jnp.zeros((n_q, KV_DIM), F32)
        for s in range(SEQ_PER_GROUP):
            n = gi * SEQ_PER_GROUP + s
            p = probs[n]
            o = (lax.dot_general(p[:, :WINDOW], cv_ref[n].astype(BF16), _NT,
                                 preferred_element_type=F32)
                 + jnp.dot(p[:, WINDOW:], v_new_bs[gi], preferred_element_type=F32))
            o_grp = jnp.where(q_seq == s, o, o_grp)
        for h in range(N_KV_HEADS):
            slab = jnp.zeros((GROUP_ROWS, KV_DIM), F32)
            for g in range(GQA_GROUP):
                hd = h * GQA_GROUP + g
                piece = jnp.where(blk16 == h, o_grp[hd * GROUP_ROWS:(hd + 1) * GROUP_ROWS, :], 0.0)
                slab = slab + pltpu.roll(piece, ((g - h) % GQA_GROUP) * HEAD_DIM, axis=1)
            rows = pl.ds(row0s[gi], GROUP_ROWS)
            cols = slice(h * KV_DIM, (h + 1) * KV_DIM)
            mix_ref[rows, cols] = (slab * gate_a[rows, cols]).astype(BF16)

    @pl.when(grp == n_steps - 1)
    def _():
        out = jnp.dot(mix_ref[...], wout_ref[...], preferred_element_type=F32)
        y_ref[...] = _layer_norm(alpha * x_ref[...] + out, g_ref[...], b_ref[...])


def _sample_layer(layer, x8, st_all, ck_all, cv_all, kb_all, vb_all, w_in, w_out, conv_w, sinks,
                  ln_g, ln_b, alpha):
    n_rows, d = x8.shape
    depth, n_seq = ck_all.shape[:2]
    this_layer = lambda g: (layer, 0, 0)
    n_steps = n_seq // SEQ_PER_STEP
    n_q = N_HEADS * GROUP_ROWS
    const2 = lambda g: (0, 0)
    cache_spec = pl.BlockSpec((None, SEQ_PER_STEP, KV_DIM, WINDOW), lambda g: (layer, g, 0, 0))
    kernel = functools.partial(_sample_kernel, layer=layer, n_rows=n_rows, alpha=alpha)
    operands = [sinks, x8, w_in, w_out, conv_w, ln_g, ln_b, ck_all, cv_all, st_all, kb_all, vb_all]
    in_specs = [
        pl.BlockSpec(memory_space=pltpu.SMEM),
        _resident((n_rows, d), const2),
        _resident((d, PROJ_DIM), const2),
        _resident((ATT_DIM + CONV_DIM, d), const2),
        pl.BlockSpec((None, 3, CONV_DIM), this_layer),
        pl.BlockSpec((None, 1, d), this_layer),
        pl.BlockSpec((None, 1, d), this_layer),
        cache_spec,
        cache_spec,
        _resident((None, n_rows, CONV_DIM), this_layer),
        pl.BlockSpec(memory_space=pl.ANY),
        pl.BlockSpec(memory_space=pl.ANY),
    ]
    aliases = {len(operands) - 2: 1, len(operands) - 1: 2}
    return pl.pallas_call(
        kernel,
        grid=(n_steps,),
        in_specs=in_specs,
        out_specs=[
            pl.BlockSpec((n_rows, d), const2),
            cache_spec,
            cache_spec,
            pl.BlockSpec((n_rows, CONV_DIM), const2),
        ],
        out_shape=[
            jax.ShapeDtypeStruct((n_rows, d), F32),
            jax.ShapeDtypeStruct(ck_all.shape, F32),
            jax.ShapeDtypeStruct(cv_all.shape, F32),
            jax.ShapeDtypeStruct((n_rows, CONV_DIM), F32),
        ],
        input_output_aliases=aliases,
        scratch_shapes=[
            pltpu.VMEM((N_HEADS, n_rows, KV_DIM), BF16),
            pltpu.VMEM((n_rows, KV_DIM), F32),
            pltpu.VMEM((n_rows, KV_DIM), F32),
            pltpu.VMEM((n_rows, ATT_DIM), F32),
            pltpu.VMEM((n_q, 2 * WINDOW), F32),
            pltpu.VMEM((n_q, LANES), F32),
            pltpu.VMEM((n_rows, ATT_DIM + CONV_DIM), BF16),
        ],
        compiler_params=pltpu.CompilerParams(
            dimension_semantics=("arbitrary",),
            vmem_limit_bytes=VMEM_LIMIT,
        ),
        name="sample_layer",
    )(*operands)


def kernel(x_prompt, x_sample, cache_k, cache_v, state_conv, meta_tokens,
           w_in, conv_w, sinks, w_out, ln_g, ln_b):
    depth = w_in.shape[0]
    alpha = float((2 * depth) ** 0.25)
    batch, seq, d = x_prompt.shape
    n_seq, n_tok = x_sample.shape[:2]
    assert d == D_MODEL and seq % PROMPT_TILE == 0 and n_tok == SAMPLE_ROWS - TOK0
    assert meta_tokens.shape[0] == N_META and n_seq % SEQ_PER_STEP == 0
    assert cache_k.shape[2] == WINDOW and state_conv.shape[2] == 2

    w_in_b = w_in[0].astype(BF16)
    w_out_b = w_out[0].astype(BF16)
    ln_g3 = ln_g.reshape(depth, 1, d)
    ln_b3 = ln_b.reshape(depth, 1, d)
    xp = x_prompt
    xh = meta_tokens.astype(F32)
    xs = jnp.pad(x_sample, ((0, 0), (TOK0, 0), (0, 0))).reshape(n_seq * SAMPLE_ROWS, d)
    ck_all = jnp.transpose(cache_k, (0, 1, 3, 4, 2)).reshape(depth, n_seq, KV_DIM, WINDOW)
    cv_all = jnp.transpose(cache_v, (0, 1, 3, 4, 2)).reshape(depth, n_seq, KV_DIM, WINDOW)
    st_all = jnp.pad(state_conv, ((0, 0), (0, 0), (TOK0 - 2, SAMPLE_ROWS - TOK0), (0, 0)))
    st_all = st_all.reshape(depth, n_seq * SAMPLE_ROWS, CONV_DIM)

    kp, vp, cp, cs = [], [], [], []
    kb_all = jnp.zeros(ck_all.shape, F32)
    vb_all = jnp.zeros(cv_all.shape, F32)
    for l in range(depth):
        xp, xh, k_last, v_last, c_last, *next_weights = _prompt_layer(
            l, xp, xh, w_in_b, w_out_b, w_in, w_out, conv_w, sinks, ln_g3, ln_b3, alpha)
        kp.append(k_last.reshape(batch, WINDOW, N_KV_HEADS, HEAD_DIM))
        vp.append(v_last.reshape(batch, WINDOW, N_KV_HEADS, HEAD_DIM))
        cp.append(c_last[:, 6:8, :])
        xs, kb_all, vb_all, u8 = _sample_layer(
            l, xs, st_all, ck_all, cv_all, kb_all, vb_all, w_in_b, w_out_b, conv_w, sinks, ln_g3,
            ln_b3, alpha)
        if next_weights:
            w_in_b, w_out_b = next_weights
        cs.append(u8.reshape(n_seq, SAMPLE_ROWS, CONV_DIM)[:, SAMPLE_ROWS - 2:, :])
    y_sample = xs.reshape(n_seq, SAMPLE_ROWS, d)[:, TOK0:, :]
    kv_shape = (depth, n_seq, N_KV_HEADS, HEAD_DIM, WINDOW)
    k_sample = jnp.transpose(kb_all.reshape(kv_shape), (0, 1, 4, 2, 3))
    v_sample = jnp.transpose(vb_all.reshape(kv_shape), (0, 1, 4, 2, 3))
    return (xp, y_sample, jnp.stack(kp), jnp.stack(vp), jnp.stack(cp),
            k_sample, v_sample, jnp.stack(cs))
```

```python
import functools

import numpy as np
import jax
import jax.numpy as jnp
from jax import lax
from jax.experimental import pallas as pl
from jax.experimental.pallas import tpu as pltpu

F32 = jnp.float32
BF16 = jnp.bfloat16

D_MODEL = 2048
N_META = 16
ATT_DIM = 1024
CONV_DIM = 1024
HEAD_DIM = 64
N_HEADS = 16
N_KV_HEADS = 4
GQA_GROUP = N_HEADS // N_KV_HEADS
KV_DIM = N_KV_HEADS * HEAD_DIM
WINDOW = 128
PROJ_DIM = 2 * ATT_DIM + 2 * KV_DIM + 4 * CONV_DIM
LN_EPS = 1e-5
NEG_INF = -1e30
Q_SCALE = HEAD_DIM ** -0.5

C_Q = 0
C_K = ATT_DIM
C_V = C_K + KV_DIM
C_GA = C_V + KV_DIM
C_B = C_GA + ATT_DIM
C_C = C_B + CONV_DIM
C_H = C_C + CONV_DIM
C_GC = C_H + CONV_DIM

LANES = 128
HEAD_PAD = WINDOW - N_META
PROMPT_TILE = 256
FILL_CHUNK = 256
SAMPLE_ROWS = 8
VMEM_LIMIT = 58 * 1024 * 1024

SLOPES = [float(np.float32(2.0 ** (-8.0 * (h + 1) / N_HEADS))) for h in range(N_HEADS)]

_NT = (((1,), (1,)), ((), ()))


def _silu(g):
    return g * (1.0 / (1.0 + jnp.exp(-g)))


def _softmax_rows(s, sink):
    m = jnp.maximum(jnp.max(s, axis=1, keepdims=True), sink)
    p = jnp.exp(s - m)
    denom = jnp.sum(p, axis=1, keepdims=True) + jnp.exp(sink - m)
    return p * (1.0 / denom)


def _softmax_cols(s, sink):
    m = jnp.maximum(jnp.max(s, axis=0, keepdims=True), sink)
    p = jnp.exp(s - m)
    denom = jnp.sum(p, axis=0, keepdims=True) + jnp.exp(sink - m)
    return p * (1.0 / denom)


def _layer_norm(z, g, b):
    mu = jnp.mean(z, axis=1, keepdims=True)
    zc = z - mu
    var = jnp.mean(zc * zc, axis=1, keepdims=True)
    return zc * lax.rsqrt(var + LN_EPS) * g + b


def _proj(xb, w_ref, c0, width):
    return jnp.dot(xb, w_ref[:, c0:c0 + width], preferred_element_type=F32)


def _store_k_variants(src, dst, r0, rows):
    low = lax.broadcasted_iota(jnp.int32, (rows, LANES), 1) < HEAD_DIM
    for cc in range(KV_DIM // LANES):
        col = src[:, cc * LANES:(cc + 1) * LANES]
        swapped = pltpu.roll(col, HEAD_DIM, axis=1)
        h_even, h_odd = 2 * cc, 2 * cc + 1
        dst[2 * h_even + 0, r0:r0 + rows, :] = jnp.where(low, col, 0.0).astype(BF16)
        dst[2 * h_even + 1, r0:r0 + rows, :] = jnp.where(low, 0.0, swapped).astype(BF16)
        dst[2 * h_odd + 0, r0:r0 + rows, :] = jnp.where(low, swapped, 0.0).astype(BF16)
        dst[2 * h_odd + 1, r0:r0 + rows, :] = jnp.where(low, 0.0, col).astype(BF16)


def _conv_chunk_pieces(xb, win_ref, cw_ref, ucar, mix_ref, rows, lo, width):
    cols = slice(lo, lo + width)
    got = {}

    def project(name, c0):
        got[name] = _proj(xb, win_ref, c0 + lo, width)

    def finish():
        u = got["c"] * got["h"]
        row = lax.broadcasted_iota(jnp.int32, (rows, width), 0)
        prev1 = ucar[7:8, cols]
        prev2 = ucar[6:7, cols]
        u1 = jnp.where(row == 0, prev1, pltpu.roll(u, 1, axis=0))
        u2 = jnp.where(row == 0, prev2, jnp.where(row == 1, prev1, pltpu.roll(u, 2, axis=0)))
        cy = cw_ref[0:1, cols] * u2 + cw_ref[1:2, cols] * u1 + cw_ref[2:3, cols] * u
        ucar[:, cols] = u[rows - 8:rows, :]
        gate_c = _silu(_proj(xb, win_ref, C_GC + lo, width))
        mix_ref[0:rows, ATT_DIM + lo:ATT_DIM + lo + width] = (
            (got["b"] * cy) * gate_c).astype(BF16)

    return [functools.partial(project, "b", C_B), functools.partial(project, "c", C_C),
            functools.partial(project, "h", C_H), finish]


def _store_v_variants(src, dst, c0, rows):
    vt = src.T
    zeros = jnp.zeros((HEAD_DIM, rows), F32)
    for h in range(N_KV_HEADS):
        vh = vt[h * HEAD_DIM:(h + 1) * HEAD_DIM, :]
        dst[2 * h + 0, :, c0:c0 + rows] = jnp.concatenate([vh, zeros], axis=0).astype(BF16)
        dst[2 * h + 1, :, c0:c0 + rows] = jnp.concatenate([zeros, vh], axis=0).astype(BF16)


def _scores_group(layer, qb, kmask, sinks_ref, kvar, bias_ref, p_scr, r0, h):
    q4 = jnp.concatenate(
        [qb[r0:r0 + WINDOW, c * LANES:(c + 1) * LANES] for c in (2 * h, 2 * h + 1)], axis=0)
    for par in range(2):
        keys = kvar[2 * h + par, r0:r0 + 2 * WINDOW, :]
        s2 = lax.dot_general(keys, q4, _NT, preferred_element_type=F32)
        for half in range(2):
            hd = GQA_GROUP * h + 2 * half + par
            lanes = slice(half * LANES, (half + 1) * LANES)
            s = s2[:, lanes] + SLOPES[hd] * bias_ref[...]
            if kmask is not None:
                s = s + kmask
            p_scr[2 * h + par, :, lanes] = _softmax_cols(s, sinks_ref[layer, hd]).astype(BF16)


def _values_group(gate_ref, vtvar, p_scr, mix_ref, r0, h):
    o_t = None
    for par in range(2):
        vals_t = vtvar[2 * h + par, :, r0:r0 + 2 * WINDOW]
        o = jnp.dot(vals_t, p_scr[2 * h + par], preferred_element_type=F32)
        o_t = o if o_t is None else o_t + o
    for half in range(2):
        lanes = slice((2 * h + half) * LANES, (2 * h + half + 1) * LANES)
        o = o_t[:, half * LANES:(half + 1) * LANES].T
        mix_ref[r0:r0 + WINDOW, lanes] = (o * gate_ref[r0:r0 + WINDOW, lanes]).astype(BF16)


def _gate_chunk(xb, win_ref, gate_ref, rows, lo, width):
    gate_ref[0:rows, lo:lo + width] = _silu(_proj(xb, win_ref, C_GA + lo, width))


def _prompt_rows(x, rows, kmask0, after_qkv, layer, sinks_ref, win_ref, wout_ref, cw_ref,
                 kvar, vtvar, ucar, bias_ref, mix_ref, p_scr, gate_ref, alpha):
    xb = x.astype(BF16)
    hq = _proj(xb, win_ref, 0, C_GA)
    after_qkv()
    qb = (hq[:, C_Q:C_Q + ATT_DIM] * Q_SCALE).astype(BF16)
    kf = hq[:, C_K:C_K + KV_DIM]
    vf = hq[:, C_V:C_V + KV_DIM]
    _store_k_variants(kf, kvar, WINDOW, rows)
    _store_v_variants(vf, vtvar, WINDOW, rows)

    n_blocks = rows // WINDOW
    gates = [functools.partial(_gate_chunk, xb, win_ref, gate_ref, rows, lo, FILL_CHUNK)
             for lo in range(0, ATT_DIM, FILL_CHUNK)]
    convs = [_conv_chunk_pieces(xb, win_ref, cw_ref, ucar, mix_ref, rows, lo, FILL_CHUNK)
             for lo in range(0, CONV_DIM, FILL_CHUNK)]
    fillers = gates[:2] + convs[0] + gates[2:] + [p for conv in convs[1:] for p in conv]
    n_units = n_blocks * N_KV_HEADS
    cuts = [len(fillers) * u // n_units for u in range(n_units + 1)]
    assert cuts[N_KV_HEADS] >= len(gates) + len(convs[0])
    for blk in range(n_blocks):
        r0 = blk * WINDOW
        for h in range(N_KV_HEADS):
            unit = blk * N_KV_HEADS + h
            for filler in fillers[cuts[unit]:cuts[unit + 1]]:
                filler()
            _scores_group(layer, qb, kmask0 if blk == 0 else None, sinks_ref, kvar, bias_ref,
                          p_scr, r0, h)
        for h in range(N_KV_HEADS):
            _values_group(gate_ref, vtvar, p_scr, mix_ref, r0, h)

    out = jnp.dot(mix_ref[0:rows, :], wout_ref[...], preferred_element_type=F32)
    z = alpha * x + out

    for i in range(2 * N_KV_HEADS):
        kvar[i, 0:WINDOW, :] = kvar[i, rows:rows + WINDOW, :]
        vtvar[i, :, 0:WINDOW] = vtvar[i, :, rows:rows + WINDOW]
    return z, kf, vf


def _cast_next_weights(t, last, next_layer, win_f32, wout_f32, win_next, wout_next,
                       stage_in, stage_out, sem):
    rows = stage_in[0].shape[0]
    srcs = (win_f32, wout_f32)
    dsts = (win_next, wout_next)

    def read(k, i):
        return pltpu.make_async_copy(srcs[i].at[next_layer, pl.ds(k * rows, rows), :],
                                     stage_in[i], sem.at[i])

    def write(k, i):
        return pltpu.make_async_copy(stage_out[i], dsts[i].at[pl.ds(k * rows, rows), :],
                                     sem.at[2 + i])

    @pl.when(t >= 1)
    def _():
        for i in range(2):
            read(t - 1, i).wait()

        @pl.when(t >= 2)
        def _():
            for i in range(2):
                write(t - 2, i).wait()

        for i in range(2):
            stage_out[i][...] = stage_in[i][...].astype(BF16)
            write(t - 1, i).start()

    @pl.when(t < last)
    def _():
        for i in range(2):
            read(t, i).start()

    @pl.when(t == last)
    def _():
        for i in range(2):
            write(t - 1, i).wait()


def _prompt_kernel_casting(sinks_ref, x_ref, xh_ref, win_ref, wout_ref, cw_ref, g_ref, b_ref,
                           win_f32, wout_f32,
                           y_ref, yh_ref, kl_ref, vl_ref, cs_ref, win_next, wout_next,
                           *scratch, layer, **static):
    cast_in_a, cast_in_b, cast_out_a, cast_out_b, cast_sem = scratch[-5:]
    _cast_next_weights(pl.program_id(0), pl.num_programs(0) - 1, layer + 1, win_f32, wout_f32,
                       win_next, wout_next, (cast_in_a, cast_in_b), (cast_out_a, cast_out_b),
                       cast_sem)
    _prompt_kernel(sinks_ref, x_ref, xh_ref, win_ref, wout_ref, cw_ref, g_ref, b_ref,
                   y_ref, yh_ref, kl_ref, vl_ref, cs_ref, *scratch[:-5], layer=layer, **static)


def _prompt_kernel(sinks_ref, x_ref, xh_ref, win_ref, wout_ref, cw_ref, g_ref, b_ref,
                   y_ref, yh_ref, kl_ref, vl_ref, cs_ref,
                   kvar, vtvar, khead, vhead, ucar, uhead, bias_ref, mix_ref, p_scr, gate_ref,
                   z_scr, *, layer, tm, n_tiles, alpha):
    t = pl.program_id(0)
    last = pl.num_programs(0) - 1
    j = t % n_tiles
    shared = (layer, sinks_ref, win_ref, wout_ref, cw_ref, kvar, vtvar, ucar, bias_ref, mix_ref,
              p_scr, gate_ref, alpha)
    key_row = lax.broadcasted_iota(jnp.int32, (2 * WINDOW, WINDOW), 0)

    @pl.when(t == 0)
    def _():
        qi = lax.broadcasted_iota(jnp.int32, (2 * WINDOW, WINDOW), 1)
        dist = WINDOW + qi - key_row
        visible = (dist >= 0) & (dist < WINDOW)
        bias_ref[...] = jnp.where(visible, -dist.astype(F32), NEG_INF)
        z_scr[...] = jnp.zeros(z_scr.shape, F32)
        kvar[:, 0:WINDOW, :] = jnp.zeros((2 * N_KV_HEADS, WINDOW, LANES), BF16)
        vtvar[:, :, 0:WINDOW] = jnp.zeros((2 * N_KV_HEADS, LANES, WINDOW), BF16)
        ucar[...] = jnp.zeros(ucar.shape, F32)
        kmask = jnp.where(key_row < WINDOW + HEAD_PAD, NEG_INF, 0.0)
        xh = jnp.concatenate([jnp.zeros((HEAD_PAD, D_MODEL), F32), xh_ref[...]], axis=0)
        zh, _, _ = _prompt_rows(xh, WINDOW, kmask, lambda: None, *shared)
        yh_ref[...] = _layer_norm(zh, g_ref[...], b_ref[...])[HEAD_PAD:, :]
        khead[...] = kvar[:, 0:WINDOW, :]
        vhead[...] = vtvar[:, :, 0:WINDOW]
        uhead[...] = ucar[...]

    def norm_previous_tile():
        y_ref[...] = _layer_norm(z_scr[...], g_ref[...], b_ref[...])

    @pl.when(t < last)
    def _():
        @pl.when(j == 0)
        def _():
            kvar[:, 0:WINDOW, :] = khead[...]
            vtvar[:, :, 0:WINDOW] = vhead[...]
            ucar[...] = uhead[...]

        kmask = jnp.where(key_row < HEAD_PAD, jnp.where(j == 0, NEG_INF, 0.0), 0.0)
        z, kf, vf = _prompt_rows(x_ref[...], tm, kmask, norm_previous_tile, *shared)
        z_scr[...] = z

        @pl.when(j == n_tiles - 1)
        def _():
            kl_ref[...] = kf[tm - WINDOW:tm, :]
            vl_ref[...] = vf[tm - WINDOW:tm, :]
            cs_ref[...] = ucar[...]

    @pl.when(t == last)
    def _():
        norm_previous_tile()


def _resident(shape, index_map):
    return pl.BlockSpec(shape, index_map, pipeline_mode=pl.Buffered(1))


def _prompt_layer(layer, x, xh, w_in_b, w_out_b, w_in, w_out, conv_w, sinks, ln_g, ln_b, alpha):
    batch, seq, d = x.shape
    depth = w_in.shape[0]
    tm = PROMPT_TILE
    n_tiles = seq // tm
    total = batch * n_tiles
    const2 = lambda t: (0, 0)
    this_layer = lambda t: (layer, 0, 0)

    def tile_block(t):
        t = jnp.minimum(t, total - 1)
        return (t // n_tiles, t % n_tiles, 0)

    def prev_tile_block(t):
        return tile_block(jnp.maximum(t - 1, 0))

    per_batch = lambda t: (jnp.minimum(t, total - 1) // n_tiles, 0, 0)
    static = dict(layer=layer, tm=tm, n_tiles=n_tiles, alpha=alpha)
    operands = [sinks, x, xh, w_in_b, w_out_b, conv_w, ln_g, ln_b]
    in_specs = [
        pl.BlockSpec(memory_space=pltpu.SMEM),
        pl.BlockSpec((None, tm, d), tile_block),
        _resident((N_META, d), const2),
        _resident((d, PROJ_DIM), const2),
        _resident((ATT_DIM + CONV_DIM, d), const2),
        pl.BlockSpec((None, 3, CONV_DIM), this_layer),
        pl.BlockSpec((None, 1, d), this_layer),
        pl.BlockSpec((None, 1, d), this_layer),
    ]
    out_specs = [
        pl.BlockSpec((None, tm, d), prev_tile_block),
        pl.BlockSpec((N_META, d), const2),
        pl.BlockSpec((None, WINDOW, KV_DIM), per_batch),
        pl.BlockSpec((None, WINDOW, KV_DIM), per_batch),
        pl.BlockSpec((None, 8, CONV_DIM), per_batch),
    ]
    out_shape = [
        jax.ShapeDtypeStruct((batch, seq, d), F32),
        jax.ShapeDtypeStruct((N_META, d), F32),
        jax.ShapeDtypeStruct((batch, WINDOW, KV_DIM), F32),
        jax.ShapeDtypeStruct((batch, WINDOW, KV_DIM), F32),
        jax.ShapeDtypeStruct((batch, 8, CONV_DIM), F32),
    ]
    scratch_shapes = [
        pltpu.VMEM((2 * N_KV_HEADS, WINDOW + tm, LANES), BF16),
        pltpu.VMEM((2 * N_KV_HEADS, LANES, WINDOW + tm), BF16),
        pltpu.VMEM((2 * N_KV_HEADS, WINDOW, LANES), BF16),
        pltpu.VMEM((2 * N_KV_HEADS, LANES, WINDOW), BF16),
        pltpu.VMEM((8, CONV_DIM), F32),
        pltpu.VMEM((8, CONV_DIM), F32),
        pltpu.VMEM((2 * WINDOW, WINDOW), F32),
        pltpu.VMEM((tm, ATT_DIM + CONV_DIM), BF16),
        pltpu.VMEM((2 * N_KV_HEADS, 2 * WINDOW, 2 * WINDOW), BF16),
        pltpu.VMEM((tm, ATT_DIM), F32),
        pltpu.VMEM((tm, d), F32),
    ]
    kernel = _prompt_kernel
    if layer + 1 < depth:
        cast_rows = d // total
        assert cast_rows * total == d and cast_rows % 16 == 0
        kernel = _prompt_kernel_casting
        operands += [w_in, w_out]
        in_specs += [pl.BlockSpec(memory_space=pl.ANY)] * 2
        out_specs += [pl.BlockSpec(memory_space=pl.ANY)] * 2
        out_shape += [jax.ShapeDtypeStruct(w_in_b.shape, BF16),
                      jax.ShapeDtypeStruct(w_out_b.shape, BF16)]
        scratch_shapes += [
            pltpu.VMEM((cast_rows, PROJ_DIM), F32),
            pltpu.VMEM((cast_rows, d), F32),
            pltpu.VMEM((cast_rows, PROJ_DIM), BF16),
            pltpu.VMEM((cast_rows, d), BF16),
            pltpu.SemaphoreType.DMA((4,)),
        ]
    return pl.pallas_call(
        functools.partial(kernel, **static),
        grid=(total + 1,),
        in_specs=in_specs,
        out_specs=out_specs,
        out_shape=out_shape,
        scratch_shapes=scratch_shapes,
        compiler_params=pltpu.CompilerParams(
            dimension_semantics=("arbitrary",),
            vmem_limit_bytes=VMEM_LIMIT,
        ),
        name="prompt_layer",
    )(*operands)


SEQ_PER_GROUP = 2
GROUP_ROWS = SEQ_PER_GROUP * SAMPLE_ROWS
GROUPS_PER_STEP = 2
SEQ_PER_STEP = SEQ_PER_GROUP * GROUPS_PER_STEP
N_NEW = 4
TOK0 = SAMPLE_ROWS - N_NEW
LOG_SAMPLE_ROWS = 3
LOG_GROUP_ROWS = 4
LOG_HEAD_DIM = 6
assert (1 << LOG_SAMPLE_ROWS, 1 << LOG_GROUP_ROWS, 1 << LOG_HEAD_DIM) == (
    SAMPLE_ROWS, GROUP_ROWS, HEAD_DIM)


def _sample_kernel(sinks_ref, x_ref, win_ref, wout_ref, cw_ref, g_ref, b_ref, ck_ref, cv_ref,
                   st_ref, kb_prev, vb_prev, y_ref, kb_ref, vb_ref, u_ref,
                   qe, knew, vnew, gate_a, bias_ref, sinkcol, mix_ref, *, layer, n_rows, alpha):
    del kb_prev, vb_prev
    grp = pl.program_id(0)
    n_steps = pl.num_programs(0)
    n_q = N_HEADS * GROUP_ROWS
    lane_blk = lax.broadcasted_iota(jnp.int32, (n_rows, KV_DIM), 1) >> LOG_HEAD_DIM

    @pl.when(grp == 0)
    def _():
        x = x_ref[...]
        xb = x.astype(BF16)
        hq = _proj(xb, win_ref, 0, C_GA)
        q = hq[:, C_Q:C_Q + ATT_DIM] * Q_SCALE
        knew[...] = hq[:, C_K:C_K + KV_DIM]
        vnew[...] = hq[:, C_V:C_V + KV_DIM]
        for hd in range(N_HEADS):
            h, g = divmod(hd, GQA_GROUP)
            slab = q[:, h * KV_DIM:(h + 1) * KV_DIM]
            moved = pltpu.roll(slab, ((h - g) % GQA_GROUP) * HEAD_DIM, axis=1)
            qe[hd] = jnp.where(lane_blk == h, moved, 0.0).astype(BF16)
        gate_a[...] = _silu(_proj(xb, win_ref, C_GA, ATT_DIM))

        bg = _proj(xb, win_ref, C_B, CONV_DIM)
        u = _proj(xb, win_ref, C_C, CONV_DIM) * _proj(xb, win_ref, C_H, CONV_DIM)
        r8 = lax.broadcasted_iota(jnp.int32, (n_rows, CONV_DIM), 0) & (SAMPLE_ROWS - 1)
        is_state = (r8 >= TOK0 - 2) & (r8 < TOK0)
        u = jnp.where(is_state, st_ref[...], u)
        u_ref[...] = u
        cy = (cw_ref[0:1, :] * pltpu.roll(u, 2, axis=0) + cw_ref[1:2, :] * pltpu.roll(u, 1, axis=0)
              + cw_ref[2:3, :] * u)
        gate_c = _silu(_proj(xb, win_ref, C_GC, CONV_DIM))
        mix_ref[:, ATT_DIM:] = ((bg * cy) * gate_c).astype(BF16)

        qrow = lax.broadcasted_iota(jnp.int32, (n_q, 2 * WINDOW), 0)
        key = lax.broadcasted_iota(jnp.int32, (n_q, 2 * WINDOW), 1)
        q_tok = jnp.maximum((qrow & (SAMPLE_ROWS - 1)) - TOK0, 0)
        q_seq = (qrow >> LOG_SAMPLE_ROWS) & (SEQ_PER_GROUP - 1)
        new = key - WINDOW
        k_tok = (new & (SAMPLE_ROWS - 1)) - TOK0
        k_seq = new >> LOG_SAMPLE_ROWS
        cached = key < WINDOW
        dist = jnp.where(cached, WINDOW + q_tok - key, q_tok - k_tok)
        ok_new = (new >= 0) & (new < GROUP_ROWS) & (k_seq == q_seq) & (k_tok >= 0)
        visible = (dist >= 0) & (dist < WINDOW) & (cached | ok_new)
        slope = jnp.zeros((n_q, 2 * WINDOW), F32)
        sink = jnp.zeros((n_q, LANES), F32)
        srow = lax.broadcasted_iota(jnp.int32, (n_q, LANES), 0)
        for hd in range(N_HEADS):
            slope = jnp.where((qrow >> LOG_GROUP_ROWS) == hd, SLOPES[hd], slope)
            sink = jnp.where((srow >> LOG_GROUP_ROWS) == hd, sinks_ref[layer, hd], sink)
        bias_ref[...] = jnp.where(visible, -(slope * dist.astype(F32)), NEG_INF)
        sinkcol[...] = sink

    q_seq = ((lax.broadcasted_iota(jnp.int32, (n_q, KV_DIM), 0) >> LOG_SAMPLE_ROWS)
             & (SEQ_PER_GROUP - 1))
    newest = lax.broadcasted_iota(jnp.int32, (KV_DIM, WINDOW), 1) >= WINDOW - N_NEW
    blk16 = lax.broadcasted_iota(jnp.int32, (GROUP_ROWS, KV_DIM), 1) >> LOG_HEAD_DIM
    pad_rows = jnp.zeros((WINDOW - GROUP_ROWS, KV_DIM), F32)
    sink = sinkcol[:, 0:1]

    probs, v_new_bs, row0s = [], [], []
    for gi in range(GROUPS_PER_STEP):
        g0 = pl.multiple_of((grp * GROUPS_PER_STEP + gi) * GROUP_ROWS, GROUP_ROWS)
        row0s.append(g0)
        w_g = jnp.concatenate([qe[hd, pl.ds(g0, GROUP_ROWS), :] for hd in range(N_HEADS)], axis=0)
        k_new = jnp.concatenate([knew[pl.ds(g0, GROUP_ROWS), :], pad_rows], axis=0)
        v_new = jnp.concatenate([vnew[pl.ds(g0, GROUP_ROWS), :], pad_rows], axis=0)
        k_new_b = k_new.astype(BF16)
        v_new_bs.append(v_new.astype(BF16))
        k_new_t = k_new.T
        v_new_t = v_new.T
        for s in range(SEQ_PER_GROUP):
            n = gi * SEQ_PER_GROUP + s
            ck_t = ck_ref[n]
            sc = jnp.concatenate(
                [jnp.dot(w_g, ck_t.astype(BF16), preferred_element_type=F32),
                 lax.dot_general(w_g, k_new_b, _NT, preferred_element_type=F32)], axis=1)
            probs.append(_softmax_rows(sc + bias_ref[...], sink).astype(BF16))
            to_tail = WINDOW - N_NEW - (s * SAMPLE_ROWS + TOK0)
            for cache_t, new_t, out_ref in ((ck_t, k_new_t, kb_ref), (cv_ref[n], v_new_t, vb_ref)):
                out_ref[n] = jnp.where(newest, pltpu.roll(new_t, to_tail, axis=1),
                                       pltpu.roll(cache_t, WINDOW - N_NEW, axis=1))

    for gi in range(GROUPS_PER_STEP):
        o_grp = jnp.zeros((n_q, KV_DIM), F32)
        for s in range(SEQ_PER_GROUP):
            n = gi * SEQ_PER_GROUP + s
            p = probs[n]
            o = (lax.dot_general(p[:, :WINDOW], cv_ref[n].astype(BF16), _NT,
                                 preferred_element_type=F32)
                 + jnp.dot(p[:, WINDOW:], v_new_bs[gi], preferred_element_type=F32))
            o_grp = jnp.where(q_seq == s, o, o_grp)
        for h in range(N_KV_HEADS):
            slab = jnp.zeros((GROUP_ROWS, KV_DIM), F32)
            for g in range(GQA_GROUP):
                hd = h * GQA_GROUP + g
                piece = jnp.where(blk16 == h, o_grp[hd * GROUP_ROWS:(hd + 1) * GROUP_ROWS, :], 0.0)
                slab = slab + pltpu.roll(piece, ((g - h) % GQA_GROUP) * HEAD_DIM, axis=1)
            rows = pl.ds(row0s[gi], GROUP_ROWS)
            cols = slice(h * KV_DIM, (h + 1) * KV_DIM)
            mix_ref[rows, cols] = (slab * gate_a[rows, cols]).astype(BF16)

    @pl.when(grp == n_steps - 1)
    def _():
        out = jnp.dot(mix_ref[...], wout_ref[...], preferred_element_type=F32)
        y_ref[...] = _layer_norm(alpha * x_ref[...] + out, g_ref[...], b_ref[...])


def _sample_layer(layer, x8, st_all, ck_all, cv_all, kb_all, vb_all, w_in, w_out, conv_w, sinks,
                  ln_g, ln_b, alpha):
    n_rows, d = x8.shape
    depth, n_seq = ck_all.shape[:2]
    this_layer = lambda g: (layer, 0, 0)
    n_steps = n_seq // SEQ_PER_STEP
    n_q = N_HEADS * GROUP_ROWS
    const2 = lambda g: (0, 0)
    cache_spec = pl.BlockSpec((None, SEQ_PER_STEP, KV_DIM, WINDOW), lambda g: (layer, g, 0, 0))
    kernel = functools.partial(_sample_kernel, layer=layer, n_rows=n_rows, alpha=alpha)
    operands = [sinks, x8, w_in, w_out, conv_w, ln_g, ln_b, ck_all, cv_all, st_all, kb_all, vb_all]
    in_specs = [
        pl.BlockSpec(memory_space=pltpu.SMEM),
        _resident((n_rows, d), const2),
        _resident((d, PROJ_DIM), const2),
        _resident((ATT_DIM + CONV_DIM, d), const2),
        pl.BlockSpec((None, 3, CONV_DIM), this_layer),
        pl.BlockSpec((None, 1, d), this_layer),
        pl.BlockSpec((None, 1, d), this_layer),
        cache_spec,
        cache_spec,
        _resident((None, n_rows, CONV_DIM), this_layer),
        pl.BlockSpec(memory_space=pl.ANY),
        pl.BlockSpec(memory_space=pl.ANY),
    ]
    aliases = {len(operands) - 2: 1, len(operands) - 1: 2}
    return pl.pallas_call(
        kernel,
        grid=(n_steps,),
        in_specs=in_specs,
        out_specs=[
            pl.BlockSpec((n_rows, d), const2),
            cache_spec,
            cache_spec,
            pl.BlockSpec((n_rows, CONV_DIM), const2),
        ],
        out_shape=[
            jax.ShapeDtypeStruct((n_rows, d), F32),
            jax.ShapeDtypeStruct(ck_all.shape, F32),
            jax.ShapeDtypeStruct(cv_all.shape, F32),
            jax.ShapeDtypeStruct((n_rows, CONV_DIM), F32),
        ],
        input_output_aliases=aliases,
        scratch_shapes=[
            pltpu.VMEM((N_HEADS, n_rows, KV_DIM), BF16),
            pltpu.VMEM((n_rows, KV_DIM), F32),
            pltpu.VMEM((n_rows, KV_DIM), F32),
            pltpu.VMEM((n_rows, ATT_DIM), F32),
            pltpu.VMEM((n_q, 2 * WINDOW), F32),
            pltpu.VMEM((n_q, LANES), F32),
            pltpu.VMEM((n_rows, ATT_DIM + CONV_DIM), BF16),
        ],
        compiler_params=pltpu.CompilerParams(
            dimension_semantics=("arbitrary",),
            vmem_limit_bytes=VMEM_LIMIT,
        ),
        name="sample_layer",
    )(*operands)


def kernel(x_prompt, x_sample, cache_k, cache_v, state_conv, meta_tokens,
           w_in, conv_w, sinks, w_out, ln_g, ln_b):
    depth = w_in.shape[0]
    alpha = float((2 * depth) ** 0.25)
    batch, seq, d = x_prompt.shape
    n_seq, n_tok = x_sample.shape[:2]
    assert d == D_MODEL and seq % PROMPT_TILE == 0 and n_tok == SAMPLE_ROWS - TOK0
    assert meta_tokens.shape[0] == N_META and n_seq % SEQ_PER_STEP == 0
    assert cache_k.shape[2] == WINDOW and state_conv.shape[2] == 2

    w_in_b = w_in[0].astype(BF16)
    w_out_b = w_out[0].astype(BF16)
    ln_g3 = ln_g.reshape(depth, 1, d)
    ln_b3 = ln_b.reshape(depth, 1, d)
    xp = x_prompt
    xh = meta_tokens.astype(F32)
    xs = jnp.pad(x_sample, ((0, 0), (TOK0, 0), (0, 0))).reshape(n_seq * SAMPLE_ROWS, d)
    ck_all = jnp.transpose(cache_k, (0, 1, 3, 4, 2)).reshape(depth, n_seq, KV_DIM, WINDOW)
    cv_all = jnp.transpose(cache_v, (0, 1, 3, 4, 2)).reshape(depth, n_seq, KV_DIM, WINDOW)
    st_all = jnp.pad(state_conv, ((0, 0), (0, 0), (TOK0 - 2, SAMPLE_ROWS - TOK0), (0, 0)))
    st_all = st_all.reshape(depth, n_seq * SAMPLE_ROWS, CONV_DIM)

    kp, vp, cp, cs = [], [], [], []
    kb_all = jnp.zeros(ck_all.shape, F32)
    vb_all = jnp.zeros(cv_all.shape, F32)
    for l in range(depth):
        xp, xh, k_last, v_last, c_last, *next_weights = _prompt_layer(
            l, xp, xh, w_in_b, w_out_b, w_in, w_out, conv_w, sinks, ln_g3, ln_b3, alpha)
        kp.append(k_last.reshape(batch, WINDOW, N_KV_HEADS, HEAD_DIM))
        vp.append(v_last.reshape(batch, WINDOW, N_KV_HEADS, HEAD_DIM))
        cp.append(c_last[:, 6:8, :])
        xs, kb_all, vb_all, u8 = _sample_layer(
            l, xs, st_all, ck_all, cv_all, kb_all, vb_all, w_in_b, w_out_b, conv_w, sinks, ln_g3,
            ln_b3, alpha)
        if next_weights:
            w_in_b, w_out_b = next_weights
        cs.append(u8.reshape(n_seq, SAMPLE_ROWS, CONV_DIM)[:, SAMPLE_ROWS - 2:, :])
    y_sample = xs.reshape(n_seq, SAMPLE_ROWS, d)[:, TOK0:, :]
    kv_shape = (depth, n_seq, N_KV_HEADS, HEAD_DIM, WINDOW)
    k_sample = jnp.transpose(kb_all.reshape(kv_shape), (0, 1, 4, 2, 3))
    v_sample = jnp.transpose(vb_all.reshape(kv_shape), (0, 1, 4, 2, 3))
    return (xp, y_sample, jnp.stack(kp), jnp.stack(vp), jnp.stack(cp),
            k_sample, v_sample, jnp.stack(cs))
```

```python
import functools

import numpy as np
import jax
import jax.numpy as jnp
from jax import lax
from jax.experimental import pallas as pl
from jax.experimental.pallas import tpu as pltpu

F32 = jnp.float32
BF16 = jnp.bfloat16

D_MODEL = 2048
N_META = 16
ATT_DIM = 1024
CONV_DIM = 1024
HEAD_DIM = 64
N_HEADS = 16
N_KV_HEADS = 4
GQA_GROUP = N_HEADS // N_KV_HEADS
KV_DIM = N_KV_HEADS * HEAD_DIM
WINDOW = 128
PROJ_DIM = 2 * ATT_DIM + 2 * KV_DIM + 4 * CONV_DIM
LN_EPS = 1e-5
NEG_INF = -1e30
Q_SCALE = HEAD_DIM ** -0.5

C_Q = 0
C_K = ATT_DIM
C_V = C_K + KV_DIM
C_GA = C_V + KV_DIM
C_B = C_GA + ATT_DIM
C_C = C_B + CONV_DIM
C_H = C_C + CONV_DIM
C_GC = C_H + CONV_DIM
COL_GROUPS = (C_Q, C_GA, C_B, C_C, C_H, C_GC, PROJ_DIM)

LANES = 128
HEAD_PAD = WINDOW - N_META
PROMPT_TILE = 256
FILL_CHUNK = 256
OUT_CHUNK = 512
SAMPLE_ROWS = 8
VMEM_LIMIT = 58 * 1024 * 1024

SLOPES = [float(np.float32(2.0 ** (-8.0 * (h + 1) / N_HEADS))) for h in range(N_HEADS)]

_NT = (((1,), (1,)), ((), ()))


def _silu(g):
    return g * (1.0 / (1.0 + jnp.exp(-g)))


def _softmax_rows(s, sink):
    m = jnp.maximum(jnp.max(s, axis=1, keepdims=True), sink)
    p = jnp.exp(s - m)
    denom = jnp.sum(p, axis=1, keepdims=True) + jnp.exp(sink - m)
    return p * (1.0 / denom)


def _softmax_cols(s, sink):
    m = jnp.maximum(jnp.max(s, axis=0, keepdims=True), sink)
    p = jnp.exp(s - m)
    denom = jnp.sum(p, axis=0, keepdims=True) + jnp.exp(sink - m)
    return p * (1.0 / denom)


def _layer_norm(z, g, b):
    mu = jnp.mean(z, axis=1, keepdims=True)
    zc = z - mu
    var = jnp.mean(zc * zc, axis=1, keepdims=True)
    return zc * lax.rsqrt(var + LN_EPS) * g + b


def _proj(xb, w_ref, c0, width):
    return jnp.dot(xb, w_ref[:, c0:c0 + width], preferred_element_type=F32)


def _store_k_variants(src, dst, r0, rows):
    low = lax.broadcasted_iota(jnp.int32, (rows, LANES), 1) < HEAD_DIM
    for cc in range(KV_DIM // LANES):
        col = src[:, cc * LANES:(cc + 1) * LANES]
        swapped = pltpu.roll(col, HEAD_DIM, axis=1)
        h_even, h_odd = 2 * cc, 2 * cc + 1
        dst[2 * h_even + 0, r0:r0 + rows, :] = jnp.where(low, col, 0.0).astype(BF16)
        dst[2 * h_even + 1, r0:r0 + rows, :] = jnp.where(low, 0.0, swapped).astype(BF16)
        dst[2 * h_odd + 0, r0:r0 + rows, :] = jnp.where(low, swapped, 0.0).astype(BF16)
        dst[2 * h_odd + 1, r0:r0 + rows, :] = jnp.where(low, 0.0, col).astype(BF16)


def _conv_chunk_pieces(xb, win_ref, cw_ref, ucar, mix_ref, rows, lo, width):
    cols = slice(lo, lo + width)
    got = {}

    def project(name, c0):
        got[name] = _proj(xb, win_ref, c0 + lo, width)

    def finish():
        u = got["c"] * got["h"]
        row = lax.broadcasted_iota(jnp.int32, (rows, width), 0)
        prev1 = ucar[7:8, cols]
        prev2 = ucar[6:7, cols]
        u1 = jnp.where(row == 0, prev1, pltpu.roll(u, 1, axis=0))
        u2 = jnp.where(row == 0, prev2, jnp.where(row == 1, prev1, pltpu.roll(u, 2, axis=0)))
        cy = cw_ref[0:1, cols] * u2 + cw_ref[1:2, cols] * u1 + cw_ref[2:3, cols] * u
        ucar[:, cols] = u[rows - 8:rows, :]
        gate_c = _silu(_proj(xb, win_ref, C_GC + lo, width))
        mix_ref[0:rows, ATT_DIM + lo:ATT_DIM + lo + width] = (
            (got["b"] * cy) * gate_c).astype(BF16)

    return [functools.partial(project, "b", C_B), functools.partial(project, "c", C_C),
            functools.partial(project, "h", C_H), finish]


def _store_v_transposed(src, dst, c0, rows):
    vt = src.T
    for h in range(N_KV_HEADS):
        dst[h, :, c0:c0 + rows] = vt[h * HEAD_DIM:(h + 1) * HEAD_DIM, :].astype(BF16)


def _scores_group(layer, qb, kmask, sinks_ref, kvar, bias_ref, p_scr, r0, h):
    q4 = jnp.concatenate(
        [qb[r0:r0 + WINDOW, c * LANES:(c + 1) * LANES] for c in (2 * h, 2 * h + 1)], axis=0)
    for par in range(2):
        keys = kvar[2 * h + par, r0:r0 + 2 * WINDOW, :]
        s2 = lax.dot_general(keys, q4, _NT, preferred_element_type=F32)
        for half in range(2):
            hd = GQA_GROUP * h + 2 * half + par
            lanes = slice(half * LANES, (half + 1) * LANES)
            s = s2[:, lanes] + SLOPES[hd] * bias_ref[...]
            if kmask is not None:
                s = s + kmask
            p_scr[2 * h + par, :, lanes] = _softmax_cols(s, sinks_ref[layer, hd]).astype(BF16)


def _values_group(gate_ref, vtvar, p_scr, mix_ref, r0, h):
    vals_t = vtvar[h, :, r0:r0 + 2 * WINDOW]
    o_t = jnp.concatenate([jnp.dot(vals_t, p_scr[2 * h + par], preferred_element_type=F32)
                           for par in range(2)], axis=0)
    for half in range(2):
        lanes = slice((2 * h + half) * LANES, (2 * h + half + 1) * LANES)
        o = o_t[:, half * LANES:(half + 1) * LANES].T
        mix_ref[r0:r0 + WINDOW, lanes] = (o * gate_ref[r0:r0 + WINDOW, lanes]).astype(BF16)


def _gate_chunk(xb, win_ref, gate_ref, rows, lo, width):
    gate_ref[0:rows, lo:lo + width] = _silu(_proj(xb, win_ref, C_GA + lo, width))


def _prompt_rows(x, rows, kmask0, after_qkv, layer, sinks_ref, win_ref, wout_ref, cw_ref,
                 kvar, vtvar, ucar, bias_ref, mix_ref, p_scr, gate_ref, z_ref, alpha):
    xb = x.astype(BF16)
    hq = _proj(xb, win_ref, 0, C_GA)
    after_qkv()
    qb = (hq[:, C_Q:C_Q + ATT_DIM] * Q_SCALE).astype(BF16)
    kf = hq[:, C_K:C_K + KV_DIM]
    vf = hq[:, C_V:C_V + KV_DIM]
    _store_k_variants(kf, kvar, WINDOW, rows)
    _store_v_transposed(vf, vtvar, WINDOW, rows)

    n_blocks = rows // WINDOW
    gates = [functools.partial(_gate_chunk, xb, win_ref, gate_ref, rows, lo, FILL_CHUNK)
             for lo in range(0, ATT_DIM, FILL_CHUNK)]
    convs = [_conv_chunk_pieces(xb, win_ref, cw_ref, ucar, mix_ref, rows, lo, FILL_CHUNK)
             for lo in range(0, CONV_DIM, FILL_CHUNK)]
    fillers = gates[:2] + convs[0] + gates[2:] + [p for conv in convs[1:] for p in conv]
    n_units = n_blocks * N_KV_HEADS
    cuts = [len(fillers) * u // n_units for u in range(n_units + 1)]
    assert cuts[N_KV_HEADS] >= len(gates) + len(convs[0])
    for blk in range(n_blocks):
        r0 = blk * WINDOW
        for h in range(N_KV_HEADS):
            unit = blk * N_KV_HEADS + h
            for filler in fillers[cuts[unit]:cuts[unit + 1]]:
                filler()
            _scores_group(layer, qb, kmask0 if blk == 0 else None, sinks_ref, kvar, bias_ref,
                          p_scr, r0, h)
        for h in range(N_KV_HEADS):
            _values_group(gate_ref, vtvar, p_scr, mix_ref, r0, h)
        if blk == 0:
            for i in range(2 * N_KV_HEADS):
                kvar[i, 0:WINDOW, :] = kvar[i, rows:rows + WINDOW, :]
            for i in range(N_KV_HEADS):
                vtvar[i, :, 0:WINDOW] = vtvar[i, :, rows:rows + WINDOW]

    mix = mix_ref[0:rows, :]
    for c0 in range(0, D_MODEL, OUT_CHUNK):
        cols = slice(c0, c0 + OUT_CHUNK)
        out = jnp.dot(mix, wout_ref[:, cols], preferred_element_type=F32)
        z_ref[0:rows, cols] = alpha * x[:, cols] + out
    return kf, vf


def _cast_next_weights(t, last, next_layer, win_f32, wout_f32, win_next, wout_next,
                       stage_in, stage_out, sem):
    rows = stage_in[0].shape[0]
    srcs = (win_f32, wout_f32)
    dsts = (win_next, wout_next)

    def read(k, i):
        return pltpu.make_async_copy(srcs[i].at[next_layer, pl.ds(k * rows, rows), :],
                                     stage_in[i], sem.at[i])

    def write(k, i):
        return pltpu.make_async_copy(stage_out[i], dsts[i].at[pl.ds(k * rows, rows), :],
                                     sem.at[2 + i])

    @pl.when(t >= 1)
    def _():
        for i in range(2):
            read(t - 1, i).wait()

        @pl.when(t >= 2)
        def _():
            for i in range(2):
                write(t - 2, i).wait()

        for i in range(2):
            stage_out[i][...] = stage_in[i][...].astype(BF16)
            write(t - 1, i).start()

    @pl.when(t < last)
    def _():
        for i in range(2):
            read(t, i).start()

    @pl.when(t == last)
    def _():
        for i in range(2):
            write(t - 1, i).wait()


def _prompt_kernel_casting(sinks_ref, x_ref, xh_ref, win_ref, wout_ref, cw_ref, g_ref, b_ref,
                           win_f32, wout_f32,
                           y_ref, yh_ref, kl_ref, vl_ref, cs_ref, win_next, wout_next,
                           *scratch, layer, **static):
    cast_in_a, cast_in_b, cast_out_a, cast_out_b, cast_sem = scratch[-5:]
    _cast_next_weights(pl.program_id(0), pl.num_programs(0) - 1, layer + 1, win_f32, wout_f32,
                       win_next, wout_next, (cast_in_a, cast_in_b), (cast_out_a, cast_out_b),
                       cast_sem)
    _prompt_kernel(sinks_ref, x_ref, xh_ref, win_ref, wout_ref, cw_ref, g_ref, b_ref,
                   y_ref, yh_ref, kl_ref, vl_ref, cs_ref, *scratch[:-5], layer=layer, **static)


def _prompt_kernel(sinks_ref, x_ref, xh_ref, win_ref, wout_ref, cw_ref, g_ref, b_ref,
                   y_ref, yh_ref, kl_ref, vl_ref, cs_ref,
                   kvar, vtvar, khead, vhead, ucar, uhead, bias_ref, mix_ref, p_scr, gate_ref,
                   z_scr, *, layer, tm, n_tiles, alpha):
    t = pl.program_id(0)
    last = pl.num_programs(0) - 1
    j = t % n_tiles
    shared = (layer, sinks_ref, win_ref, wout_ref, cw_ref, kvar, vtvar, ucar, bias_ref, mix_ref,
              p_scr, gate_ref, z_scr, alpha)
    key_row = lax.broadcasted_iota(jnp.int32, (2 * WINDOW, WINDOW), 0)

    @pl.when(t == 0)
    def _():
        qi = lax.broadcasted_iota(jnp.int32, (2 * WINDOW, WINDOW), 1)
        dist = WINDOW + qi - key_row
        visible = (dist >= 0) & (dist < WINDOW)
        bias_ref[...] = jnp.where(visible, -dist.astype(F32), NEG_INF)
        z_scr[...] = jnp.zeros(z_scr.shape, F32)
        kvar[:, 0:WINDOW, :] = jnp.zeros((2 * N_KV_HEADS, WINDOW, LANES), BF16)
        vtvar[:, :, 0:WINDOW] = jnp.zeros((N_KV_HEADS, HEAD_DIM, WINDOW), BF16)
        ucar[...] = jnp.zeros(ucar.shape, F32)
        kmask = jnp.where(key_row < WINDOW + HEAD_PAD, NEG_INF, 0.0)
        xh = jnp.concatenate([jnp.zeros((HEAD_PAD, D_MODEL), F32), xh_ref[...]], axis=0)
        _prompt_rows(xh, WINDOW, kmask, lambda: None, *shared)
        yh_ref[...] = _layer_norm(z_scr[0:WINDOW, :], g_ref[...], b_ref[...])[HEAD_PAD:, :]
        khead[...] = kvar[:, 0:WINDOW, :]
        vhead[...] = vtvar[:, :, 0:WINDOW]
        uhead[...] = ucar[...]

    def norm_previous_tile():
        y_ref[...] = _layer_norm(z_scr[...], g_ref[...], b_ref[...])

    @pl.when(t < last)
    def _():
        @pl.when(j == 0)
        def _():
            kvar[:, 0:WINDOW, :] = khead[...]
            vtvar[:, :, 0:WINDOW] = vhead[...]
            ucar[...] = uhead[...]

        kmask = jnp.where(key_row < HEAD_PAD, jnp.where(j == 0, NEG_INF, 0.0), 0.0)
        kf, vf = _prompt_rows(x_ref[...], tm, kmask, norm_previous_tile, *shared)

        @pl.when(j == n_tiles - 1)
        def _():
            kl_ref[...] = kf[tm - WINDOW:tm, :]
            vl_ref[...] = vf[tm - WINDOW:tm, :]
            cs_ref[...] = ucar[...]

    @pl.when(t == last)
    def _():
        norm_previous_tile()


def _resident(shape, index_map):
    return pl.BlockSpec(shape, index_map, pipeline_mode=pl.Buffered(1))


def _prompt_layer(layer, x, xh, w_in_b, w_out_b, w_in, w_out, conv_w, sinks, ln_g, ln_b, alpha):
    batch, seq, d = x.shape
    depth = w_in.shape[0]
    tm = PROMPT_TILE
    n_tiles = seq // tm
    total = batch * n_tiles
    const2 = lambda t: (0, 0)
    this_layer = lambda t: (layer, 0, 0)

    def tile_block(t):
        t = jnp.minimum(t, total - 1)
        return (t // n_tiles, t % n_tiles, 0)

    def prev_tile_block(t):
        return tile_block(jnp.maximum(t - 1, 0))

    per_batch = lambda t: (jnp.minimum(t, total - 1) // n_tiles, 0, 0)
    static = dict(layer=layer, tm=tm, n_tiles=n_tiles, alpha=alpha)
    operands = [sinks, x, xh, w_in_b, w_out_b, conv_w, ln_g, ln_b]
    in_specs = [
        pl.BlockSpec(memory_space=pltpu.SMEM),
        pl.BlockSpec((None, tm, d), tile_block),
        _resident((N_META, d), const2),
        _resident((d, PROJ_DIM), const2),
        _resident((ATT_DIM + CONV_DIM, d), const2),
        pl.BlockSpec((None, 3, CONV_DIM), this_layer),
        pl.BlockSpec((None, 1, d), this_layer),
        pl.BlockSpec((None, 1, d), this_layer),
    ]
    out_specs = [
        pl.BlockSpec((None, tm, d), prev_tile_block),
        pl.BlockSpec((N_META, d), const2),
        pl.BlockSpec((None, WINDOW, KV_DIM), per_batch),
        pl.BlockSpec((None, WINDOW, KV_DIM), per_batch),
        pl.BlockSpec((None, 8, CONV_DIM), per_batch),
    ]
    out_shape = [
        jax.ShapeDtypeStruct((batch, seq, d), F32),
        jax.ShapeDtypeStruct((N_META, d), F32),
        jax.ShapeDtypeStruct((batch, WINDOW, KV_DIM), F32),
        jax.ShapeDtypeStruct((batch, WINDOW, KV_DIM), F32),
        jax.ShapeDtypeStruct((batch, 8, CONV_DIM), F32),
    ]
    scratch_shapes = [
        pltpu.VMEM((2 * N_KV_HEADS, WINDOW + tm, LANES), BF16),
        pltpu.VMEM((N_KV_HEADS, HEAD_DIM, WINDOW + tm), BF16),
        pltpu.VMEM((2 * N_KV_HEADS, WINDOW, LANES), BF16),
        pltpu.VMEM((N_KV_HEADS, HEAD_DIM, WINDOW), BF16),
        pltpu.VMEM((8, CONV_DIM), F32),
        pltpu.VMEM((8, CONV_DIM), F32),
        pltpu.VMEM((2 * WINDOW, WINDOW), F32),
        pltpu.VMEM((tm, ATT_DIM + CONV_DIM), BF16),
        pltpu.VMEM((2 * N_KV_HEADS, 2 * WINDOW, 2 * WINDOW), BF16),
        pltpu.VMEM((tm, ATT_DIM), F32),
        pltpu.VMEM((tm, d), F32),
    ]
    kernel = _prompt_kernel
    if layer + 1 < depth:
        cast_rows = d // total
        assert cast_rows * total == d and cast_rows % 16 == 0
        kernel = _prompt_kernel_casting
        operands += [w_in, w_out]
        in_specs += [pl.BlockSpec(memory_space=pl.ANY)] * 2
        out_specs += [pl.BlockSpec(memory_space=pl.ANY)] * 2
        out_shape += [jax.ShapeDtypeStruct(w_in_b.shape, BF16),
                      jax.ShapeDtypeStruct(w_out_b.shape, BF16)]
        scratch_shapes += [
            pltpu.VMEM((cast_rows, PROJ_DIM), F32),
            pltpu.VMEM((cast_rows, d), F32),
            pltpu.VMEM((cast_rows, PROJ_DIM), BF16),
            pltpu.VMEM((cast_rows, d), BF16),
            pltpu.SemaphoreType.DMA((4,)),
        ]
    return pl.pallas_call(
        functools.partial(kernel, **static),
        grid=(total + 1,),
        in_specs=in_specs,
        out_specs=out_specs,
        out_shape=out_shape,
        scratch_shapes=scratch_shapes,
        compiler_params=pltpu.CompilerParams(
            dimension_semantics=("arbitrary",),
            vmem_limit_bytes=VMEM_LIMIT,
        ),
        name="prompt_layer",
    )(*operands)


SEQ_PER_GROUP = 2
GROUP_ROWS = SEQ_PER_GROUP * SAMPLE_ROWS
GROUPS_PER_STEP = 2
SEQ_PER_STEP = SEQ_PER_GROUP * GROUPS_PER_STEP
N_NEW = 4
TOK0 = SAMPLE_ROWS - N_NEW
LOG_SAMPLE_ROWS = 3
LOG_GROUP_ROWS = 4
LOG_HEAD_DIM = 6
assert (1 << LOG_SAMPLE_ROWS, 1 << LOG_GROUP_ROWS, 1 << LOG_HEAD_DIM) == (
    SAMPLE_ROWS, GROUP_ROWS, HEAD_DIM)


def _sample_kernel(sinks_ref, x_ref, win_hbm, wout_hbm, cw_ref, g_ref, b_ref, ck_ref, cv_ref,
                   st_ref, kb_prev, vb_prev, y_ref, kb_ref, vb_ref, u_ref,
                   win_ref, wout_ref, wsem,
                   qe, knew, vnew, gate_a, bias_ref, sinkcol, mix_ref, *, layer, n_rows, alpha):
    del kb_prev, vb_prev
    grp = pl.program_id(0)
    n_steps = pl.num_programs(0)
    n_q = N_HEADS * GROUP_ROWS
    lane_blk = lax.broadcasted_iota(jnp.int32, (n_rows, KV_DIM), 1) >> LOG_HEAD_DIM

    col_groups = COL_GROUPS

    def win_copy(i):
        cols = pl.ds(col_groups[i], col_groups[i + 1] - col_groups[i])
        return pltpu.make_async_copy(win_hbm.at[:, cols], win_ref.at[:, cols], wsem.at[i])

    def wout_copy():
        return pltpu.make_async_copy(wout_hbm, wout_ref, wsem.at[len(col_groups) - 1])

    def proj(xb, i, width):
        win_copy(i).wait()
        return _proj(xb, win_ref, col_groups[i], width)

    @pl.when(grp == 0)
    def _():
        for i in range(len(col_groups) - 1):
            win_copy(i).start()
        wout_copy().start()
        x = x_ref[...]
        xb = x.astype(BF16)
        hq = proj(xb, 0, C_GA)
        q = hq[:, C_Q:C_Q + ATT_DIM] * Q_SCALE
        knew[...] = hq[:, C_K:C_K + KV_DIM]
        vnew[...] = hq[:, C_V:C_V + KV_DIM]
        for hd in range(N_HEADS):
            h, g = divmod(hd, GQA_GROUP)
            slab = q[:, h * KV_DIM:(h + 1) * KV_DIM]
            moved = pltpu.roll(slab, ((h - g) % GQA_GROUP) * HEAD_DIM, axis=1)
            qe[hd] = jnp.where(lane_blk == h, moved, 0.0).astype(BF16)
        gate_a[...] = _silu(proj(xb, 1, ATT_DIM))

        bg = proj(xb, 2, CONV_DIM)
        u = proj(xb, 3, CONV_DIM) * proj(xb, 4, CONV_DIM)
        r8 = lax.broadcasted_iota(jnp.int32, (n_rows, CONV_DIM), 0) & (SAMPLE_ROWS - 1)
        is_state = (r8 >= TOK0 - 2) & (r8 < TOK0)
        u = jnp.where(is_state, st_ref[...], u)
        u_ref[...] = u
        cy = (cw_ref[0:1, :] * pltpu.roll(u, 2, axis=0) + cw_ref[1:2, :] * pltpu.roll(u, 1, axis=0)
              + cw_ref[2:3, :] * u)
        gate_c = _silu(proj(xb, 5, CONV_DIM))
        mix_ref[:, ATT_DIM:] = ((bg * cy) * gate_c).astype(BF16)

        qrow = lax.broadcasted_iota(jnp.int32, (n_q, 2 * WINDOW), 0)
        key = lax.broadcasted_iota(jnp.int32, (n_q, 2 * WINDOW), 1)
        q_tok = jnp.maximum((qrow & (SAMPLE_ROWS - 1)) - TOK0, 0)
        q_seq = (qrow >> LOG_SAMPLE_ROWS) & (SEQ_PER_GROUP - 1)
        new = key - WINDOW
        k_tok = (new & (SAMPLE_ROWS - 1)) - TOK0
        k_seq = new >> LOG_SAMPLE_ROWS
        cached = key < WINDOW
        dist = jnp.where(cached, WINDOW + q_tok - key, q_tok - k_tok)
        ok_new = (new >= 0) & (new < GROUP_ROWS) & (k_seq == q_seq) & (k_tok >= 0)
        visible = (dist >= 0) & (dist < WINDOW) & (cached | ok_new)
        slope = jnp.zeros((n_q, 2 * WINDOW), F32)
        sink = jnp.zeros((n_q, LANES), F32)
        srow = lax.broadcasted_iota(jnp.int32, (n_q, LANES), 0)
        for hd in range(N_HEADS):
            slope = jnp.where((qrow >> LOG_GROUP_ROWS) == hd, SLOPES[hd], slope)
            sink = jnp.where((srow >> LOG_GROUP_ROWS) == hd, sinks_ref[layer, hd], sink)
        bias_ref[...] = jnp.where(visible, -(slope * dist.astype(F32)), NEG_INF)
        sinkcol[...] = sink

    q_seq = ((lax.broadcasted_iota(jnp.int32, (n_q, KV_DIM), 0) >> LOG_SAMPLE_ROWS)
             & (SEQ_PER_GROUP - 1))
    newest = lax.broadcasted_iota(jnp.int32, (KV_DIM, WINDOW), 1) >= WINDOW - N_NEW
    blk16 = lax.broadcasted_iota(jnp.int32, (GROUP_ROWS, KV_DIM), 1) >> LOG_HEAD_DIM
    pad_rows = jnp.zeros((WINDOW - GROUP_ROWS, KV_DIM), F32)
    sink = sinkcol[:, 0:1]

    probs, v_new_bs, row0s = [], [], []
    for gi in range(GROUPS_PER_STEP):
        g0 = pl.multiple_of((grp * GROUPS_PER_STEP + gi) * GROUP_ROWS, GROUP_ROWS)
        row0s.append(g0)
        w_g = jnp.concatenate([qe[hd, pl.ds(g0, GROUP_ROWS), :] for hd in range(N_HEADS)], axis=0)
        k_new = jnp.concatenate([knew[pl.ds(g0, GROUP_ROWS), :], pad_rows], axis=0)
        v_new = jnp.concatenate([vnew[pl.ds(g0, GROUP_ROWS), :], pad_rows], axis=0)
        k_new_b = k_new.astype(BF16)
        v_new_bs.append(v_new.astype(BF16))
        k_new_t = k_new.T
        v_new_t = v_new.T
        for s in range(SEQ_PER_GROUP):
            n = gi * SEQ_PER_GROUP + s
            ck_t = ck_ref[n]
            sc = jnp.concatenate(
                [jnp.dot(w_g, ck_t.astype(BF16), preferred_element_type=F32),
                 lax.dot_general(w_g, k_new_b, _NT, preferred_element_type=F32)], axis=1)
            probs.append(_softmax_rows(sc + bias_ref[...], sink).astype(BF16))
            to_tail = WINDOW - N_NEW - (s * SAMPLE_ROWS + TOK0)
            for cache_t, new_t, out_ref in ((ck_t, k_new_t, kb_ref), (cv_ref[n], v_new_t, vb_ref)):
                out_ref[n] = jnp.where(newest, pltpu.roll(new_t, to_tail, axis=1),
                                       pltpu.roll(cache_t, WINDOW - N_NEW, axis=1))

    for gi in range(GROUPS_PER_STEP):
        o_grp = jnp.zeros((n_q, KV_DIM), F32)
        for s in range(SEQ_PER_GROUP):
            n = gi * SEQ_PER_GROUP + s
            p = probs[n]
            o = (lax.dot_general(p[:, :WINDOW], cv_ref[n].astype(BF16), _NT,
                                 preferred_element_type=F32)
                 + jnp.dot(p[:, WINDOW:], v_new_bs[gi], preferred_element_type=F32))
            o_grp = jnp.where(q_seq == s, o, o_grp)
        for h in range(N_KV_HEADS):
            slab = jnp.zeros((GROUP_ROWS, KV_DIM), F32)
            for g in range(GQA_GROUP):
                hd = h * GQA_GROUP + g
                piece = jnp.where(blk16 == h, o_grp[hd * GROUP_ROWS:(hd + 1) * GROUP_ROWS, :], 0.0)
                slab = slab + pltpu.roll(piece, ((g - h) % GQA_GROUP) * HEAD_DIM, axis=1)
            rows = pl.ds(row0s[gi], GROUP_ROWS)
            cols = slice(h * KV_DIM, (h + 1) * KV_DIM)
            mix_ref[rows, cols] = (slab * gate_a[rows, cols]).astype(BF16)

    @pl.when(grp == n_steps - 1)
    def _():
        wout_copy().wait()
        out = jnp.dot(mix_ref[...], wout_ref[...], preferred_element_type=F32)
        y_ref[...] = _layer_norm(alpha * x_ref[...] + out, g_ref[...], b_ref[...])


def _sample_layer(layer, x8, st_all, ck_all, cv_all, kb_all, vb_all, w_in, w_out, conv_w, sinks,
                  ln_g, ln_b, alpha):
    n_rows, d = x8.shape
    depth, n_seq = ck_all.shape[:2]
    this_layer = lambda g: (layer, 0, 0)
    n_steps = n_seq // SEQ_PER_STEP
    n_q = N_HEADS * GROUP_ROWS
    const2 = lambda g: (0, 0)
    cache_spec = pl.BlockSpec((None, SEQ_PER_STEP, KV_DIM, WINDOW), lambda g: (layer, g, 0, 0))
    kernel = functools.partial(_sample_kernel, layer=layer, n_rows=n_rows, alpha=alpha)
    operands = [sinks, x8, w_in, w_out, conv_w, ln_g, ln_b, ck_all, cv_all, st_all, kb_all, vb_all]
    in_specs = [
        pl.BlockSpec(memory_space=pltpu.SMEM),
        _resident((n_rows, d), const2),
        pl.BlockSpec(memory_space=pl.ANY),
        pl.BlockSpec(memory_space=pl.ANY),
        pl.BlockSpec((None, 3, CONV_DIM), this_layer),
        pl.BlockSpec((None, 1, d), this_layer),
        pl.BlockSpec((None, 1, d), this_layer),
        cache_spec,
        cache_spec,
        _resident((None, n_rows, CONV_DIM), this_layer),
        pl.BlockSpec(memory_space=pl.ANY),
        pl.BlockSpec(memory_space=pl.ANY),
    ]
    aliases = {len(operands) - 2: 1, len(operands) - 1: 2}
    return pl.pallas_call(
        kernel,
        grid=(n_steps,),
        in_specs=in_specs,
        out_specs=[
            pl.BlockSpec((n_rows, d), const2),
            cache_spec,
            cache_spec,
            pl.BlockSpec((n_rows, CONV_DIM), const2),
        ],
        out_shape=[
            jax.ShapeDtypeStruct((n_rows, d), F32),
            jax.ShapeDtypeStruct(ck_all.shape, F32),
            jax.ShapeDtypeStruct(cv_all.shape, F32),
            jax.ShapeDtypeStruct((n_rows, CONV_DIM), F32),
        ],
        input_output_aliases=aliases,
        scratch_shapes=[
            pltpu.VMEM((d, PROJ_DIM), BF16),
            pltpu.VMEM((ATT_DIM + CONV_DIM, d), BF16),
            pltpu.SemaphoreType.DMA((len(COL_GROUPS),)),
            pltpu.VMEM((N_HEADS, n_rows, KV_DIM), BF16),
            pltpu.VMEM((n_rows, KV_DIM), F32),
            pltpu.VMEM((n_rows, KV_DIM), F32),
            pltpu.VMEM((n_rows, ATT_DIM), F32),
            pltpu.VMEM((n_q, 2 * WINDOW), F32),
            pltpu.VMEM((n_q, LANES), F32),
            pltpu.VMEM((n_rows, ATT_DIM + CONV_DIM), BF16),
        ],
        compiler_params=pltpu.CompilerParams(
            dimension_semantics=("arbitrary",),
            vmem_limit_bytes=VMEM_LIMIT,
        ),
        name="sample_layer",
    )(*operands)


def kernel(x_prompt, x_sample, cache_k, cache_v, state_conv, meta_tokens,
           w_in, conv_w, sinks, w_out, ln_g, ln_b):
    depth = w_in.shape[0]
    alpha = float((2 * depth) ** 0.25)
    batch, seq, d = x_prompt.shape
    n_seq, n_tok = x_sample.shape[:2]
    assert d == D_MODEL and seq % PROMPT_TILE == 0 and n_tok == SAMPLE_ROWS - TOK0
    assert meta_tokens.shape[0] == N_META and n_seq % SEQ_PER_STEP == 0
    assert cache_k.shape[2] == WINDOW and state_conv.shape[2] == 2

    w_in_b = w_in[0].astype(BF16)
    w_out_b = w_out[0].astype(BF16)
    ln_g3 = ln_g.reshape(depth, 1, d)
    ln_b3 = ln_b.reshape(depth, 1, d)
    xp = x_prompt
    xh = meta_tokens.astype(F32)
    xs = jnp.pad(x_sample, ((0, 0), (TOK0, 0), (0, 0))).reshape(n_seq * SAMPLE_ROWS, d)
    ck_all = jnp.transpose(cache_k, (0, 1, 3, 4, 2)).reshape(depth, n_seq, KV_DIM, WINDOW)
    cv_all = jnp.transpose(cache_v, (0, 1, 3, 4, 2)).reshape(depth, n_seq, KV_DIM, WINDOW)
    st_all = jnp.pad(state_conv, ((0, 0), (0, 0), (TOK0 - 2, SAMPLE_ROWS - TOK0), (0, 0)))
    st_all = st_all.reshape(depth, n_seq * SAMPLE_ROWS, CONV_DIM)

    kp, vp, cp, cs = [], [], [], []
    kb_all = jnp.zeros(ck_all.shape, F32)
    vb_all = jnp.zeros(cv_all.shape, F32)
    for l in range(depth):
        xp, xh, k_last, v_last, c_last, *next_weights = _prompt_layer(
            l, xp, xh, w_in_b, w_out_b, w_in, w_out, conv_w, sinks, ln_g3, ln_b3, alpha)
        kp.append(k_last.reshape(batch, WINDOW, N_KV_HEADS, HEAD_DIM))
        vp.append(v_last.reshape(batch, WINDOW, N_KV_HEADS, HEAD_DIM))
        cp.append(c_last[:, 6:8, :])
        xs, kb_all, vb_all, u8 = _sample_layer(
            l, xs, st_all, ck_all, cv_all, kb_all, vb_all, w_in_b, w_out_b, conv_w, sinks, ln_g3,
            ln_b3, alpha)
        if next_weights:
            w_in_b, w_out_b = next_weights
        cs.append(u8.reshape(n_seq, SAMPLE_ROWS, CONV_DIM)[:, SAMPLE_ROWS - 2:, :])
    y_sample = xs.reshape(n_seq, SAMPLE_ROWS, d)[:, TOK0:, :]
    kv_shape = (depth, n_seq, N_KV_HEADS, HEAD_DIM, WINDOW)
    k_sample = jnp.transpose(kb_all.reshape(kv_shape), (0, 1, 4, 2, 3))
    v_sample = jnp.transpose(vb_all.reshape(kv_shape), (0, 1, 4, 2, 3))
    return (xp, y_sample, jnp.stack(kp), jnp.stack(vp), jnp.stack(cp),
            k_sample, v_sample, jnp.stack(cs))
```

```python
import functools

import numpy as np
import jax
import jax.numpy as jnp
from jax import lax
from jax.experimental import pallas as pl
from jax.experimental.pallas import tpu as pltpu

F32 = jnp.float32
BF16 = jnp.bfloat16

D_MODEL = 2048
N_META = 16
ATT_DIM = 1024
CONV_DIM = 1024
HEAD_DIM = 64
N_HEADS = 16
N_KV_HEADS = 4
GQA_GROUP = N_HEADS // N_KV_HEADS
KV_DIM = N_KV_HEADS * HEAD_DIM
WINDOW = 128
PROJ_DIM = 2 * ATT_DIM + 2 * KV_DIM + 4 * CONV_DIM
LN_EPS = 1e-5
NEG_INF = -1e30
Q_SCALE = HEAD_DIM ** -0.5

C_Q = 0
C_K = ATT_DIM
C_V = C_K + KV_DIM
C_GA = C_V + KV_DIM
C_B = C_GA + ATT_DIM
C_C = C_B + CONV_DIM
C_H = C_C + CONV_DIM
C_GC = C_H + CONV_DIM
COL_GROUPS = (C_Q, C_GA, C_B, C_C, C_H, C_GC, PROJ_DIM)

LANES = 128
HEAD_PAD = WINDOW - N_META
PROMPT_TILE = 256
FILL_CHUNK = 256
OUT_CHUNK = 512
SAMPLE_ROWS = 8
VMEM_LIMIT = 58 * 1024 * 1024

SLOPES = [float(np.float32(2.0 ** (-8.0 * (h + 1) / N_HEADS))) for h in range(N_HEADS)]

_NT = (((1,), (1,)), ((), ()))


def _silu(g):
    return g * (1.0 / (1.0 + jnp.exp(-g)))


def _softmax_rows(s, sink):
    m = jnp.maximum(jnp.max(s, axis=1, keepdims=True), sink)
    p = jnp.exp(s - m)
    denom = jnp.sum(p, axis=1, keepdims=True) + jnp.exp(sink - m)
    return p * (1.0 / denom)


def _softmax_cols(s, sink):
    m = jnp.maximum(jnp.max(s, axis=0, keepdims=True), sink)
    p = jnp.exp(s - m)
    denom = jnp.sum(p, axis=0, keepdims=True) + jnp.exp(sink - m)
    return p * (1.0 / denom)


def _layer_norm(z, g, b):
    mu = jnp.mean(z, axis=1, keepdims=True)
    zc = z - mu
    var = jnp.mean(zc * zc, axis=1, keepdims=True)
    return zc * lax.rsqrt(var + LN_EPS) * g + b


def _proj(xb, w_ref, c0, width):
    return jnp.dot(xb, w_ref[:, c0:c0 + width], preferred_element_type=F32)


def _store_k_variants(src, dst, r0, rows):
    low = lax.broadcasted_iota(jnp.int32, (rows, LANES), 1) < HEAD_DIM
    for cc in range(KV_DIM // LANES):
        col = src[:, cc * LANES:(cc + 1) * LANES]
        swapped = pltpu.roll(col, HEAD_DIM, axis=1)
        h_even, h_odd = 2 * cc, 2 * cc + 1
        dst[2 * h_even + 0, r0:r0 + rows, :] = jnp.where(low, col, 0.0).astype(BF16)
        dst[2 * h_even + 1, r0:r0 + rows, :] = jnp.where(low, 0.0, swapped).astype(BF16)
        dst[2 * h_odd + 0, r0:r0 + rows, :] = jnp.where(low, swapped, 0.0).astype(BF16)
        dst[2 * h_odd + 1, r0:r0 + rows, :] = jnp.where(low, 0.0, col).astype(BF16)


def _conv_chunk_pieces(xb, win_ref, cw_ref, ucar, mix_ref, rows, lo, width):
    cols = slice(lo, lo + width)
    got = {}

    def project(name, c0):
        got[name] = _proj(xb, win_ref, c0 + lo, width)

    def finish():
        u = got["c"] * got["h"]
        row = lax.broadcasted_iota(jnp.int32, (rows, width), 0)
        prev1 = ucar[7:8, cols]
        prev2 = ucar[6:7, cols]
        u1 = jnp.where(row == 0, prev1, pltpu.roll(u, 1, axis=0))
        u2 = jnp.where(row == 0, prev2, jnp.where(row == 1, prev1, pltpu.roll(u, 2, axis=0)))
        cy = cw_ref[0:1, cols] * u2 + cw_ref[1:2, cols] * u1 + cw_ref[2:3, cols] * u
        ucar[:, cols] = u[rows - 8:rows, :]
        gate_c = _silu(_proj(xb, win_ref, C_GC + lo, width))
        mix_ref[0:rows, ATT_DIM + lo:ATT_DIM + lo + width] = (
            (got["b"] * cy) * gate_c).astype(BF16)

    return [functools.partial(project, "b", C_B), functools.partial(project, "c", C_C),
            functools.partial(project, "h", C_H), finish]


def _store_v_transposed(src, dst, c0, rows):
    vt = src.T
    for h in range(N_KV_HEADS):
        dst[h, :, c0:c0 + rows] = vt[h * HEAD_DIM:(h + 1) * HEAD_DIM, :].astype(BF16)


def _scores_group(layer, qb, kmask, sinks_ref, kvar, bias_ref, p_scr, r0, h):
    q4 = jnp.concatenate(
        [qb[r0:r0 + WINDOW, c * LANES:(c + 1) * LANES] for c in (2 * h, 2 * h + 1)], axis=0)
    for par in range(2):
        keys = kvar[2 * h + par, r0:r0 + 2 * WINDOW, :]
        s2 = lax.dot_general(keys, q4, _NT, preferred_element_type=F32)
        for half in range(2):
            hd = GQA_GROUP * h + 2 * half + par
            lanes = slice(half * LANES, (half + 1) * LANES)
            s = s2[:, lanes] + SLOPES[hd] * bias_ref[...]
            if kmask is not None:
                s = s + kmask
            p_scr[2 * h + par, :, lanes] = _softmax_cols(s, sinks_ref[layer, hd]).astype(BF16)


def _values_group(gate_ref, vtvar, p_scr, mix_ref, r0, h):
    vals_t = vtvar[h, :, r0:r0 + 2 * WINDOW]
    o_t = jnp.concatenate([jnp.dot(vals_t, p_scr[2 * h + par], preferred_element_type=F32)
                           for par in range(2)], axis=0)
    for half in range(2):
        lanes = slice((2 * h + half) * LANES, (2 * h + half + 1) * LANES)
        o = o_t[:, half * LANES:(half + 1) * LANES].T
        mix_ref[r0:r0 + WINDOW, lanes] = (o * gate_ref[r0:r0 + WINDOW, lanes]).astype(BF16)


def _gate_chunk(xb, win_ref, gate_ref, rows, lo, width):
    gate_ref[0:rows, lo:lo + width] = _silu(_proj(xb, win_ref, C_GA + lo, width))


def _prompt_rows(x, rows, kmask0, after_qkv, layer, sinks_ref, win_ref, wout_ref, cw_ref,
                 kvar, vtvar, ucar, bias_ref, mix_ref, p_scr, gate_ref, z_ref, alpha):
    xb = x.astype(BF16)
    hq = _proj(xb, win_ref, 0, C_GA)
    after_qkv()
    qb = (hq[:, C_Q:C_Q + ATT_DIM] * Q_SCALE).astype(BF16)
    kf = hq[:, C_K:C_K + KV_DIM]
    vf = hq[:, C_V:C_V + KV_DIM]
    _store_k_variants(kf, kvar, WINDOW, rows)
    _store_v_transposed(vf, vtvar, WINDOW, rows)

    n_blocks = rows // WINDOW
    gates = [functools.partial(_gate_chunk, xb, win_ref, gate_ref, rows, lo, FILL_CHUNK)
             for lo in range(0, ATT_DIM, FILL_CHUNK)]
    convs = [_conv_chunk_pieces(xb, win_ref, cw_ref, ucar, mix_ref, rows, lo, FILL_CHUNK)
             for lo in range(0, CONV_DIM, FILL_CHUNK)]
    fillers = gates[:2] + convs[0] + gates[2:] + [p for conv in convs[1:] for p in conv]
    n_units = n_blocks * N_KV_HEADS
    cuts = [len(fillers) * u // n_units for u in range(n_units + 1)]
    assert cuts[N_KV_HEADS] >= len(gates) + len(convs[0])
    for blk in range(n_blocks):
        r0 = blk * WINDOW
        for h in range(N_KV_HEADS):
            unit = blk * N_KV_HEADS + h
            for filler in fillers[cuts[unit]:cuts[unit + 1]]:
                filler()
            _scores_group(layer, qb, kmask0 if blk == 0 else None, sinks_ref, kvar, bias_ref,
                          p_scr, r0, h)
        for h in range(N_KV_HEADS):
            _values_group(gate_ref, vtvar, p_scr, mix_ref, r0, h)
        if blk == 0:
            for i in range(2 * N_KV_HEADS):
                kvar[i, 0:WINDOW, :] = kvar[i, rows:rows + WINDOW, :]
            for i in range(N_KV_HEADS):
                vtvar[i, :, 0:WINDOW] = vtvar[i, :, rows:rows + WINDOW]

    mix = mix_ref[0:rows, :]
    for c0 in range(0, D_MODEL, OUT_CHUNK):
        cols = slice(c0, c0 + OUT_CHUNK)
        out = jnp.dot(mix, wout_ref[:, cols], preferred_element_type=F32)
        z_ref[0:rows, cols] = alpha * x[:, cols] + out
    return kf, vf


def _cast_next_weights(t, last, next_layer, win_f32, wout_f32, win_next, wout_next,
                       stage_in, stage_out, sem):
    rows = stage_in[0].shape[0]
    srcs = (win_f32, wout_f32)
    dsts = (win_next, wout_next)

    def read(k, i):
        return pltpu.make_async_copy(srcs[i].at[next_layer, pl.ds(k * rows, rows), :],
                                     stage_in[i], sem.at[i])

    def write(k, i):
        return pltpu.make_async_copy(stage_out[i], dsts[i].at[pl.ds(k * rows, rows), :],
                                     sem.at[2 + i])

    @pl.when(t >= 1)
    def _():
        for i in range(2):
            read(t - 1, i).wait()

        @pl.when(t >= 2)
        def _():
            for i in range(2):
                write(t - 2, i).wait()

        for i in range(2):
            stage_out[i][...] = stage_in[i][...].astype(BF16)
            write(t - 1, i).start()

    @pl.when(t < last)
    def _():
        for i in range(2):
            read(t, i).start()

    @pl.when(t == last)
    def _():
        for i in range(2):
            write(t - 1, i).wait()


def _prompt_kernel_casting(sinks_ref, x_ref, xh_ref, win_ref, wout_ref, cw_ref, g_ref, b_ref,
                           win_f32, wout_f32,
                           y_ref, yh_ref, kl_ref, vl_ref, cs_ref, win_next, wout_next,
                           *scratch, layer, **static):
    cast_in_a, cast_in_b, cast_out_a, cast_out_b, cast_sem = scratch[-5:]
    _cast_next_weights(pl.program_id(0), pl.num_programs(0) - 1, layer + 1, win_f32, wout_f32,
                       win_next, wout_next, (cast_in_a, cast_in_b), (cast_out_a, cast_out_b),
                       cast_sem)
    _prompt_kernel(sinks_ref, x_ref, xh_ref, win_ref, wout_ref, cw_ref, g_ref, b_ref,
                   y_ref, yh_ref, kl_ref, vl_ref, cs_ref, *scratch[:-5], layer=layer, **static)


def _prompt_kernel(sinks_ref, x_ref, xh_ref, win_ref, wout_ref, cw_ref, g_ref, b_ref,
                   y_ref, yh_ref, kl_ref, vl_ref, cs_ref,
                   kvar, vtvar, khead, vhead, ucar, uhead, bias_ref, mix_ref, p_scr, gate_ref,
                   z_scr, *, layer, tm, n_tiles, alpha):
    t = pl.program_id(0)
    last = pl.num_programs(0) - 1
    j = t % n_tiles
    shared = (layer, sinks_ref, win_ref, wout_ref, cw_ref, kvar, vtvar, ucar, bias_ref, mix_ref,
              p_scr, gate_ref, z_scr, alpha)
    key_row = lax.broadcasted_iota(jnp.int32, (2 * WINDOW, WINDOW), 0)

    @pl.when(t == 0)
    def _():
        qi = lax.broadcasted_iota(jnp.int32, (2 * WINDOW, WINDOW), 1)
        dist = WINDOW + qi - key_row
        visible = (dist >= 0) & (dist < WINDOW)
        bias_ref[...] = jnp.where(visible, -dist.astype(F32), NEG_INF)
        z_scr[...] = jnp.zeros(z_scr.shape, F32)
        kvar[:, 0:WINDOW, :] = jnp.zeros((2 * N_KV_HEADS, WINDOW, LANES), BF16)
        vtvar[:, :, 0:WINDOW] = jnp.zeros((N_KV_HEADS, HEAD_DIM, WINDOW), BF16)
        ucar[...] = jnp.zeros(ucar.shape, F32)
        kmask = jnp.where(key_row < WINDOW + HEAD_PAD, NEG_INF, 0.0)
        xh = jnp.concatenate([jnp.zeros((HEAD_PAD, D_MODEL), F32), xh_ref[...]], axis=0)
        _prompt_rows(xh, WINDOW, kmask, lambda: None, *shared)
        yh_ref[...] = _layer_norm(z_scr[0:WINDOW, :], g_ref[...], b_ref[...])[HEAD_PAD:, :]
        khead[...] = kvar[:, 0:WINDOW, :]
        vhead[...] = vtvar[:, :, 0:WINDOW]
        uhead[...] = ucar[...]

    def norm_previous_tile():
        y_ref[...] = _layer_norm(z_scr[...], g_ref[...], b_ref[...])

    @pl.when(t < last)
    def _():
        @pl.when(j == 0)
        def _():
            kvar[:, 0:WINDOW, :] = khead[...]
            vtvar[:, :, 0:WINDOW] = vhead[...]
            ucar[...] = uhead[...]

        kmask = jnp.where(key_row < HEAD_PAD, jnp.where(j == 0, NEG_INF, 0.0), 0.0)
        kf, vf = _prompt_rows(x_ref[...], tm, kmask, norm_previous_tile, *shared)

        @pl.when(j == n_tiles - 1)
        def _():
            kl_ref[...] = kf[tm - WINDOW:tm, :]
            vl_ref[...] = vf[tm - WINDOW:tm, :]
            cs_ref[...] = ucar[...]

    @pl.when(t == last)
    def _():
        norm_previous_tile()


def _resident(shape, index_map):
    return pl.BlockSpec(shape, index_map, pipeline_mode=pl.Buffered(1))


def _prompt_layer(layer, x, xh, w_in_b, w_out_b, w_in, w_out, conv_w, sinks, ln_g, ln_b, alpha):
    batch, seq, d = x.shape
    depth = w_in.shape[0]
    tm = PROMPT_TILE
    n_tiles = seq // tm
    total = batch * n_tiles
    const2 = lambda t: (0, 0)
    this_layer = lambda t: (layer, 0, 0)

    def tile_block(t):
        t = jnp.minimum(t, total - 1)
        return (t // n_tiles, t % n_tiles, 0)

    def prev_tile_block(t):
        return tile_block(jnp.maximum(t - 1, 0))

    per_batch = lambda t: (jnp.minimum(t, total - 1) // n_tiles, 0, 0)
    static = dict(layer=layer, tm=tm, n_tiles=n_tiles, alpha=alpha)
    operands = [sinks, x, xh, w_in_b, w_out_b, conv_w, ln_g, ln_b]
    in_specs = [
        pl.BlockSpec(memory_space=pltpu.SMEM),
        pl.BlockSpec((None, tm, d), tile_block),
        _resident((N_META, d), const2),
        _resident((d, PROJ_DIM), const2),
        _resident((ATT_DIM + CONV_DIM, d), const2),
        pl.BlockSpec((None, 3, CONV_DIM), this_layer),
        pl.BlockSpec((None, 1, d), this_layer),
        pl.BlockSpec((None, 1, d), this_layer),
    ]
    out_specs = [
        pl.BlockSpec((None, tm, d), prev_tile_block),
        pl.BlockSpec((N_META, d), const2),
        pl.BlockSpec((None, WINDOW, KV_DIM), per_batch),
        pl.BlockSpec((None, WINDOW, KV_DIM), per_batch),
        pl.BlockSpec((None, 8, CONV_DIM), per_batch),
    ]
    out_shape = [
        jax.ShapeDtypeStruct((batch, seq, d), F32),
        jax.ShapeDtypeStruct((N_META, d), F32),
        jax.ShapeDtypeStruct((batch, WINDOW, KV_DIM), F32),
        jax.ShapeDtypeStruct((batch, WINDOW, KV_DIM), F32),
        jax.ShapeDtypeStruct((batch, 8, CONV_DIM), F32),
    ]
    scratch_shapes = [
        pltpu.VMEM((2 * N_KV_HEADS, WINDOW + tm, LANES), BF16),
        pltpu.VMEM((N_KV_HEADS, HEAD_DIM, WINDOW + tm), BF16),
        pltpu.VMEM((2 * N_KV_HEADS, WINDOW, LANES), BF16),
        pltpu.VMEM((N_KV_HEADS, HEAD_DIM, WINDOW), BF16),
        pltpu.VMEM((8, CONV_DIM), F32),
        pltpu.VMEM((8, CONV_DIM), F32),
        pltpu.VMEM((2 * WINDOW, WINDOW), F32),
        pltpu.VMEM((tm, ATT_DIM + CONV_DIM), BF16),
        pltpu.VMEM((2 * N_KV_HEADS, 2 * WINDOW, 2 * WINDOW), BF16),
        pltpu.VMEM((tm, ATT_DIM), F32),
        pltpu.VMEM((tm, d), F32),
    ]
    kernel = _prompt_kernel
    if layer + 1 < depth:
        cast_rows = d // total
        assert cast_rows * total == d and cast_rows % 16 == 0
        kernel = _prompt_kernel_casting
        operands += [w_in, w_out]
        in_specs += [pl.BlockSpec(memory_space=pl.ANY)] * 2
        out_specs += [pl.BlockSpec(memory_space=pl.ANY)] * 2
        out_shape += [jax.ShapeDtypeStruct(w_in_b.shape, BF16),
                      jax.ShapeDtypeStruct(w_out_b.shape, BF16)]
        scratch_shapes += [
            pltpu.VMEM((cast_rows, PROJ_DIM), F32),
            pltpu.VMEM((cast_rows, d), F32),
            pltpu.VMEM((cast_rows, PROJ_DIM), BF16),
            pltpu.VMEM((cast_rows, d), BF16),
            pltpu.SemaphoreType.DMA((4,)),
        ]
    return pl.pallas_call(
        functools.partial(kernel, **static),
        grid=(total + 1,),
        in_specs=in_specs,
        out_specs=out_specs,
        out_shape=out_shape,
        scratch_shapes=scratch_shapes,
        compiler_params=pltpu.CompilerParams(
            dimension_semantics=("arbitrary",),
            vmem_limit_bytes=VMEM_LIMIT,
        ),
        name="prompt_layer",
    )(*operands)


SEQ_PER_GROUP = 2
GROUP_ROWS = SEQ_PER_GROUP * SAMPLE_ROWS
GROUPS_PER_STEP = 2
SEQ_PER_STEP = SEQ_PER_GROUP * GROUPS_PER_STEP
N_NEW = 4
TOK0 = SAMPLE_ROWS - N_NEW
LOG_SAMPLE_ROWS = 3
LOG_GROUP_ROWS = 4
LOG_HEAD_DIM = 6
assert (1 << LOG_SAMPLE_ROWS, 1 << LOG_GROUP_ROWS, 1 << LOG_HEAD_DIM) == (
    SAMPLE_ROWS, GROUP_ROWS, HEAD_DIM)


def _sample_kernel(sinks_ref, x_ref, win_hbm, wout_hbm, cw_ref, g_ref, b_ref, ck_ref, cv_ref,
                   st_ref, kb_prev, vb_prev, y_ref, kb_ref, vb_ref, u_ref,
                   win_ref, wout_ref, wsem,
                   qe, knew, vnew, gate_a, bias_ref, sinkcol, mix_ref, *, layer, n_rows, alpha):
    del kb_prev, vb_prev
    grp = pl.program_id(0)
    n_steps = pl.num_programs(0)
    n_q = N_HEADS * GROUP_ROWS
    lane_blk = lax.broadcasted_iota(jnp.int32, (n_rows, KV_DIM), 1) >> LOG_HEAD_DIM

    col_groups = COL_GROUPS

    def win_copy(i):
        cols = pl.ds(col_groups[i], col_groups[i + 1] - col_groups[i])
        return pltpu.make_async_copy(win_hbm.at[:, cols], win_ref.at[:, cols], wsem.at[i])

    def wout_copy():
        return pltpu.make_async_copy(wout_hbm, wout_ref, wsem.at[len(col_groups) - 1])

    def proj(xb, i, width):
        win_copy(i).wait()
        return _proj(xb, win_ref, col_groups[i], width)

    @pl.when(grp == 0)
    def _():
        for i in range(len(col_groups) - 1):
            win_copy(i).start()
        wout_copy().start()
        x = x_ref[...]
        xb = x.astype(BF16)
        hq = proj(xb, 0, C_GA)
        q = hq[:, C_Q:C_Q + ATT_DIM] * Q_SCALE
        knew[...] = hq[:, C_K:C_K + KV_DIM]
        vnew[...] = hq[:, C_V:C_V + KV_DIM]
        for hd in range(N_HEADS):
            h, g = divmod(hd, GQA_GROUP)
            slab = q[:, h * KV_DIM:(h + 1) * KV_DIM]
            moved = pltpu.roll(slab, ((h - g) % GQA_GROUP) * HEAD_DIM, axis=1)
            qe[hd] = jnp.where(lane_blk == h, moved, 0.0).astype(BF16)
        gate_a[...] = _silu(proj(xb, 1, ATT_DIM))

        bg = proj(xb, 2, CONV_DIM)
        u = proj(xb, 3, CONV_DIM) * proj(xb, 4, CONV_DIM)
        r8 = lax.broadcasted_iota(jnp.int32, (n_rows, CONV_DIM), 0) & (SAMPLE_ROWS - 1)
        is_state = (r8 >= TOK0 - 2) & (r8 < TOK0)
        u = jnp.where(is_state, st_ref[...], u)
        u_ref[...] = u
        cy = (cw_ref[0:1, :] * pltpu.roll(u, 2, axis=0) + cw_ref[1:2, :] * pltpu.roll(u, 1, axis=0)
              + cw_ref[2:3, :] * u)
        gate_c = _silu(proj(xb, 5, CONV_DIM))
        mix_ref[:, ATT_DIM:] = ((bg * cy) * gate_c).astype(BF16)

        qrow = lax.broadcasted_iota(jnp.int32, (n_q, 2 * WINDOW), 0)
        key = lax.broadcasted_iota(jnp.int32, (n_q, 2 * WINDOW), 1)
        q_tok = jnp.maximum((qrow & (SAMPLE_ROWS - 1)) - TOK0, 0)
        q_seq = (qrow >> LOG_SAMPLE_ROWS) & (SEQ_PER_GROUP - 1)
        new = key - WINDOW
        k_tok = (new & (SAMPLE_ROWS - 1)) - TOK0
        k_seq = new >> LOG_SAMPLE_ROWS
        cached = key < WINDOW
        dist = jnp.where(cached, WINDOW + q_tok - key, q_tok - k_tok)
        ok_new = (new >= 0) & (new < GROUP_ROWS) & (k_seq == q_seq) & (k_tok >= 0)
        visible = (dist >= 0) & (dist < WINDOW) & (cached | ok_new)
        slope = jnp.zeros((n_q, 2 * WINDOW), F32)
        sink = jnp.zeros((n_q, LANES), F32)
        srow = lax.broadcasted_iota(jnp.int32, (n_q, LANES), 0)
        for hd in range(N_HEADS):
            slope = jnp.where((qrow >> LOG_GROUP_ROWS) == hd, SLOPES[hd], slope)
            sink = jnp.where((srow >> LOG_GROUP_ROWS) == hd, sinks_ref[layer, hd], sink)
        bias_ref[...] = jnp.where(visible, -(slope * dist.astype(F32)), NEG_INF)
        sinkcol[...] = sink

    q_seq = ((lax.broadcasted_iota(jnp.int32, (n_q, KV_DIM), 0) >> LOG_SAMPLE_ROWS)
             & (SEQ_PER_GROUP - 1))
    newest = lax.broadcasted_iota(jnp.int32, (KV_DIM, WINDOW), 1) >= WINDOW - N_NEW
    blk16 = lax.broadcasted_iota(jnp.int32, (GROUP_ROWS, KV_DIM), 1) >> LOG_HEAD_DIM
    pad_rows = jnp.zeros((WINDOW - GROUP_ROWS, KV_DIM), F32)
    sink = sinkcol[:, 0:1]

    probs, v_new_bs, row0s = [], [], []
    for gi in range(GROUPS_PER_STEP):
        g0 = pl.multiple_of((grp * GROUPS_PER_STEP + gi) * GROUP_ROWS, GROUP_ROWS)
        row0s.append(g0)
        w_g = jnp.concatenate([qe[hd, pl.ds(g0, GROUP_ROWS), :] for hd in range(N_HEADS)], axis=0)
        k_new = jnp.concatenate([knew[pl.ds(g0, GROUP_ROWS), :], pad_rows], axis=0)
        v_new = jnp.concatenate([vnew[pl.ds(g0, GROUP_ROWS), :], pad_rows], axis=0)
        v_new_bs.append(v_new.astype(BF16))
        k_new_t = k_new.T
        v_new_t = v_new.T
        sc_cached = None
        for s in range(SEQ_PER_GROUP):
            n = gi * SEQ_PER_GROUP + s
            ck_t = ck_ref[n]
            sc_s = jnp.dot(w_g, ck_t.astype(BF16), preferred_element_type=F32)
            sc_cached = sc_s if s == 0 else jnp.where(q_seq[:, :WINDOW] == s, sc_s, sc_cached)
            to_tail = WINDOW - N_NEW - (s * SAMPLE_ROWS + TOK0)
            for cache_t, new_t, out_ref in ((ck_t, k_new_t, kb_ref), (cv_ref[n], v_new_t, vb_ref)):
                out_ref[n] = jnp.where(newest, pltpu.roll(new_t, to_tail, axis=1),
                                       pltpu.roll(cache_t, WINDOW - N_NEW, axis=1))
        sc_new = lax.dot_general(w_g, k_new.astype(BF16), _NT, preferred_element_type=F32)
        sc = jnp.concatenate([sc_cached, sc_new], axis=1)
        probs.append(_softmax_rows(sc + bias_ref[...], sink).astype(BF16))

    for gi in range(GROUPS_PER_STEP):
        p = probs[gi]
        o_grp = jnp.dot(p[:, WINDOW:], v_new_bs[gi], preferred_element_type=F32)
        o_cached = None
        for s in range(SEQ_PER_GROUP):
            cv_t = cv_ref[gi * SEQ_PER_GROUP + s]
            o_s = lax.dot_general(p[:, :WINDOW], cv_t.astype(BF16), _NT,
                                  preferred_element_type=F32)
            o_cached = o_s if s == 0 else jnp.where(q_seq == s, o_s, o_cached)
        o_grp = o_grp + o_cached
        for h in range(N_KV_HEADS):
            slab = jnp.zeros((GROUP_ROWS, KV_DIM), F32)
            for g in range(GQA_GROUP):
                hd = h * GQA_GROUP + g
                piece = jnp.where(blk16 == h, o_grp[hd * GROUP_ROWS:(hd + 1) * GROUP_ROWS, :], 0.0)
                slab = slab + pltpu.roll(piece, ((g - h) % GQA_GROUP) * HEAD_DIM, axis=1)
            rows = pl.ds(row0s[gi], GROUP_ROWS)
            cols = slice(h * KV_DIM, (h + 1) * KV_DIM)
            mix_ref[rows, cols] = (slab * gate_a[rows, cols]).astype(BF16)

    @pl.when(grp == n_steps - 1)
    def _():
        wout_copy().wait()
        out = jnp.dot(mix_ref[...], wout_ref[...], preferred_element_type=F32)
        y_ref[...] = _layer_norm(alpha * x_ref[...] + out, g_ref[...], b_ref[...])


def _sample_layer(layer, x8, st_all, ck_all, cv_all, kb_all, vb_all, w_in, w_out, conv_w, sinks,
                  ln_g, ln_b, alpha):
    n_rows, d = x8.shape
    depth, n_seq = ck_all.shape[:2]
    this_layer = lambda g: (layer, 0, 0)
    n_steps = n_seq // SEQ_PER_STEP
    n_q = N_HEADS * GROUP_ROWS
    const2 = lambda g: (0, 0)
    cache_spec = pl.BlockSpec((None, SEQ_PER_STEP, KV_DIM, WINDOW), lambda g: (layer, g, 0, 0))
    kernel = functools.partial(_sample_kernel, layer=layer, n_rows=n_rows, alpha=alpha)
    operands = [sinks, x8, w_in, w_out, conv_w, ln_g, ln_b, ck_all, cv_all, st_all, kb_all, vb_all]
    in_specs = [
        pl.BlockSpec(memory_space=pltpu.SMEM),
        _resident((n_rows, d), const2),
        pl.BlockSpec(memory_space=pl.ANY),
        pl.BlockSpec(memory_space=pl.ANY),
        pl.BlockSpec((None, 3, CONV_DIM), this_layer),
        pl.BlockSpec((None, 1, d), this_layer),
        pl.BlockSpec((None, 1, d), this_layer),
        cache_spec,
        cache_spec,
        _resident((None, n_rows, CONV_DIM), this_layer),
        pl.BlockSpec(memory_space=pl.ANY),
        pl.BlockSpec(memory_space=pl.ANY),
    ]
    aliases = {len(operands) - 2: 1, len(operands) - 1: 2}
    return pl.pallas_call(
        kernel,
        grid=(n_steps,),
        in_specs=in_specs,
        out_specs=[
            pl.BlockSpec((n_rows, d), const2),
            cache_spec,
            cache_spec,
            pl.BlockSpec((n_rows, CONV_DIM), const2),
        ],
        out_shape=[
            jax.ShapeDtypeStruct((n_rows, d), F32),
            jax.ShapeDtypeStruct(ck_all.shape, F32),
            jax.ShapeDtypeStruct(cv_all.shape, F32),
            jax.ShapeDtypeStruct((n_rows, CONV_DIM), F32),
        ],
        input_output_aliases=aliases,
        scratch_shapes=[
            pltpu.VMEM((d, PROJ_DIM), BF16),
            pltpu.VMEM((ATT_DIM + CONV_DIM, d), BF16),
            pltpu.SemaphoreType.DMA((len(COL_GROUPS),)),
            pltpu.VMEM((N_HEADS, n_rows, KV_DIM), BF16),
            pltpu.VMEM((n_rows, KV_DIM), F32),
            pltpu.VMEM((n_rows, KV_DIM), F32),
            pltpu.VMEM((n_rows, ATT_DIM), F32),
            pltpu.VMEM((n_q, 2 * WINDOW), F32),
            pltpu.VMEM((n_q, LANES), F32),
            pltpu.VMEM((n_rows, ATT_DIM + CONV_DIM), BF16),
        ],
        compiler_params=pltpu.CompilerParams(
            dimension_semantics=("arbitrary",),
            vmem_limit_bytes=VMEM_LIMIT,
        ),
        name="sample_layer",
    )(*operands)


def kernel(x_prompt, x_sample, cache_k, cache_v, state_conv, meta_tokens,
           w_in, conv_w, sinks, w_out, ln_g, ln_b):
    depth = w_in.shape[0]
    alpha = float((2 * depth) ** 0.25)
    batch, seq, d = x_prompt.shape
    n_seq, n_tok = x_sample.shape[:2]
    assert d == D_MODEL and seq % PROMPT_TILE == 0 and n_tok == SAMPLE_ROWS - TOK0
    assert meta_tokens.shape[0] == N_META and n_seq % SEQ_PER_STEP == 0
    assert cache_k.shape[2] == WINDOW and state_conv.shape[2] == 2

    w_in_b = w_in[0].astype(BF16)
    w_out_b = w_out[0].astype(BF16)
    ln_g3 = ln_g.reshape(depth, 1, d)
    ln_b3 = ln_b.reshape(depth, 1, d)
    xp = x_prompt
    xh = meta_tokens.astype(F32)
    xs = jnp.pad(x_sample, ((0, 0), (TOK0, 0), (0, 0))).reshape(n_seq * SAMPLE_ROWS, d)
    ck_all = jnp.transpose(cache_k, (0, 1, 3, 4, 2)).reshape(depth, n_seq, KV_DIM, WINDOW)
    cv_all = jnp.transpose(cache_v, (0, 1, 3, 4, 2)).reshape(depth, n_seq, KV_DIM, WINDOW)
    st_all = jnp.pad(state_conv, ((0, 0), (0, 0), (TOK0 - 2, SAMPLE_ROWS - TOK0), (0, 0)))
    st_all = st_all.reshape(depth, n_seq * SAMPLE_ROWS, CONV_DIM)

    kp, vp, cp, cs = [], [], [], []
    kb_all = jnp.zeros(ck_all.shape, F32)
    vb_all = jnp.zeros(cv_all.shape, F32)
    for l in range(depth):
        xp, xh, k_last, v_last, c_last, *next_weights = _prompt_layer(
            l, xp, xh, w_in_b, w_out_b, w_in, w_out, conv_w, sinks, ln_g3, ln_b3, alpha)
        kp.append(k_last.reshape(batch, WINDOW, N_KV_HEADS, HEAD_DIM))
        vp.append(v_last.reshape(batch, WINDOW, N_KV_HEADS, HEAD_DIM))
        cp.append(c_last[:, 6:8, :])
        xs, kb_all, vb_all, u8 = _sample_layer(
            l, xs, st_all, ck_all, cv_all, kb_all, vb_all, w_in_b, w_out_b, conv_w, sinks, ln_g3,
            ln_b3, alpha)
        if next_weights:
            w_in_b, w_out_b = next_weights
        cs.append(u8.reshape(n_seq, SAMPLE_ROWS, CONV_DIM)[:, SAMPLE_ROWS - 2:, :])
    y_sample = xs.reshape(n_seq, SAMPLE_ROWS, d)[:, TOK0:, :]
    kv_shape = (depth, n_seq, N_KV_HEADS, HEAD_DIM, WINDOW)
    k_sample = jnp.transpose(kb_all.reshape(kv_shape), (0, 1, 4, 2, 3))
    v_sample = jnp.transpose(vb_all.reshape(kv_shape), (0, 1, 4, 2, 3))
    return (xp, y_sample, jnp.stack(kp), jnp.stack(vp), jnp.stack(cp),
            k_sample, v_sample, jnp.stack(cs))
```

```python
import functools

import numpy as np
import jax
import jax.numpy as jnp
from jax import lax
from jax.experimental import pallas as pl
from jax.experimental.pallas import tpu as pltpu

F32 = jnp.float32
BF16 = jnp.bfloat16

D_MODEL = 2048
N_META = 16
ATT_DIM = 1024
CONV_DIM = 1024
HEAD_DIM = 64
N_HEADS = 16
N_KV_HEADS = 4
GQA_GROUP = N_HEADS // N_KV_HEADS
KV_DIM = N_KV_HEADS * HEAD_DIM
WINDOW = 128
PROJ_DIM = 2 * ATT_DIM + 2 * KV_DIM + 4 * CONV_DIM
LN_EPS = 1e-5
NEG_INF = -1e30
Q_SCALE = HEAD_DIM ** -0.5

C_Q = 0
C_K = ATT_DIM
C_V = C_K + KV_DIM
C_GA = C_V + KV_DIM
C_B = C_GA + ATT_DIM
C_C = C_B + CONV_DIM
C_H = C_C + CONV_DIM
C_GC = C_H + CONV_DIM
COL_GROUPS = (C_Q, C_GA, C_B, C_C, C_H, C_GC, PROJ_DIM)

LANES = 128
HEAD_PAD = WINDOW - N_META
PROMPT_TILE = 256
FILL_CHUNK = 256
OUT_CHUNK = 512
SAMPLE_ROWS = 8
VMEM_LIMIT = 58 * 1024 * 1024

SLOPES = [float(np.float32(2.0 ** (-8.0 * (h + 1) / N_HEADS))) for h in range(N_HEADS)]

_NT = (((1,), (1,)), ((), ()))


def _silu(g):
    return g * (1.0 / (1.0 + jnp.exp(-g)))


def _softmax_rows(s, sink):
    m = jnp.maximum(jnp.max(s, axis=1, keepdims=True), sink)
    p = jnp.exp(s - m)
    denom = jnp.sum(p, axis=1, keepdims=True) + jnp.exp(sink - m)
    return p * (1.0 / denom)


def _softmax_cols(s, sink):
    m = jnp.maximum(jnp.max(s, axis=0, keepdims=True), sink)
    p = jnp.exp(s - m)
    denom = jnp.sum(p, axis=0, keepdims=True) + jnp.exp(sink - m)
    return p * (1.0 / denom)


def _layer_norm(z, g, b):
    mu = jnp.mean(z, axis=1, keepdims=True)
    zc = z - mu
    var = jnp.mean(zc * zc, axis=1, keepdims=True)
    return zc * lax.rsqrt(var + LN_EPS) * g + b


def _proj(xb, w_ref, c0, width):
    return jnp.dot(xb, w_ref[:, c0:c0 + width], preferred_element_type=F32)


def _store_k_variants(src, dst, r0, rows):
    low = lax.broadcasted_iota(jnp.int32, (rows, LANES), 1) < HEAD_DIM
    for cc in range(KV_DIM // LANES):
        col = src[:, cc * LANES:(cc + 1) * LANES]
        swapped = pltpu.roll(col, HEAD_DIM, axis=1)
        h_even, h_odd = 2 * cc, 2 * cc + 1
        dst[2 * h_even + 0, r0:r0 + rows, :] = jnp.where(low, col, 0.0).astype(BF16)
        dst[2 * h_even + 1, r0:r0 + rows, :] = jnp.where(low, 0.0, swapped).astype(BF16)
        dst[2 * h_odd + 0, r0:r0 + rows, :] = jnp.where(low, swapped, 0.0).astype(BF16)
        dst[2 * h_odd + 1, r0:r0 + rows, :] = jnp.where(low, 0.0, col).astype(BF16)


def _conv_chunk_pieces(xb, win_ref, cw_ref, ucar, mix_ref, rows, lo, width):
    cols = slice(lo, lo + width)
    got = {}

    def project(name, c0):
        got[name] = _proj(xb, win_ref, c0 + lo, width)

    def finish():
        u = got["c"] * got["h"]
        row = lax.broadcasted_iota(jnp.int32, (rows, width), 0)
        prev1 = ucar[7:8, cols]
        prev2 = ucar[6:7, cols]
        u1 = jnp.where(row == 0, prev1, pltpu.roll(u, 1, axis=0))
        u2 = jnp.where(row == 0, prev2, jnp.where(row == 1, prev1, pltpu.roll(u, 2, axis=0)))
        cy = cw_ref[0:1, cols] * u2 + cw_ref[1:2, cols] * u1 + cw_ref[2:3, cols] * u
        ucar[:, cols] = u[rows - 8:rows, :]
        gate_c = _silu(_proj(xb, win_ref, C_GC + lo, width))
        mix_ref[0:rows, ATT_DIM + lo:ATT_DIM + lo + width] = (
            (got["b"] * cy) * gate_c).astype(BF16)

    return [functools.partial(project, "b", C_B), functools.partial(project, "c", C_C),
            functools.partial(project, "h", C_H), finish]


def _store_v_transposed(src, dst, c0, rows):
    vt = src.T
    for h in range(N_KV_HEADS):
        dst[h, :, c0:c0 + rows] = vt[h * HEAD_DIM:(h + 1) * HEAD_DIM, :].astype(BF16)


def _scores_group(layer, qb, kmask, sinks_ref, kvar, bias_ref, p_scr, r0, h):
    q4 = jnp.concatenate(
        [qb[r0:r0 + WINDOW, c * LANES:(c + 1) * LANES] for c in (2 * h, 2 * h + 1)], axis=0)
    for par in range(2):
        keys = kvar[2 * h + par, r0:r0 + 2 * WINDOW, :]
        s2 = lax.dot_general(keys, q4, _NT, preferred_element_type=F32)
        for half in range(2):
            hd = GQA_GROUP * h + 2 * half + par
            lanes = slice(half * LANES, (half + 1) * LANES)
            s = s2[:, lanes] + SLOPES[hd] * bias_ref[...]
            if kmask is not None:
                s = s + kmask
            p_scr[2 * h + par, :, lanes] = _softmax_cols(s, sinks_ref[layer, hd]).astype(BF16)


def _values_group(gate_ref, vtvar, p_scr, mix_ref, r0, h):
    vals_t = vtvar[h, :, r0:r0 + 2 * WINDOW]
    o_t = jnp.concatenate([jnp.dot(vals_t, p_scr[2 * h + par], preferred_element_type=F32)
                           for par in range(2)], axis=0)
    for half in range(2):
        lanes = slice((2 * h + half) * LANES, (2 * h + half + 1) * LANES)
        o = o_t[:, half * LANES:(half + 1) * LANES].T
        mix_ref[r0:r0 + WINDOW, lanes] = (o * gate_ref[r0:r0 + WINDOW, lanes]).astype(BF16)


def _gate_chunk(xb, win_ref, gate_ref, rows, lo, width):
    gate_ref[0:rows, lo:lo + width] = _silu(_proj(xb, win_ref, C_GA + lo, width))


def _prompt_rows(x, rows, kmask0, after_qkv, layer, sinks_ref, win_ref, wout_ref, cw_ref,
                 kvar, vtvar, ucar, bias_ref, mix_ref, p_scr, gate_ref, z_ref, alpha):
    xb = x.astype(BF16)
    hq = _proj(xb, win_ref, 0, C_GA)
    after_qkv()
    qb = (hq[:, C_Q:C_Q + ATT_DIM] * Q_SCALE).astype(BF16)
    kf = hq[:, C_K:C_K + KV_DIM]
    vf = hq[:, C_V:C_V + KV_DIM]
    _store_k_variants(kf, kvar, WINDOW, rows)
    _store_v_transposed(vf, vtvar, WINDOW, rows)

    n_blocks = rows // WINDOW
    gates = [functools.partial(_gate_chunk, xb, win_ref, gate_ref, rows, lo, FILL_CHUNK)
             for lo in range(0, ATT_DIM, FILL_CHUNK)]
    convs = [_conv_chunk_pieces(xb, win_ref, cw_ref, ucar, mix_ref, rows, lo, FILL_CHUNK)
             for lo in range(0, CONV_DIM, FILL_CHUNK)]
    fillers = gates[:2] + convs[0] + gates[2:] + [p for conv in convs[1:] for p in conv]
    n_units = n_blocks * N_KV_HEADS
    cuts = [len(fillers) * u // n_units for u in range(n_units + 1)]
    assert cuts[N_KV_HEADS] >= len(gates) + len(convs[0])
    for blk in range(n_blocks):
        r0 = blk * WINDOW
        for h in range(N_KV_HEADS):
            unit = blk * N_KV_HEADS + h
            for filler in fillers[cuts[unit]:cuts[unit + 1]]:
                filler()
            _scores_group(layer, qb, kmask0 if blk == 0 else None, sinks_ref, kvar, bias_ref,
                          p_scr, r0, h)
        for h in range(N_KV_HEADS):
            _values_group(gate_ref, vtvar, p_scr, mix_ref, r0, h)
        if blk == 0:
            for i in range(2 * N_KV_HEADS):
                kvar[i, 0:WINDOW, :] = kvar[i, rows:rows + WINDOW, :]
            for i in range(N_KV_HEADS):
                vtvar[i, :, 0:WINDOW] = vtvar[i, :, rows:rows + WINDOW]

    mix = mix_ref[0:rows, :]
    for c0 in range(0, D_MODEL, OUT_CHUNK):
        cols = slice(c0, c0 + OUT_CHUNK)
        out = jnp.dot(mix, wout_ref[:, cols], preferred_element_type=F32)
        z_ref[0:rows, cols] = alpha * x[:, cols] + out
    return kf, vf


def _cast_next_weights(t, last, next_layer, win_f32, wout_f32, win_next, wout_next,
                       stage_in, stage_out, sem):
    rows = stage_in[0].shape[0]
    srcs = (win_f32, wout_f32)
    dsts = (win_next, wout_next)

    def read(k, i):
        return pltpu.make_async_copy(srcs[i].at[next_layer, pl.ds(k * rows, rows), :],
                                     stage_in[i], sem.at[i])

    def write(k, i):
        return pltpu.make_async_copy(stage_out[i], dsts[i].at[pl.ds(k * rows, rows), :],
                                     sem.at[2 + i])

    @pl.when(t >= 1)
    def _():
        for i in range(2):
            read(t - 1, i).wait()

        @pl.when(t >= 2)
        def _():
            for i in range(2):
                write(t - 2, i).wait()

        for i in range(2):
            stage_out[i][...] = stage_in[i][...].astype(BF16)
            write(t - 1, i).start()

    @pl.when(t < last)
    def _():
        for i in range(2):
            read(t, i).start()

    @pl.when(t == last)
    def _():
        for i in range(2):
            write(t - 1, i).wait()


def _prompt_kernel_casting(sinks_ref, x_ref, xh_ref, win_ref, wout_ref, cw_ref, g_ref, b_ref,
                           win_f32, wout_f32,
                           y_ref, yh_ref, kl_ref, vl_ref, cs_ref, win_next, wout_next,
                           *scratch, layer, **static):
    cast_in_a, cast_in_b, cast_out_a, cast_out_b, cast_sem = scratch[-5:]
    _cast_next_weights(pl.program_id(0), pl.num_programs(0) - 1, layer + 1, win_f32, wout_f32,
                       win_next, wout_next, (cast_in_a, cast_in_b), (cast_out_a, cast_out_b),
                       cast_sem)
    _prompt_kernel(sinks_ref, x_ref, xh_ref, win_ref, wout_ref, cw_ref, g_ref, b_ref,
                   y_ref, yh_ref, kl_ref, vl_ref, cs_ref, *scratch[:-5], layer=layer, **static)


def _prompt_kernel(sinks_ref, x_ref, xh_ref, win_ref, wout_ref, cw_ref, g_ref, b_ref,
                   y_ref, yh_ref, kl_ref, vl_ref, cs_ref,
                   kvar, vtvar, khead, vhead, ucar, uhead, bias_ref, mix_ref, p_scr, gate_ref,
                   z_scr, *, layer, tm, n_tiles, alpha):
    t = pl.program_id(0)
    last = pl.num_programs(0) - 1
    j = t % n_tiles
    shared = (layer, sinks_ref, win_ref, wout_ref, cw_ref, kvar, vtvar, ucar, bias_ref, mix_ref,
              p_scr, gate_ref, z_scr, alpha)
    key_row = lax.broadcasted_iota(jnp.int32, (2 * WINDOW, WINDOW), 0)

    @pl.when(t == 0)
    def _():
        qi = lax.broadcasted_iota(jnp.int32, (2 * WINDOW, WINDOW), 1)
        dist = WINDOW + qi - key_row
        visible = (dist >= 0) & (dist < WINDOW)
        bias_ref[...] = jnp.where(visible, -dist.astype(F32), NEG_INF)
        z_scr[...] = jnp.zeros(z_scr.shape, F32)
        kvar[:, 0:WINDOW, :] = jnp.zeros((2 * N_KV_HEADS, WINDOW, LANES), BF16)
        vtvar[:, :, 0:WINDOW] = jnp.zeros((N_KV_HEADS, HEAD_DIM, WINDOW), BF16)
        ucar[...] = jnp.zeros(ucar.shape, F32)
        kmask = jnp.where(key_row < WINDOW + HEAD_PAD, NEG_INF, 0.0)
        xh = jnp.concatenate([jnp.zeros((HEAD_PAD, D_MODEL), F32), xh_ref[...]], axis=0)
        _prompt_rows(xh, WINDOW, kmask, lambda: None, *shared)
        yh_ref[...] = _layer_norm(z_scr[0:WINDOW, :], g_ref[...], b_ref[...])[HEAD_PAD:, :]
        khead[...] = kvar[:, 0:WINDOW, :]
        vhead[...] = vtvar[:, :, 0:WINDOW]
        uhead[...] = ucar[...]

    def norm_previous_tile():
        y_ref[...] = _layer_norm(z_scr[...], g_ref[...], b_ref[...])

    @pl.when(t < last)
    def _():
        @pl.when(j == 0)
        def _():
            kvar[:, 0:WINDOW, :] = khead[...]
            vtvar[:, :, 0:WINDOW] = vhead[...]
            ucar[...] = uhead[...]

        kmask = jnp.where(key_row < HEAD_PAD, jnp.where(j == 0, NEG_INF, 0.0), 0.0)
        kf, vf = _prompt_rows(x_ref[...], tm, kmask, norm_previous_tile, *shared)

        @pl.when(j == n_tiles - 1)
        def _():
            kl_ref[...] = kf[tm - WINDOW:tm, :]
            vl_ref[...] = vf[tm - WINDOW:tm, :]
            cs_ref[...] = ucar[...]

    @pl.when(t == last)
    def _():
        norm_previous_tile()


def _resident(shape, index_map):
    return pl.BlockSpec(shape, index_map, pipeline_mode=pl.Buffered(1))


def _prompt_layer(layer, x, xh, w_in_b, w_out_b, w_in, w_out, conv_w, sinks, ln_g, ln_b, alpha):
    batch, seq, d = x.shape
    depth = w_in.shape[0]
    tm = PROMPT_TILE
    n_tiles = seq // tm
    total = batch * n_tiles
    const2 = lambda t: (0, 0)
    this_layer = lambda t: (layer, 0, 0)

    def tile_block(t):
        t = jnp.minimum(t, total - 1)
        return (t // n_tiles, t % n_tiles, 0)

    def prev_tile_block(t):
        return tile_block(jnp.maximum(t - 1, 0))

    per_batch = lambda t: (jnp.minimum(t, total - 1) // n_tiles, 0, 0)
    static = dict(layer=layer, tm=tm, n_tiles=n_tiles, alpha=alpha)
    operands = [sinks, x, xh, w_in_b, w_out_b, conv_w, ln_g, ln_b]
    in_specs = [
        pl.BlockSpec(memory_space=pltpu.SMEM),
        pl.BlockSpec((None, tm, d), tile_block),
        _resident((N_META, d), const2),
        _resident((d, PROJ_DIM), const2),
        _resident((ATT_DIM + CONV_DIM, d), const2),
        pl.BlockSpec((None, 3, CONV_DIM), this_layer),
        pl.BlockSpec((None, 1, d), this_layer),
        pl.BlockSpec((None, 1, d), this_layer),
    ]
    out_specs = [
        pl.BlockSpec((None, tm, d), prev_tile_block),
        pl.BlockSpec((N_META, d), const2),
        pl.BlockSpec((None, WINDOW, KV_DIM), per_batch),
        pl.BlockSpec((None, WINDOW, KV_DIM), per_batch),
        pl.BlockSpec((None, 8, CONV_DIM), per_batch),
    ]
    out_shape = [
        jax.ShapeDtypeStruct((batch, seq, d), F32),
        jax.ShapeDtypeStruct((N_META, d), F32),
        jax.ShapeDtypeStruct((batch, WINDOW, KV_DIM), F32),
        jax.ShapeDtypeStruct((batch, WINDOW, KV_DIM), F32),
        jax.ShapeDtypeStruct((batch, 8, CONV_DIM), F32),
    ]
    scratch_shapes = [
        pltpu.VMEM((2 * N_KV_HEADS, WINDOW + tm, LANES), BF16),
        pltpu.VMEM((N_KV_HEADS, HEAD_DIM, WINDOW + tm), BF16),
        pltpu.VMEM((2 * N_KV_HEADS, WINDOW, LANES), BF16),
        pltpu.VMEM((N_KV_HEADS, HEAD_DIM, WINDOW), BF16),
        pltpu.VMEM((8, CONV_DIM), F32),
        pltpu.VMEM((8, CONV_DIM), F32),
        pltpu.VMEM((2 * WINDOW, WINDOW), F32),
        pltpu.VMEM((tm, ATT_DIM + CONV_DIM), BF16),
        pltpu.VMEM((2 * N_KV_HEADS, 2 * WINDOW, 2 * WINDOW), BF16),
        pltpu.VMEM((tm, ATT_DIM), F32),
        pltpu.VMEM((tm, d), F32),
    ]
    kernel = _prompt_kernel
    if layer + 1 < depth:
        cast_rows = d // total
        assert cast_rows * total == d and cast_rows % 16 == 0
        kernel = _prompt_kernel_casting
        operands += [w_in, w_out]
        in_specs += [pl.BlockSpec(memory_space=pl.ANY)] * 2
        out_specs += [pl.BlockSpec(memory_space=pl.ANY)] * 2
        out_shape += [jax.ShapeDtypeStruct(w_in_b.shape, BF16),
                      jax.ShapeDtypeStruct(w_out_b.shape, BF16)]
        scratch_shapes += [
            pltpu.VMEM((cast_rows, PROJ_DIM), F32),
            pltpu.VMEM((cast_rows, d), F32),
            pltpu.VMEM((cast_rows, PROJ_DIM), BF16),
            pltpu.VMEM((cast_rows, d), BF16),
            pltpu.SemaphoreType.DMA((4,)),
        ]
    return pl.pallas_call(
        functools.partial(kernel, **static),
        grid=(total + 1,),
        in_specs=in_specs,
        out_specs=out_specs,
        out_shape=out_shape,
        scratch_shapes=scratch_shapes,
        compiler_params=pltpu.CompilerParams(
            dimension_semantics=("arbitrary",),
            vmem_limit_bytes=VMEM_LIMIT,
        ),
        name="prompt_layer",
    )(*operands)


SEQ_PER_GROUP = 2
GROUP_ROWS = SEQ_PER_GROUP * SAMPLE_ROWS
GROUPS_PER_STEP = 2
SEQ_PER_STEP = SEQ_PER_GROUP * GROUPS_PER_STEP
N_NEW = 4
TOK0 = SAMPLE_ROWS - N_NEW
LOG_SAMPLE_ROWS = 3
LOG_GROUP_ROWS = 4
LOG_HEAD_DIM = 6
assert (1 << LOG_SAMPLE_ROWS, 1 << LOG_GROUP_ROWS, 1 << LOG_HEAD_DIM) == (
    SAMPLE_ROWS, GROUP_ROWS, HEAD_DIM)


def _sample_kernel(sinks_ref, x_hbm, cw_ref, g_ref, b_ref, ck_ref, cv_ref, st_ref, *refs,
                   depth, n_steps, n_rows, alpha):
    win_hbm, wout_hbm = refs[:depth], refs[depth:2 * depth]
    (y_hbm, kb_ref, vb_ref, u_ref, win_ref, wout_ref, wsem, iosem, xcur,
     qe, knew, vnew, gate_a, bias_ref, sinkcol, mix_ref) = refs[2 * depth:]
    layer = pl.program_id(0) // n_steps
    grp = pl.program_id(0) % n_steps
    n_q = N_HEADS * GROUP_ROWS
    lane_blk = lax.broadcasted_iota(jnp.int32, (n_rows, KV_DIM), 1) >> LOG_HEAD_DIM

    col_groups = COL_GROUPS
    n_win = len(col_groups) - 1

    def win_copy(l, i):
        cols = pl.ds(col_groups[i], col_groups[i + 1] - col_groups[i])
        return pltpu.make_async_copy(win_hbm[l].at[:, cols], win_ref.at[:, cols], wsem.at[i])

    def wout_copy(l):
        return pltpu.make_async_copy(wout_hbm[l], wout_ref, wsem.at[n_win])

    def start_for_layer(next_layer, start):
        for l in range(1, depth):
            @pl.when(next_layer == l)
            def _():
                start(l)

    def proj(xb, i, width):
        win_copy(0, i).wait()
        return _proj(xb, win_ref, col_groups[i], width)

    @pl.when(pl.program_id(0) == 0)
    def _():
        x_in = pltpu.make_async_copy(x_hbm, xcur, iosem.at[0])
        x_in.start()
        for i in range(n_win):
            win_copy(0, i).start()
        wout_copy(0).start()
        x_in.wait()

    @pl.when(grp == 0)
    def _():
        x = xcur[...]
        xb = x.astype(BF16)
        hq = proj(xb, 0, C_GA)
        q = hq[:, C_Q:C_Q + ATT_DIM] * Q_SCALE
        knew[...] = hq[:, C_K:C_K + KV_DIM]
        vnew[...] = hq[:, C_V:C_V + KV_DIM]
        for hd in range(N_HEADS):
            h, g = divmod(hd, GQA_GROUP)
            slab = q[:, h * KV_DIM:(h + 1) * KV_DIM]
            moved = pltpu.roll(slab, ((h - g) % GQA_GROUP) * HEAD_DIM, axis=1)
            qe[hd] = jnp.where(lane_blk == h, moved, 0.0).astype(BF16)
        gate_a[...] = _silu(proj(xb, 1, ATT_DIM))

        bg = proj(xb, 2, CONV_DIM)
        u = proj(xb, 3, CONV_DIM) * proj(xb, 4, CONV_DIM)
        r8 = lax.broadcasted_iota(jnp.int32, (n_rows, CONV_DIM), 0) & (SAMPLE_ROWS - 1)
        is_state = (r8 >= TOK0 - 2) & (r8 < TOK0)
        u = jnp.where(is_state, st_ref[...], u)
        u_ref[...] = u
        cy = (cw_ref[0:1, :] * pltpu.roll(u, 2, axis=0) + cw_ref[1:2, :] * pltpu.roll(u, 1, axis=0)
              + cw_ref[2:3, :] * u)
        gate_c = _silu(proj(xb, 5, CONV_DIM))
        mix_ref[:, ATT_DIM:] = ((bg * cy) * gate_c).astype(BF16)

        qrow = lax.broadcasted_iota(jnp.int32, (n_q, 2 * WINDOW), 0)
        key = lax.broadcasted_iota(jnp.int32, (n_q, 2 * WINDOW), 1)
        q_tok = jnp.maximum((qrow & (SAMPLE_ROWS - 1)) - TOK0, 0)
        q_seq = (qrow >> LOG_SAMPLE_ROWS) & (SEQ_PER_GROUP - 1)
        new = key - WINDOW
        k_tok = (new & (SAMPLE_ROWS - 1)) - TOK0
        k_seq = new >> LOG_SAMPLE_ROWS
        cached = key < WINDOW
        dist = jnp.where(cached, WINDOW + q_tok - key, q_tok - k_tok)
        ok_new = (new >= 0) & (new < GROUP_ROWS) & (k_seq == q_seq) & (k_tok >= 0)
        visible = (dist >= 0) & (dist < WINDOW) & (cached | ok_new)
        slope = jnp.zeros((n_q, 2 * WINDOW), F32)
        sink = jnp.zeros((n_q, LANES), F32)
        srow = lax.broadcasted_iota(jnp.int32, (n_q, LANES), 0)
        for hd in range(N_HEADS):
            slope = jnp.where((qrow >> LOG_GROUP_ROWS) == hd, SLOPES[hd], slope)
            sink = jnp.where((srow >> LOG_GROUP_ROWS) == hd, sinks_ref[layer, hd], sink)
        bias_ref[...] = jnp.where(visible, -(slope * dist.astype(F32)), NEG_INF)
        sinkcol[...] = sink
        start_for_layer(layer + 1, lambda l: [win_copy(l, i).start() for i in range(n_win)])

    q_seq = ((lax.broadcasted_iota(jnp.int32, (n_q, KV_DIM), 0) >> LOG_SAMPLE_ROWS)
             & (SEQ_PER_GROUP - 1))
    newest = lax.broadcasted_iota(jnp.int32, (KV_DIM, WINDOW), 1) >= WINDOW - N_NEW
    blk16 = lax.broadcasted_iota(jnp.int32, (GROUP_ROWS, KV_DIM), 1) >> LOG_HEAD_DIM
    pad_rows = jnp.zeros((WINDOW - GROUP_ROWS, KV_DIM), F32)
    sink = sinkcol[:, 0:1]

    probs, v_new_bs, row0s = [], [], []
    for gi in range(GROUPS_PER_STEP):
        g0 = pl.multiple_of((grp * GROUPS_PER_STEP + gi) * GROUP_ROWS, GROUP_ROWS)
        row0s.append(g0)
        w_g = jnp.concatenate([qe[hd, pl.ds(g0, GROUP_ROWS), :] for hd in range(N_HEADS)], axis=0)
        k_new = jnp.concatenate([knew[pl.ds(g0, GROUP_ROWS), :], pad_rows], axis=0)
        v_new = jnp.concatenate([vnew[pl.ds(g0, GROUP_ROWS), :], pad_rows], axis=0)
        v_new_bs.append(v_new.astype(BF16))
        k_new_t = k_new.T
        v_new_t = v_new.T
        sc_cached = None
        for s in range(SEQ_PER_GROUP):
            n = gi * SEQ_PER_GROUP + s
            ck_t = ck_ref[n]
            sc_s = jnp.dot(w_g, ck_t.astype(BF16), preferred_element_type=F32)
            sc_cached = sc_s if s == 0 else jnp.where(q_seq[:, :WINDOW] == s, sc_s, sc_cached)
            to_tail = WINDOW - N_NEW - (s * SAMPLE_ROWS + TOK0)
            for cache_t, new_t, out_ref in ((ck_t, k_new_t, kb_ref), (cv_ref[n], v_new_t, vb_ref)):
                out_ref[n] = jnp.where(newest, pltpu.roll(new_t, to_tail, axis=1),
                                       pltpu.roll(cache_t, WINDOW - N_NEW, axis=1))
        sc_new = lax.dot_general(w_g, k_new.astype(BF16), _NT, preferred_element_type=F32)
        sc = jnp.concatenate([sc_cached, sc_new], axis=1)
        probs.append(_softmax_rows(sc + bias_ref[...], sink).astype(BF16))

    for gi in range(GROUPS_PER_STEP):
        p = probs[gi]
        o_grp = jnp.dot(p[:, WINDOW:], v_new_bs[gi], preferred_element_type=F32)
        o_cached = None
        for s in range(SEQ_PER_GROUP):
            cv_t = cv_ref[gi * SEQ_PER_GROUP + s]
            o_s = lax.dot_general(p[:, :WINDOW], cv_t.astype(BF16), _NT,
                                  preferred_element_type=F32)
            o_cached = o_s if s == 0 else jnp.where(q_seq == s, o_s, o_cached)
        o_grp = o_grp + o_cached
        for h in range(N_KV_HEADS):
            slab = jnp.zeros((GROUP_ROWS, KV_DIM), F32)
            for g in range(GQA_GROUP):
                hd = h * GQA_GROUP + g
                piece = jnp.where(blk16 == h, o_grp[hd * GROUP_ROWS:(hd + 1) * GROUP_ROWS, :], 0.0)
                slab = slab + pltpu.roll(piece, ((g - h) % GQA_GROUP) * HEAD_DIM, axis=1)
            rows = pl.ds(row0s[gi], GROUP_ROWS)
            cols = slice(h * KV_DIM, (h + 1) * KV_DIM)
            mix_ref[rows, cols] = (slab * gate_a[rows, cols]).astype(BF16)

    @pl.when(grp == n_steps - 1)
    def _():
        wout_copy(0).wait()
        out = jnp.dot(mix_ref[...], wout_ref[...], preferred_element_type=F32)
        xcur[...] = _layer_norm(alpha * xcur[...] + out, g_ref[...], b_ref[...])
        start_for_layer(layer + 1, lambda l: wout_copy(l).start())

        @pl.when(layer == depth - 1)
        def _():
            y_out = pltpu.make_async_copy(xcur, y_hbm, iosem.at[1])
            y_out.start()
            y_out.wait()


def _sample_layers(x8, st_all, ck_all, cv_all, w_in_bs, w_out_bs, conv_w, sinks, ln_g, ln_b, alpha):
    n_rows, d = x8.shape
    depth, n_seq = ck_all.shape[:2]
    n_steps = n_seq // SEQ_PER_STEP
    n_q = N_HEADS * GROUP_ROWS
    by_layer = lambda g: (g // n_steps, 0, 0)
    cache_spec = pl.BlockSpec((None, SEQ_PER_STEP, KV_DIM, WINDOW),
                              lambda g: (g // n_steps, g % n_steps, 0, 0))
    hbm = pl.BlockSpec(memory_space=pl.ANY)
    kernel = functools.partial(_sample_kernel, depth=depth, n_steps=n_steps, n_rows=n_rows,
                               alpha=alpha)
    return pl.pallas_call(
        kernel,
        grid=(depth * n_steps,),
        in_specs=[
            pl.BlockSpec(memory_space=pltpu.SMEM),
            hbm,
            pl.BlockSpec((None, 3, CONV_DIM), by_layer),
            pl.BlockSpec((None, 1, d), by_layer),
            pl.BlockSpec((None, 1, d), by_layer),
            cache_spec,
            cache_spec,
            pl.BlockSpec((None, n_rows, CONV_DIM), by_layer),
        ] + [hbm] * (2 * depth),
        out_specs=[
            hbm,
            cache_spec,
            cache_spec,
            pl.BlockSpec((None, n_rows, CONV_DIM), by_layer),
        ],
        out_shape=[
            jax.ShapeDtypeStruct((n_rows, d), F32),
            jax.ShapeDtypeStruct(ck_all.shape, F32),
            jax.ShapeDtypeStruct(cv_all.shape, F32),
            jax.ShapeDtypeStruct((depth, n_rows, CONV_DIM), F32),
        ],
        scratch_shapes=[
            pltpu.VMEM((d, PROJ_DIM), BF16),
            pltpu.VMEM((ATT_DIM + CONV_DIM, d), BF16),
            pltpu.SemaphoreType.DMA((len(COL_GROUPS),)),
            pltpu.SemaphoreType.DMA((2,)),
            pltpu.VMEM((n_rows, d), F32),
            pltpu.VMEM((N_HEADS, n_rows, KV_DIM), BF16),
            pltpu.VMEM((n_rows, KV_DIM), F32),
            pltpu.VMEM((n_rows, KV_DIM), F32),
            pltpu.VMEM((n_rows, ATT_DIM), F32),
            pltpu.VMEM((n_q, 2 * WINDOW), F32),
            pltpu.VMEM((n_q, LANES), F32),
            pltpu.VMEM((n_rows, ATT_DIM + CONV_DIM), BF16),
        ],
        compiler_params=pltpu.CompilerParams(
            dimension_semantics=("arbitrary",),
            vmem_limit_bytes=VMEM_LIMIT,
        ),
        name="sample_layers",
    )(sinks, x8, conv_w, ln_g, ln_b, ck_all, cv_all, st_all, *w_in_bs, *w_out_bs)


def kernel(x_prompt, x_sample, cache_k, cache_v, state_conv, meta_tokens,
           w_in, conv_w, sinks, w_out, ln_g, ln_b):
    depth = w_in.shape[0]
    alpha = float((2 * depth) ** 0.25)
    batch, seq, d = x_prompt.shape
    n_seq, n_tok = x_sample.shape[:2]
    assert d == D_MODEL and seq % PROMPT_TILE == 0 and n_tok == SAMPLE_ROWS - TOK0
    assert meta_tokens.shape[0] == N_META and n_seq % SEQ_PER_STEP == 0
    assert cache_k.shape[2] == WINDOW and state_conv.shape[2] == 2

    w_in_b = w_in[0].astype(BF16)
    w_out_b = w_out[0].astype(BF16)
    ln_g3 = ln_g.reshape(depth, 1, d)
    ln_b3 = ln_b.reshape(depth, 1, d)
    xp = x_prompt
    xh = meta_tokens.astype(F32)
    xs = jnp.pad(x_sample, ((0, 0), (TOK0, 0), (0, 0))).reshape(n_seq * SAMPLE_ROWS, d)
    ck_all = jnp.transpose(cache_k, (0, 1, 3, 4, 2)).reshape(depth, n_seq, KV_DIM, WINDOW)
    cv_all = jnp.transpose(cache_v, (0, 1, 3, 4, 2)).reshape(depth, n_seq, KV_DIM, WINDOW)
    st_all = jnp.pad(state_conv, ((0, 0), (0, 0), (TOK0 - 2, SAMPLE_ROWS - TOK0), (0, 0)))
    st_all = st_all.reshape(depth, n_seq * SAMPLE_ROWS, CONV_DIM)

    kp, vp, cp = [], [], []
    w_in_bs, w_out_bs = [w_in_b], [w_out_b]
    for l in range(depth):
        xp, xh, k_last, v_last, c_last, *next_weights = _prompt_layer(
            l, xp, xh, w_in_bs[l], w_out_bs[l], w_in, w_out, conv_w, sinks, ln_g3, ln_b3, alpha)
        kp.append(k_last.reshape(batch, WINDOW, N_KV_HEADS, HEAD_DIM))
        vp.append(v_last.reshape(batch, WINDOW, N_KV_HEADS, HEAD_DIM))
        cp.append(c_last[:, 6:8, :])
        if next_weights:
            w_in_bs.append(next_weights[0])
            w_out_bs.append(next_weights[1])
    xs, kb_all, vb_all, u_all = _sample_layers(
        xs, st_all, ck_all, cv_all, w_in_bs, w_out_bs, conv_w, sinks, ln_g3, ln_b3, alpha)
    c_sample = u_all.reshape(depth, n_seq, SAMPLE_ROWS, CONV_DIM)[:, :, SAMPLE_ROWS - 2:, :]
    y_sample = xs.reshape(n_seq, SAMPLE_ROWS, d)[:, TOK0:, :]
    kv_shape = (depth, n_seq, N_KV_HEADS, HEAD_DIM, WINDOW)
    k_sample = jnp.transpose(kb_all.reshape(kv_shape), (0, 1, 4, 2, 3))
    v_sample = jnp.transpose(vb_all.reshape(kv_shape), (0, 1, 4, 2, 3))
    return (xp, y_sample, jnp.stack(kp), jnp.stack(vp), jnp.stack(cp),
            k_sample, v_sample, c_sample)
```

```python
import functools

import numpy as np
import jax
import jax.numpy as jnp
from jax import lax
from jax.experimental import pallas as pl
from jax.experimental.pallas import tpu as pltpu

F32 = jnp.float32
BF16 = jnp.bfloat16

D_MODEL = 2048
N_META = 16
ATT_DIM = 1024
CONV_DIM = 1024
HEAD_DIM = 64
N_HEADS = 16
N_KV_HEADS = 4
GQA_GROUP = N_HEADS // N_KV_HEADS
KV_DIM = N_KV_HEADS * HEAD_DIM
WINDOW = 128
PROJ_DIM = 2 * ATT_DIM + 2 * KV_DIM + 4 * CONV_DIM
LN_EPS = 1e-5
NEG_INF = -1e30
Q_SCALE = HEAD_DIM ** -0.5

C_Q = 0
C_K = ATT_DIM
C_V = C_K + KV_DIM
C_GA = C_V + KV_DIM
C_B = C_GA + ATT_DIM
C_C = C_B + CONV_DIM
C_H = C_C + CONV_DIM
C_GC = C_H + CONV_DIM
COL_GROUPS = (C_Q, C_GA, C_B, C_C, C_H, C_GC, PROJ_DIM)

LANES = 128
HEAD_PAD = WINDOW - N_META
PROMPT_TILE = 256
FILL_CHUNK = 256
OUT_CHUNK = 512
SAMPLE_ROWS = 8
VMEM_LIMIT = 58 * 1024 * 1024
DECODE_VMEM_LIMIT = 62 * 1024 * 1024

SLOPES = [float(np.float32(2.0 ** (-8.0 * (h + 1) / N_HEADS))) for h in range(N_HEADS)]

_NT = (((1,), (1,)), ((), ()))


def _silu(g):
    return g * (1.0 / (1.0 + jnp.exp(-g)))


def _softmax_rows(s, sink):
    m = jnp.maximum(jnp.max(s, axis=1, keepdims=True), sink)
    p = jnp.exp(s - m)
    denom = jnp.sum(p, axis=1, keepdims=True) + jnp.exp(sink - m)
    return p * (1.0 / denom)


def _softmax_cols(s, sink):
    m = jnp.maximum(jnp.max(s, axis=0, keepdims=True), sink)
    p = jnp.exp(s - m)
    denom = jnp.sum(p, axis=0, keepdims=True) + jnp.exp(sink - m)
    return p * (1.0 / denom)


def _layer_norm(z, g, b):
    mu = jnp.mean(z, axis=1, keepdims=True)
    zc = z - mu
    var = jnp.mean(zc * zc, axis=1, keepdims=True)
    return zc * lax.rsqrt(var + LN_EPS) * g + b


def _proj(xb, w_ref, c0, width):
    return jnp.dot(xb, w_ref[:, c0:c0 + width], preferred_element_type=F32)


def _store_k_variants(src, dst, r0, rows):
    low = lax.broadcasted_iota(jnp.int32, (rows, LANES), 1) < HEAD_DIM
    for cc in range(KV_DIM // LANES):
        col = src[:, cc * LANES:(cc + 1) * LANES]
        swapped = pltpu.roll(col, HEAD_DIM, axis=1)
        h_even, h_odd = 2 * cc, 2 * cc + 1
        dst[2 * h_even + 0, r0:r0 + rows, :] = jnp.where(low, col, 0.0).astype(BF16)
        dst[2 * h_even + 1, r0:r0 + rows, :] = jnp.where(low, 0.0, swapped).astype(BF16)
        dst[2 * h_odd + 0, r0:r0 + rows, :] = jnp.where(low, swapped, 0.0).astype(BF16)
        dst[2 * h_odd + 1, r0:r0 + rows, :] = jnp.where(low, 0.0, col).astype(BF16)


def _conv_chunk_pieces(xb, win_ref, cw_ref, ucar, mix_ref, rows, lo, width):
    cols = slice(lo, lo + width)
    got = {}

    def project(name, c0):
        got[name] = _proj(xb, win_ref, c0 + lo, width)

    def finish():
        u = got["c"] * got["h"]
        row = lax.broadcasted_iota(jnp.int32, (rows, width), 0)
        prev1 = ucar[7:8, cols]
        prev2 = ucar[6:7, cols]
        u1 = jnp.where(row == 0, prev1, pltpu.roll(u, 1, axis=0))
        u2 = jnp.where(row == 0, prev2, jnp.where(row == 1, prev1, pltpu.roll(u, 2, axis=0)))
        cy = cw_ref[0:1, cols] * u2 + cw_ref[1:2, cols] * u1 + cw_ref[2:3, cols] * u
        ucar[:, cols] = u[rows - 8:rows, :]
        gate_c = _silu(_proj(xb, win_ref, C_GC + lo, width))
        mix_ref[0:rows, ATT_DIM + lo:ATT_DIM + lo + width] = (
            (got["b"] * cy) * gate_c).astype(BF16)

    return [functools.partial(project, "b", C_B), functools.partial(project, "c", C_C),
            functools.partial(project, "h", C_H), finish]


def _store_v_transposed(src, dst, c0, rows):
    vt = src.T
    for h in range(N_KV_HEADS):
        dst[h, :, c0:c0 + rows] = vt[h * HEAD_DIM:(h + 1) * HEAD_DIM, :].astype(BF16)


def _scores_group(layer, qb, kmask, sinks_ref, kvar, bias_ref, p_scr, r0, h):
    q4 = jnp.concatenate(
        [qb[r0:r0 + WINDOW, c * LANES:(c + 1) * LANES] for c in (2 * h, 2 * h + 1)], axis=0)
    for par in range(2):
        keys = kvar[2 * h + par, r0:r0 + 2 * WINDOW, :]
        s2 = lax.dot_general(keys, q4, _NT, preferred_element_type=F32)
        for half in range(2):
            hd = GQA_GROUP * h + 2 * half + par
            lanes = slice(half * LANES, (half + 1) * LANES)
            s = s2[:, lanes] + SLOPES[hd] * bias_ref[...]
            if kmask is not None:
                s = s + kmask
            p_scr[2 * h + par, :, lanes] = _softmax_cols(s, sinks_ref[layer, hd]).astype(BF16)


def _values_group(gate_ref, vtvar, p_scr, mix_ref, r0, h):
    vals_t = vtvar[h, :, r0:r0 + 2 * WINDOW]
    o_t = jnp.concatenate([jnp.dot(vals_t, p_scr[2 * h + par], preferred_element_type=F32)
                           for par in range(2)], axis=0)
    for half in range(2):
        lanes = slice((2 * h + half) * LANES, (2 * h + half + 1) * LANES)
        o = o_t[:, half * LANES:(half + 1) * LANES].T
        mix_ref[r0:r0 + WINDOW, lanes] = (o * gate_ref[r0:r0 + WINDOW, lanes]).astype(BF16)


def _gate_chunk(xb, win_ref, gate_ref, rows, lo, width):
    gate_ref[0:rows, lo:lo + width] = _silu(_proj(xb, win_ref, C_GA + lo, width))


def _prompt_rows(x, rows, kmask0, after_qkv, layer, sinks_ref, win_ref, wout_ref, cw_ref,
                 kvar, vtvar, ucar, bias_ref, mix_ref, p_scr, gate_ref, z_ref, alpha):
    xb = x.astype(BF16)
    hq = _proj(xb, win_ref, 0, C_GA)
    after_qkv()
    qb = (hq[:, C_Q:C_Q + ATT_DIM] * Q_SCALE).astype(BF16)
    kf = hq[:, C_K:C_K + KV_DIM]
    vf = hq[:, C_V:C_V + KV_DIM]
    _store_k_variants(kf, kvar, WINDOW, rows)
    _store_v_transposed(vf, vtvar, WINDOW, rows)

    n_blocks = rows // WINDOW
    gates = [functools.partial(_gate_chunk, xb, win_ref, gate_ref, rows, lo, FILL_CHUNK)
             for lo in range(0, ATT_DIM, FILL_CHUNK)]
    convs = [_conv_chunk_pieces(xb, win_ref, cw_ref, ucar, mix_ref, rows, lo, FILL_CHUNK)
             for lo in range(0, CONV_DIM, FILL_CHUNK)]
    fillers = gates[:2] + convs[0] + gates[2:] + [p for conv in convs[1:] for p in conv]
    n_units = n_blocks * N_KV_HEADS
    cuts = [len(fillers) * u // n_units for u in range(n_units + 1)]
    assert cuts[N_KV_HEADS] >= len(gates) + len(convs[0])
    for blk in range(n_blocks):
        r0 = blk * WINDOW
        for h in range(N_KV_HEADS):
            unit = blk * N_KV_HEADS + h
            for filler in fillers[cuts[unit]:cuts[unit + 1]]:
                filler()
            _scores_group(layer, qb, kmask0 if blk == 0 else None, sinks_ref, kvar, bias_ref,
                          p_scr, r0, h)
        for h in range(N_KV_HEADS):
            _values_group(gate_ref, vtvar, p_scr, mix_ref, r0, h)
        if blk == 0:
            for i in range(2 * N_KV_HEADS):
                kvar[i, 0:WINDOW, :] = kvar[i, rows:rows + WINDOW, :]
            for i in range(N_KV_HEADS):
                vtvar[i, :, 0:WINDOW] = vtvar[i, :, rows:rows + WINDOW]

    mix = mix_ref[0:rows, :]
    for c0 in range(0, D_MODEL, OUT_CHUNK):
        cols = slice(c0, c0 + OUT_CHUNK)
        out = jnp.dot(mix, wout_ref[:, cols], preferred_element_type=F32)
        z_ref[0:rows, cols] = alpha * x[:, cols] + out
    return kf, vf


def _cast_next_weights(t, last, next_layer, win_f32, wout_f32, win_next, wout_next,
                       stage_in, stage_out, sem):
    rows = stage_in[0].shape[0]
    srcs = (win_f32, wout_f32)
    dsts = (win_next, wout_next)

    def read(k, i):
        return pltpu.make_async_copy(srcs[i].at[next_layer, pl.ds(k * rows, rows), :],
                                     stage_in[i], sem.at[i])

    def write(k, i):
        return pltpu.make_async_copy(stage_out[i], dsts[i].at[pl.ds(k * rows, rows), :],
                                     sem.at[2 + i])

    @pl.when(t >= 1)
    def _():
        for i in range(2):
            read(t - 1, i).wait()

        @pl.when(t >= 2)
        def _():
            for i in range(2):
                write(t - 2, i).wait()

        for i in range(2):
            stage_out[i][...] = stage_in[i][...].astype(BF16)
            write(t - 1, i).start()

    @pl.when(t < last)
    def _():
        for i in range(2):
            read(t, i).start()

    @pl.when(t == last)
    def _():
        for i in range(2):
            write(t - 1, i).wait()


def _prompt_kernel_casting(sinks_ref, x_ref, xh_ref, win_ref, wout_ref, cw_ref, g_ref, b_ref,
                           win_f32, wout_f32,
                           y_ref, yh_ref, kl_ref, vl_ref, cs_ref, win_next, wout_next,
                           *scratch, layer, **static):
    cast_in_a, cast_in_b, cast_out_a, cast_out_b, cast_sem = scratch[-5:]
    _cast_next_weights(pl.program_id(0), pl.num_programs(0) - 1, layer + 1, win_f32, wout_f32,
                       win_next, wout_next, (cast_in_a, cast_in_b), (cast_out_a, cast_out_b),
                       cast_sem)
    _prompt_kernel(sinks_ref, x_ref, xh_ref, win_ref, wout_ref, cw_ref, g_ref, b_ref,
                   y_ref, yh_ref, kl_ref, vl_ref, cs_ref, *scratch[:-5], layer=layer, **static)


def _prompt_kernel(sinks_ref, x_ref, xh_ref, win_ref, wout_ref, cw_ref, g_ref, b_ref,
                   y_ref, yh_ref, kl_ref, vl_ref, cs_ref,
                   kvar, vtvar, khead, vhead, ucar, uhead, bias_ref, mix_ref, p_scr, gate_ref,
                   z_scr, *, layer, tm, n_tiles, alpha):
    t = pl.program_id(0)
    last = pl.num_programs(0) - 1
    j = t % n_tiles
    shared = (layer, sinks_ref, win_ref, wout_ref, cw_ref, kvar, vtvar, ucar, bias_ref, mix_ref,
              p_scr, gate_ref, z_scr, alpha)
    key_row = lax.broadcasted_iota(jnp.int32, (2 * WINDOW, WINDOW), 0)

    @pl.when(t == 0)
    def _():
        qi = lax.broadcasted_iota(jnp.int32, (2 * WINDOW, WINDOW), 1)
        dist = WINDOW + qi - key_row
        visible = (dist >= 0) & (dist < WINDOW)
        bias_ref[...] = jnp.where(visible, -dist.astype(F32), NEG_INF)
        z_scr[...] = jnp.zeros(z_scr.shape, F32)
        kvar[:, 0:WINDOW, :] = jnp.zeros((2 * N_KV_HEADS, WINDOW, LANES), BF16)
        vtvar[:, :, 0:WINDOW] = jnp.zeros((N_KV_HEADS, HEAD_DIM, WINDOW), BF16)
        ucar[...] = jnp.zeros(ucar.shape, F32)
        kmask = jnp.where(key_row < WINDOW + HEAD_PAD, NEG_INF, 0.0)
        xh = jnp.concatenate([jnp.zeros((HEAD_PAD, D_MODEL), F32), xh_ref[...]], axis=0)
        _prompt_rows(xh, WINDOW, kmask, lambda: None, *shared)
        yh_ref[...] = _layer_norm(z_scr[0:WINDOW, :], g_ref[...], b_ref[...])[HEAD_PAD:, :]
        khead[...] = kvar[:, 0:WINDOW, :]
        vhead[...] = vtvar[:, :, 0:WINDOW]
        uhead[...] = ucar[...]

    def norm_previous_tile():
        y_ref[...] = _layer_norm(z_scr[...], g_ref[...], b_ref[...])

    @pl.when(t < last)
    def _():
        @pl.when(j == 0)
        def _():
            kvar[:, 0:WINDOW, :] = khead[...]
            vtvar[:, :, 0:WINDOW] = vhead[...]
            ucar[...] = uhead[...]

        kmask = jnp.where(key_row < HEAD_PAD, jnp.where(j == 0, NEG_INF, 0.0), 0.0)
        kf, vf = _prompt_rows(x_ref[...], tm, kmask, norm_previous_tile, *shared)

        @pl.when(j == n_tiles - 1)
        def _():
            kl_ref[...] = kf[tm - WINDOW:tm, :]
            vl_ref[...] = vf[tm - WINDOW:tm, :]
            cs_ref[...] = ucar[...]

    @pl.when(t == last)
    def _():
        norm_previous_tile()


def _resident(shape, index_map):
    return pl.BlockSpec(shape, index_map, pipeline_mode=pl.Buffered(1))


def _prompt_layer(layer, x, xh, w_in_b, w_out_b, w_in, w_out, conv_w, sinks, ln_g, ln_b, alpha):
    batch, seq, d = x.shape
    depth = w_in.shape[0]
    tm = PROMPT_TILE
    n_tiles = seq // tm
    total = batch * n_tiles
    const2 = lambda t: (0, 0)
    this_layer = lambda t: (layer, 0, 0)

    def tile_block(t):
        t = jnp.minimum(t, total - 1)
        return (t // n_tiles, t % n_tiles, 0)

    def prev_tile_block(t):
        return tile_block(jnp.maximum(t - 1, 0))

    per_batch = lambda t: (jnp.minimum(t, total - 1) // n_tiles, 0, 0)
    static = dict(layer=layer, tm=tm, n_tiles=n_tiles, alpha=alpha)
    operands = [sinks, x, xh, w_in_b, w_out_b, conv_w, ln_g, ln_b]
    in_specs = [
        pl.BlockSpec(memory_space=pltpu.SMEM),
        pl.BlockSpec((None, tm, d), tile_block),
        _resident((N_META, d), const2),
        _resident((d, PROJ_DIM), const2),
        _resident((ATT_DIM + CONV_DIM, d), const2),
        pl.BlockSpec((None, 3, CONV_DIM), this_layer),
        pl.BlockSpec((None, 1, d), this_layer),
        pl.BlockSpec((None, 1, d), this_layer),
    ]
    out_specs = [
        pl.BlockSpec((None, tm, d), prev_tile_block),
        pl.BlockSpec((N_META, d), const2),
        pl.BlockSpec((None, WINDOW, KV_DIM), per_batch),
        pl.BlockSpec((None, WINDOW, KV_DIM), per_batch),
        pl.BlockSpec((None, 8, CONV_DIM), per_batch),
    ]
    out_shape = [
        jax.ShapeDtypeStruct((batch, seq, d), F32),
        jax.ShapeDtypeStruct((N_META, d), F32),
        jax.ShapeDtypeStruct((batch, WINDOW, KV_DIM), F32),
        jax.ShapeDtypeStruct((batch, WINDOW, KV_DIM), F32),
        jax.ShapeDtypeStruct((batch, 8, CONV_DIM), F32),
    ]
    scratch_shapes = [
        pltpu.VMEM((2 * N_KV_HEADS, WINDOW + tm, LANES), BF16),
        pltpu.VMEM((N_KV_HEADS, HEAD_DIM, WINDOW + tm), BF16),
        pltpu.VMEM((2 * N_KV_HEADS, WINDOW, LANES), BF16),
        pltpu.VMEM((N_KV_HEADS, HEAD_DIM, WINDOW), BF16),
        pltpu.VMEM((8, CONV_DIM), F32),
        pltpu.VMEM((8, CONV_DIM), F32),
        pltpu.VMEM((2 * WINDOW, WINDOW), F32),
        pltpu.VMEM((tm, ATT_DIM + CONV_DIM), BF16),
        pltpu.VMEM((2 * N_KV_HEADS, 2 * WINDOW, 2 * WINDOW), BF16),
        pltpu.VMEM((tm, ATT_DIM), F32),
        pltpu.VMEM((tm, d), F32),
    ]
    kernel = _prompt_kernel
    if layer + 1 < depth:
        cast_rows = d // total
        assert cast_rows * total == d and cast_rows % 16 == 0
        kernel = _prompt_kernel_casting
        operands += [w_in, w_out]
        in_specs += [pl.BlockSpec(memory_space=pl.ANY)] * 2
        out_specs += [pl.BlockSpec(memory_space=pl.ANY)] * 2
        out_shape += [jax.ShapeDtypeStruct(w_in_b.shape, BF16),
                      jax.ShapeDtypeStruct(w_out_b.shape, BF16)]
        scratch_shapes += [
            pltpu.VMEM((cast_rows, PROJ_DIM), F32),
            pltpu.VMEM((cast_rows, d), F32),
            pltpu.VMEM((cast_rows, PROJ_DIM), BF16),
            pltpu.VMEM((cast_rows, d), BF16),
            pltpu.SemaphoreType.DMA((4,)),
        ]
    return pl.pallas_call(
        functools.partial(kernel, **static),
        grid=(total + 1,),
        in_specs=in_specs,
        out_specs=out_specs,
        out_shape=out_shape,
        scratch_shapes=scratch_shapes,
        compiler_params=pltpu.CompilerParams(
            dimension_semantics=("arbitrary",),
            vmem_limit_bytes=VMEM_LIMIT,
        ),
        name="prompt_layer",
    )(*operands)


SEQ_PER_GROUP = 2
GROUP_ROWS = SEQ_PER_GROUP * SAMPLE_ROWS
GROUPS_PER_STEP = 4
SEQ_PER_STEP = SEQ_PER_GROUP * GROUPS_PER_STEP
N_NEW = 4
TOK0 = SAMPLE_ROWS - N_NEW
LOG_SAMPLE_ROWS = 3
LOG_GROUP_ROWS = 4
LOG_HEAD_DIM = 6
assert (1 << LOG_SAMPLE_ROWS, 1 << LOG_GROUP_ROWS, 1 << LOG_HEAD_DIM) == (
    SAMPLE_ROWS, GROUP_ROWS, HEAD_DIM)


def _sample_kernel(sinks_ref, x_hbm, cw_ref, g_ref, b_ref, ck_ref, cv_ref, st_ref, *refs,
                   depth, n_steps, n_rows, alpha):
    win_hbm, wout_hbm = refs[:depth], refs[depth:2 * depth]
    (y_hbm, kb_ref, vb_ref, u_ref, win_ref, wout_ref, wsem, iosem, xcur,
     qe, knew, vnew, gate_a, bias_ref, sinkcol, mix_ref) = refs[2 * depth:]
    layer = pl.program_id(0) // n_steps
    grp = pl.program_id(0) % n_steps
    n_q = N_HEADS * GROUP_ROWS
    lane_blk = lax.broadcasted_iota(jnp.int32, (n_rows, KV_DIM), 1) >> LOG_HEAD_DIM

    col_groups = COL_GROUPS
    n_win = len(col_groups) - 1

    def win_copy(l, i):
        cols = pl.ds(col_groups[i], col_groups[i + 1] - col_groups[i])
        return pltpu.make_async_copy(win_hbm[l].at[:, cols], win_ref.at[:, cols], wsem.at[i])

    def wout_copy(l):
        return pltpu.make_async_copy(wout_hbm[l], wout_ref, wsem.at[n_win])

    def start_for_layer(next_layer, start):
        for l in range(1, depth):
            @pl.when(next_layer == l)
            def _():
                start(l)

    def proj(xb, i, width):
        win_copy(0, i).wait()
        return _proj(xb, win_ref, col_groups[i], width)

    @pl.when(pl.program_id(0) == 0)
    def _():
        x_in = pltpu.make_async_copy(x_hbm, xcur, iosem.at[0])
        x_in.start()
        for i in range(n_win):
            win_copy(0, i).start()
        wout_copy(0).start()
        x_in.wait()

    @pl.when(grp == 0)
    def _():
        x = xcur[...]
        xb = x.astype(BF16)
        hq = proj(xb, 0, C_GA)
        q = hq[:, C_Q:C_Q + ATT_DIM] * Q_SCALE
        knew[...] = hq[:, C_K:C_K + KV_DIM]
        vnew[...] = hq[:, C_V:C_V + KV_DIM]
        for hd in range(N_HEADS):
            h, g = divmod(hd, GQA_GROUP)
            slab = q[:, h * KV_DIM:(h + 1) * KV_DIM]
            moved = pltpu.roll(slab, ((h - g) % GQA_GROUP) * HEAD_DIM, axis=1)
            qe[hd] = jnp.where(lane_blk == h, moved, 0.0).astype(BF16)
        gate_a[...] = _silu(proj(xb, 1, ATT_DIM))

        bg = proj(xb, 2, CONV_DIM)
        u = proj(xb, 3, CONV_DIM) * proj(xb, 4, CONV_DIM)
        r8 = lax.broadcasted_iota(jnp.int32, (n_rows, CONV_DIM), 0) & (SAMPLE_ROWS - 1)
        is_state = (r8 >= TOK0 - 2) & (r8 < TOK0)
        u = jnp.where(is_state, st_ref[...], u)
        u_ref[...] = u
        cy = (cw_ref[0:1, :] * pltpu.roll(u, 2, axis=0) + cw_ref[1:2, :] * pltpu.roll(u, 1, axis=0)
              + cw_ref[2:3, :] * u)
        gate_c = _silu(proj(xb, 5, CONV_DIM))
        mix_ref[:, ATT_DIM:] = ((bg * cy) * gate_c).astype(BF16)

        qrow = lax.broadcasted_iota(jnp.int32, (n_q, 2 * WINDOW), 0)
        key = lax.broadcasted_iota(jnp.int32, (n_q, 2 * WINDOW), 1)
        q_tok = jnp.maximum((qrow & (SAMPLE_ROWS - 1)) - TOK0, 0)
        q_seq = (qrow >> LOG_SAMPLE_ROWS) & (SEQ_PER_GROUP - 1)
        new = key - WINDOW
        k_tok = (new & (SAMPLE_ROWS - 1)) - TOK0
        k_seq = new >> LOG_SAMPLE_ROWS
        cached = key < WINDOW
        dist = jnp.where(cached, WINDOW + q_tok - key, q_tok - k_tok)
        ok_new = (new >= 0) & (new < GROUP_ROWS) & (k_seq == q_seq) & (k_tok >= 0)
        visible = (dist >= 0) & (dist < WINDOW) & (cached | ok_new)
        slope = jnp.zeros((n_q, 2 * WINDOW), F32)
        sink = jnp.zeros((n_q, LANES), F32)
        srow = lax.broadcasted_iota(jnp.int32, (n_q, LANES), 0)
        for hd in range(N_HEADS):
            slope = jnp.where((qrow >> LOG_GROUP_ROWS) == hd, SLOPES[hd], slope)
            sink = jnp.where((srow >> LOG_GROUP_ROWS) == hd, sinks_ref[layer, hd], sink)
        bias_ref[...] = jnp.where(visible, -(slope * dist.astype(F32)), NEG_INF)
        sinkcol[...] = sink
        start_for_layer(layer + 1, lambda l: [win_copy(l, i).start() for i in range(n_win)])

    q_seq = ((lax.broadcasted_iota(jnp.int32, (n_q, KV_DIM), 0) >> LOG_SAMPLE_ROWS)
             & (SEQ_PER_GROUP - 1))
    newest = lax.broadcasted_iota(jnp.int32, (KV_DIM, WINDOW), 1) >= WINDOW - N_NEW
    blk16 = lax.broadcasted_iota(jnp.int32, (GROUP_ROWS, KV_DIM), 1) >> LOG_HEAD_DIM
    pad_rows = jnp.zeros((WINDOW - GROUP_ROWS, KV_DIM), F32)
    sink = sinkcol[:, 0:1]

    probs, v_new_bs, row0s = [], [], []
    for gi in range(GROUPS_PER_STEP):
        g0 = pl.multiple_of((grp * GROUPS_PER_STEP + gi) * GROUP_ROWS, GROUP_ROWS)
        row0s.append(g0)
        w_g = jnp.concatenate([qe[hd, pl.ds(g0, GROUP_ROWS), :] for hd in range(N_HEADS)], axis=0)
        k_new = jnp.concatenate([knew[pl.ds(g0, GROUP_ROWS), :], pad_rows], axis=0)
        v_new = jnp.concatenate([vnew[pl.ds(g0, GROUP_ROWS), :], pad_rows], axis=0)
        v_new_bs.append(v_new.astype(BF16))
        k_new_t = k_new.T
        v_new_t = v_new.T
        sc_cached = None
        for s in range(SEQ_PER_GROUP):
            n = gi * SEQ_PER_GROUP + s
            ck_t = ck_ref[n]
            sc_s = jnp.dot(w_g, ck_t.astype(BF16), preferred_element_type=F32)
            sc_cached = sc_s if s == 0 else jnp.where(q_seq[:, :WINDOW] == s, sc_s, sc_cached)
            to_tail = WINDOW - N_NEW - (s * SAMPLE_ROWS + TOK0)
            for cache_t, new_t, out_ref in ((ck_t, k_new_t, kb_ref), (cv_ref[n], v_new_t, vb_ref)):
                out_ref[n] = jnp.where(newest, pltpu.roll(new_t, to_tail, axis=1),
                                       pltpu.roll(cache_t, WINDOW - N_NEW, axis=1))
        sc_new = lax.dot_general(w_g, k_new.astype(BF16), _NT, preferred_element_type=F32)
        sc = jnp.concatenate([sc_cached, sc_new], axis=1)
        probs.append(_softmax_rows(sc + bias_ref[...], sink).astype(BF16))

    for gi in range(GROUPS_PER_STEP):
        p = probs[gi]
        o_grp = jnp.dot(p[:, WINDOW:], v_new_bs[gi], preferred_element_type=F32)
        o_cached = None
        for s in range(SEQ_PER_GROUP):
            cv_t = cv_ref[gi * SEQ_PER_GROUP + s]
            o_s = lax.dot_general(p[:, :WINDOW], cv_t.astype(BF16), _NT,
                                  preferred_element_type=F32)
            o_cached = o_s if s == 0 else jnp.where(q_seq == s, o_s, o_cached)
        o_grp = o_grp + o_cached
        for h in range(N_KV_HEADS):
            slab = jnp.zeros((GROUP_ROWS, KV_DIM), F32)
            for g in range(GQA_GROUP):
                hd = h * GQA_GROUP + g
                piece = jnp.where(blk16 == h, o_grp[hd * GROUP_ROWS:(hd + 1) * GROUP_ROWS, :], 0.0)
                slab = slab + pltpu.roll(piece, ((g - h) % GQA_GROUP) * HEAD_DIM, axis=1)
            rows = pl.ds(row0s[gi], GROUP_ROWS)
            cols = slice(h * KV_DIM, (h + 1) * KV_DIM)
            mix_ref[rows, cols] = (slab * gate_a[rows, cols]).astype(BF16)

    @pl.when(grp == n_steps - 1)
    def _():
        wout_copy(0).wait()
        out = jnp.dot(mix_ref[...], wout_ref[...], preferred_element_type=F32)
        xcur[...] = _layer_norm(alpha * xcur[...] + out, g_ref[...], b_ref[...])
        start_for_layer(layer + 1, lambda l: wout_copy(l).start())

        @pl.when(layer == depth - 1)
        def _():
            y_out = pltpu.make_async_copy(xcur, y_hbm, iosem.at[1])
            y_out.start()
            y_out.wait()


def _sample_layers(x8, st_all, ck_all, cv_all, w_in_bs, w_out_bs, conv_w, sinks, ln_g, ln_b, alpha):
    n_rows, d = x8.shape
    depth, n_seq = ck_all.shape[:2]
    n_steps = n_seq // SEQ_PER_STEP
    n_q = N_HEADS * GROUP_ROWS
    by_layer = lambda g: (g // n_steps, 0, 0)
    cache_spec = pl.BlockSpec((None, SEQ_PER_STEP, KV_DIM, WINDOW),
                              lambda g: (g // n_steps, g % n_steps, 0, 0))
    hbm = pl.BlockSpec(memory_space=pl.ANY)
    kernel = functools.partial(_sample_kernel, depth=depth, n_steps=n_steps, n_rows=n_rows,
                               alpha=alpha)
    return pl.pallas_call(
        kernel,
        grid=(depth * n_steps,),
        in_specs=[
            pl.BlockSpec(memory_space=pltpu.SMEM),
            hbm,
            pl.BlockSpec((None, 3, CONV_DIM), by_layer),
            pl.BlockSpec((None, 1, d), by_layer),
            pl.BlockSpec((None, 1, d), by_layer),
            cache_spec,
            cache_spec,
            pl.BlockSpec((None, n_rows, CONV_DIM), by_layer),
        ] + [hbm] * (2 * depth),
        out_specs=[
            hbm,
            cache_spec,
            cache_spec,
            pl.BlockSpec((None, n_rows, CONV_DIM), by_layer),
        ],
        out_shape=[
            jax.ShapeDtypeStruct((n_rows, d), F32),
            jax.ShapeDtypeStruct(ck_all.shape, F32),
            jax.ShapeDtypeStruct(cv_all.shape, F32),
            jax.ShapeDtypeStruct((depth, n_rows, CONV_DIM), F32),
        ],
        scratch_shapes=[
            pltpu.VMEM((d, PROJ_DIM), BF16),
            pltpu.VMEM((ATT_DIM + CONV_DIM, d), BF16),
            pltpu.SemaphoreType.DMA((len(COL_GROUPS),)),
            pltpu.SemaphoreType.DMA((2,)),
            pltpu.VMEM((n_rows, d), F32),
            pltpu.VMEM((N_HEADS, n_rows, KV_DIM), BF16),
            pltpu.VMEM((n_rows, KV_DIM), F32),
            pltpu.VMEM((n_rows, KV_DIM), F32),
            pltpu.VMEM((n_rows, ATT_DIM), F32),
            pltpu.VMEM((n_q, 2 * WINDOW), F32),
            pltpu.VMEM((n_q, LANES), F32),
            pltpu.VMEM((n_rows, ATT_DIM + CONV_DIM), BF16),
        ],
        compiler_params=pltpu.CompilerParams(
            dimension_semantics=("arbitrary",),
            vmem_limit_bytes=DECODE_VMEM_LIMIT,
        ),
        name="sample_layers",
    )(sinks, x8, conv_w, ln_g, ln_b, ck_all, cv_all, st_all, *w_in_bs, *w_out_bs)


def kernel(x_prompt, x_sample, cache_k, cache_v, state_conv, meta_tokens,
           w_in, conv_w, sinks, w_out, ln_g, ln_b):
    depth = w_in.shape[0]
    alpha = float((2 * depth) ** 0.25)
    batch, seq, d = x_prompt.shape
    n_seq, n_tok = x_sample.shape[:2]
    assert d == D_MODEL and seq % PROMPT_TILE == 0 and n_tok == SAMPLE_ROWS - TOK0
    assert meta_tokens.shape[0] == N_META and n_seq % SEQ_PER_STEP == 0
    assert cache_k.shape[2] == WINDOW and state_conv.shape[2] == 2

    w_in_b = w_in[0].astype(BF16)
    w_out_b = w_out[0].astype(BF16)
    ln_g3 = ln_g.reshape(depth, 1, d)
    ln_b3 = ln_b.reshape(depth, 1, d)
    xp = x_prompt
    xh = meta_tokens.astype(F32)
    xs = jnp.pad(x_sample, ((0, 0), (TOK0, 0), (0, 0))).reshape(n_seq * SAMPLE_ROWS, d)
    ck_all = jnp.transpose(cache_k, (0, 1, 3, 4, 2)).reshape(depth, n_seq, KV_DIM, WINDOW)
    cv_all = jnp.transpose(cache_v, (0, 1, 3, 4, 2)).reshape(depth, n_seq, KV_DIM, WINDOW)
    st_all = jnp.pad(state_conv, ((0, 0), (0, 0), (TOK0 - 2, SAMPLE_ROWS - TOK0), (0, 0)))
    st_all = st_all.reshape(depth, n_seq * SAMPLE_ROWS, CONV_DIM)

    kp, vp, cp = [], [], []
    w_in_bs, w_out_bs = [w_in_b], [w_out_b]
    for l in range(depth):
        xp, xh, k_last, v_last, c_last, *next_weights = _prompt_layer(
            l, xp, xh, w_in_bs[l], w_out_bs[l], w_in, w_out, conv_w, sinks, ln_g3, ln_b3, alpha)
        kp.append(k_last.reshape(batch, WINDOW, N_KV_HEADS, HEAD_DIM))
        vp.append(v_last.reshape(batch, WINDOW, N_KV_HEADS, HEAD_DIM))
        cp.append(c_last[:, 6:8, :])
        if next_weights:
            w_in_bs.append(next_weights[0])
            w_out_bs.append(next_weights[1])
    xs, kb_all, vb_all, u_all = _sample_layers(
        xs, st_all, ck_all, cv_all, w_in_bs, w_out_bs, conv_w, sinks, ln_g3, ln_b3, alpha)
    c_sample = u_all.reshape(depth, n_seq, SAMPLE_ROWS, CONV_DIM)[:, :, SAMPLE_ROWS - 2:, :]
    y_sample = xs.reshape(n_seq, SAMPLE_ROWS, d)[:, TOK0:, :]
    kv_shape = (depth, n_seq, N_KV_HEADS, HEAD_DIM, WINDOW)
    k_sample = jnp.transpose(kb_all.reshape(kv_shape), (0, 1, 4, 2, 3))
    v_sample = jnp.transpose(vb_all.reshape(kv_shape), (0, 1, 4, 2, 3))
    return (xp, y_sample, jnp.stack(kp), jnp.stack(vp), jnp.stack(cp),
            k_sample, v_sample, c_sample)
```

```python
import functools

import numpy as np
import jax
import jax.numpy as jnp
from jax import lax
from jax.experimental import pallas as pl
from jax.experimental.pallas import tpu as pltpu

F32 = jnp.float32
BF16 = jnp.bfloat16

D_MODEL = 2048
N_META = 16
ATT_DIM = 1024
CONV_DIM = 1024
HEAD_DIM = 64
N_HEADS = 16
N_KV_HEADS = 4
GQA_GROUP = N_HEADS // N_KV_HEADS
KV_DIM = N_KV_HEADS * HEAD_DIM
WINDOW = 128
PROJ_DIM = 2 * ATT_DIM + 2 * KV_DIM + 4 * CONV_DIM
LN_EPS = 1e-5
NEG_INF = -1e30
Q_SCALE = HEAD_DIM ** -0.5

C_Q = 0
C_K = ATT_DIM
C_V = C_K + KV_DIM
C_GA = C_V + KV_DIM
C_B = C_GA + ATT_DIM
C_C = C_B + CONV_DIM
C_H = C_C + CONV_DIM
C_GC = C_H + CONV_DIM
COL_GROUPS = (C_Q, C_GA, C_B, C_C, C_H, C_GC, PROJ_DIM)

LANES = 128
HEAD_PAD = WINDOW - N_META
PROMPT_TILE = 256
FILL_CHUNK = 256
OUT_CHUNK = 512
SAMPLE_ROWS = 8
VMEM_LIMIT = 58 * 1024 * 1024
DECODE_VMEM_LIMIT = 62 * 1024 * 1024

SLOPES = [float(np.float32(2.0 ** (-8.0 * (h + 1) / N_HEADS))) for h in range(N_HEADS)]

_NT = (((1,), (1,)), ((), ()))


def _silu(g):
    return g * (1.0 / (1.0 + jnp.exp(-g)))


def _softmax_rows(s, sink):
    m = jnp.maximum(jnp.max(s, axis=1, keepdims=True), sink)
    p = jnp.exp(s - m)
    denom = jnp.sum(p, axis=1, keepdims=True) + jnp.exp(sink - m)
    return p * (1.0 / denom)


def _softmax_cols(s, sink):
    m = jnp.maximum(jnp.max(s, axis=0, keepdims=True), sink)
    p = jnp.exp(s - m)
    denom = jnp.sum(p, axis=0, keepdims=True) + jnp.exp(sink - m)
    return p * (1.0 / denom)


def _layer_norm(z, g, b):
    mu = jnp.mean(z, axis=1, keepdims=True)
    zc = z - mu
    var = jnp.mean(zc * zc, axis=1, keepdims=True)
    return zc * lax.rsqrt(var + LN_EPS) * g + b


def _proj(xb, w_ref, c0, width):
    return jnp.dot(xb, w_ref[:, c0:c0 + width], preferred_element_type=F32)


def _store_k_variants(src, dst, r0, rows):
    low = lax.broadcasted_iota(jnp.int32, (rows, LANES), 1) < HEAD_DIM
    for cc in range(KV_DIM // LANES):
        col = src[:, cc * LANES:(cc + 1) * LANES]
        swapped = pltpu.roll(col, HEAD_DIM, axis=1)
        h_even, h_odd = 2 * cc, 2 * cc + 1
        dst[2 * h_even + 0, r0:r0 + rows, :] = jnp.where(low, col, 0.0).astype(BF16)
        dst[2 * h_even + 1, r0:r0 + rows, :] = jnp.where(low, 0.0, swapped).astype(BF16)
        dst[2 * h_odd + 0, r0:r0 + rows, :] = jnp.where(low, swapped, 0.0).astype(BF16)
        dst[2 * h_odd + 1, r0:r0 + rows, :] = jnp.where(low, 0.0, col).astype(BF16)


def _conv_chunk_pieces(xb, win_ref, cw_ref, ucar, mix_ref, rows, lo, width):
    cols = slice(lo, lo + width)
    got = {}

    def project(name, c0):
        got[name] = _proj(xb, win_ref, c0 + lo, width)

    def finish():
        u = got["c"] * got["h"]
        row = lax.broadcasted_iota(jnp.int32, (rows, width), 0)
        prev1 = ucar[7:8, cols]
        prev2 = ucar[6:7, cols]
        u1 = jnp.where(row == 0, prev1, pltpu.roll(u, 1, axis=0))
        u2 = jnp.where(row == 0, prev2, jnp.where(row == 1, prev1, pltpu.roll(u, 2, axis=0)))
        cy = cw_ref[0:1, cols] * u2 + cw_ref[1:2, cols] * u1 + cw_ref[2:3, cols] * u
        ucar[:, cols] = u[rows - 8:rows, :]
        gate_c = _silu(_proj(xb, win_ref, C_GC + lo, width))
        mix_ref[0:rows, ATT_DIM + lo:ATT_DIM + lo + width] = (
            (got["b"] * cy) * gate_c).astype(BF16)

    return [functools.partial(project, "b", C_B), functools.partial(project, "c", C_C),
            functools.partial(project, "h", C_H), finish]


def _store_v_transposed(src, dst, c0, rows):
    vt = src.T
    for h in range(N_KV_HEADS):
        dst[h, :, c0:c0 + rows] = vt[h * HEAD_DIM:(h + 1) * HEAD_DIM, :].astype(BF16)


def _scores_group(layer, qb, kmask, sinks_ref, kvar, bias_ref, p_scr, r0, h):
    q4 = jnp.concatenate(
        [qb[r0:r0 + WINDOW, c * LANES:(c + 1) * LANES] for c in (2 * h, 2 * h + 1)], axis=0)
    for par in range(2):
        keys = kvar[2 * h + par, r0:r0 + 2 * WINDOW, :]
        s2 = lax.dot_general(keys, q4, _NT, preferred_element_type=F32)
        for half in range(2):
            hd = GQA_GROUP * h + 2 * half + par
            lanes = slice(half * LANES, (half + 1) * LANES)
            s = s2[:, lanes] + SLOPES[hd] * bias_ref[...]
            if kmask is not None:
                s = s + kmask
            p_scr[2 * h + par, :, lanes] = _softmax_cols(s, sinks_ref[layer, hd]).astype(BF16)


def _values_group(gate_ref, vtvar, p_scr, mix_ref, r0, h):
    vals_t = vtvar[h, :, r0:r0 + 2 * WINDOW]
    o_t = jnp.concatenate([jnp.dot(vals_t, p_scr[2 * h + par], preferred_element_type=F32)
                           for par in range(2)], axis=0)
    for half in range(2):
        lanes = slice((2 * h + half) * LANES, (2 * h + half + 1) * LANES)
        o = o_t[:, half * LANES:(half + 1) * LANES].T
        mix_ref[r0:r0 + WINDOW, lanes] = (o * gate_ref[r0:r0 + WINDOW, lanes]).astype(BF16)


def _gate_chunk(xb, win_ref, gate_ref, rows, lo, width):
    gate_ref[0:rows, lo:lo + width] = _silu(_proj(xb, win_ref, C_GA + lo, width))


def _prompt_rows(x, rows, kmask0, after_qkv, layer, sinks_ref, win_ref, wout_ref, cw_ref,
                 kvar, vtvar, ucar, bias_ref, mix_ref, p_scr, gate_ref, z_ref, alpha):
    xb = x.astype(BF16)
    hq = _proj(xb, win_ref, 0, C_GA)
    after_qkv()
    qb = (hq[:, C_Q:C_Q + ATT_DIM] * Q_SCALE).astype(BF16)
    kf = hq[:, C_K:C_K + KV_DIM]
    vf = hq[:, C_V:C_V + KV_DIM]
    _store_k_variants(kf, kvar, WINDOW, rows)
    _store_v_transposed(vf, vtvar, WINDOW, rows)

    n_blocks = rows // WINDOW
    gates = [functools.partial(_gate_chunk, xb, win_ref, gate_ref, rows, lo, FILL_CHUNK)
             for lo in range(0, ATT_DIM, FILL_CHUNK)]
    convs = [_conv_chunk_pieces(xb, win_ref, cw_ref, ucar, mix_ref, rows, lo, FILL_CHUNK)
             for lo in range(0, CONV_DIM, FILL_CHUNK)]
    fillers = gates[:2] + convs[0] + gates[2:] + [p for conv in convs[1:] for p in conv]
    n_units = n_blocks * N_KV_HEADS
    cuts = [len(fillers) * u // n_units for u in range(n_units + 1)]
    assert cuts[N_KV_HEADS] >= len(gates) + len(convs[0])
    for blk in range(n_blocks):
        r0 = blk * WINDOW
        for h in range(N_KV_HEADS):
            unit = blk * N_KV_HEADS + h
            for filler in fillers[cuts[unit]:cuts[unit + 1]]:
                filler()
            _scores_group(layer, qb, kmask0 if blk == 0 else None, sinks_ref, kvar, bias_ref,
                          p_scr, r0, h)
        for h in range(N_KV_HEADS):
            _values_group(gate_ref, vtvar, p_scr, mix_ref, r0, h)
        if blk == 0:
            for i in range(2 * N_KV_HEADS):
                kvar[i, 0:WINDOW, :] = kvar[i, rows:rows + WINDOW, :]
            for i in range(N_KV_HEADS):
                vtvar[i, :, 0:WINDOW] = vtvar[i, :, rows:rows + WINDOW]

    mix = mix_ref[0:rows, :]
    for c0 in range(0, D_MODEL, OUT_CHUNK):
        cols = slice(c0, c0 + OUT_CHUNK)
        out = jnp.dot(mix, wout_ref[:, cols], preferred_element_type=F32)
        z_ref[0:rows, cols] = alpha * x[:, cols] + out
    return kf, vf


def _cast_next_weights(t, last, next_layer, win_f32, wout_f32, win_next, wout_next,
                       stage_in, stage_out, sem):
    rows = stage_in[0].shape[0]
    srcs = (win_f32, wout_f32)
    dsts = (win_next, wout_next)

    def read(k, i):
        return pltpu.make_async_copy(srcs[i].at[next_layer, pl.ds(k * rows, rows), :],
                                     stage_in[i], sem.at[i])

    def write(k, i):
        return pltpu.make_async_copy(stage_out[i], dsts[i].at[pl.ds(k * rows, rows), :],
                                     sem.at[2 + i])

    @pl.when(t >= 1)
    def _():
        for i in range(2):
            read(t - 1, i).wait()

        @pl.when(t >= 2)
        def _():
            for i in range(2):
                write(t - 2, i).wait()

        for i in range(2):
            stage_out[i][...] = stage_in[i][...].astype(BF16)
            write(t - 1, i).start()

    @pl.when(t < last)
    def _():
        for i in range(2):
            read(t, i).start()

    @pl.when(t == last)
    def _():
        for i in range(2):
            write(t - 1, i).wait()


def _prompt_kernel_casting(sinks_ref, x_ref, xh_ref, win_ref, wout_ref, cw_ref, g_ref, b_ref,
                           win_f32, wout_f32,
                           y_ref, yh_ref, kl_ref, vl_ref, cs_ref, win_next, wout_next,
                           *scratch, layer, **static):
    cast_in_a, cast_in_b, cast_out_a, cast_out_b, cast_sem = scratch[-5:]
    _cast_next_weights(pl.program_id(0), pl.num_programs(0) - 1, layer + 1, win_f32, wout_f32,
                       win_next, wout_next, (cast_in_a, cast_in_b), (cast_out_a, cast_out_b),
                       cast_sem)
    _prompt_kernel(sinks_ref, x_ref, xh_ref, win_ref, wout_ref, cw_ref, g_ref, b_ref,
                   y_ref, yh_ref, kl_ref, vl_ref, cs_ref, *scratch[:-5], layer=layer, **static)


def _prompt_kernel(sinks_ref, x_ref, xh_ref, win_ref, wout_ref, cw_ref, g_ref, b_ref,
                   y_ref, yh_ref, kl_ref, vl_ref, cs_ref,
                   kvar, vtvar, khead, vhead, ucar, uhead, bias_ref, mix_ref, p_scr, gate_ref,
                   z_scr, *, layer, tm, n_tiles, alpha):
    t = pl.program_id(0)
    last = pl.num_programs(0) - 1
    j = t % n_tiles
    shared = (layer, sinks_ref, win_ref, wout_ref, cw_ref, kvar, vtvar, ucar, bias_ref, mix_ref,
              p_scr, gate_ref, z_scr, alpha)
    key_row = lax.broadcasted_iota(jnp.int32, (2 * WINDOW, WINDOW), 0)

    @pl.when(t == 0)
    def _():
        qi = lax.broadcasted_iota(jnp.int32, (2 * WINDOW, WINDOW), 1)
        dist = WINDOW + qi - key_row
        visible = (dist >= 0) & (dist < WINDOW)
        bias_ref[...] = jnp.where(visible, -dist.astype(F32), NEG_INF)
        z_scr[...] = jnp.zeros(z_scr.shape, F32)
        kvar[:, 0:WINDOW, :] = jnp.zeros((2 * N_KV_HEADS, WINDOW, LANES), BF16)
        vtvar[:, :, 0:WINDOW] = jnp.zeros((N_KV_HEADS, HEAD_DIM, WINDOW), BF16)
        ucar[...] = jnp.zeros(ucar.shape, F32)
        kmask = jnp.where(key_row < WINDOW + HEAD_PAD, NEG_INF, 0.0)
        xh = jnp.concatenate([jnp.zeros((HEAD_PAD, D_MODEL), F32), xh_ref[...]], axis=0)
        _prompt_rows(xh, WINDOW, kmask, lambda: None, *shared)
        yh_ref[...] = _layer_norm(z_scr[0:WINDOW, :], g_ref[...], b_ref[...])[HEAD_PAD:, :]
        khead[...] = kvar[:, 0:WINDOW, :]
        vhead[...] = vtvar[:, :, 0:WINDOW]
        uhead[...] = ucar[...]

    def norm_previous_tile():
        y_ref[...] = _layer_norm(z_scr[...], g_ref[...], b_ref[...])

    @pl.when(t < last)
    def _():
        @pl.when(j == 0)
        def _():
            kvar[:, 0:WINDOW, :] = khead[...]
            vtvar[:, :, 0:WINDOW] = vhead[...]
            ucar[...] = uhead[...]

        kmask = jnp.where(key_row < HEAD_PAD, jnp.where(j == 0, NEG_INF, 0.0), 0.0)
        kf, vf = _prompt_rows(x_ref[...], tm, kmask, norm_previous_tile, *shared)

        @pl.when(j == n_tiles - 1)
        def _():
            kl_ref[...] = kf[tm - WINDOW:tm, :]
            vl_ref[...] = vf[tm - WINDOW:tm, :]
            cs_ref[...] = ucar[...]

    @pl.when(t == last)
    def _():
        norm_previous_tile()


def _resident(shape, index_map):
    return pl.BlockSpec(shape, index_map, pipeline_mode=pl.Buffered(1))


def _prompt_layer(layer, x, xh, w_in_b, w_out_b, w_in, w_out, conv_w, sinks, ln_g, ln_b, alpha):
    batch, seq, d = x.shape
    depth = w_in.shape[0]
    tm = PROMPT_TILE
    n_tiles = seq // tm
    total = batch * n_tiles
    const2 = lambda t: (0, 0)
    this_layer = lambda t: (layer, 0, 0)

    def tile_block(t):
        t = jnp.minimum(t, total - 1)
        return (t // n_tiles, t % n_tiles, 0)

    def prev_tile_block(t):
        return tile_block(jnp.maximum(t - 1, 0))

    per_batch = lambda t: (jnp.minimum(t, total - 1) // n_tiles, 0, 0)
    static = dict(layer=layer, tm=tm, n_tiles=n_tiles, alpha=alpha)
    operands = [sinks, x, xh, w_in_b, w_out_b, conv_w, ln_g, ln_b]
    in_specs = [
        pl.BlockSpec(memory_space=pltpu.SMEM),
        pl.BlockSpec((None, tm, d), tile_block),
        _resident((N_META, d), const2),
        _resident((d, PROJ_DIM), const2),
        _resident((ATT_DIM + CONV_DIM, d), const2),
        pl.BlockSpec((None, 3, CONV_DIM), this_layer),
        pl.BlockSpec((None, 1, d), this_layer),
        pl.BlockSpec((None, 1, d), this_layer),
    ]
    out_specs = [
        pl.BlockSpec((None, tm, d), prev_tile_block),
        pl.BlockSpec((N_META, d), const2),
        pl.BlockSpec((None, WINDOW, KV_DIM), per_batch),
        pl.BlockSpec((None, WINDOW, KV_DIM), per_batch),
        pl.BlockSpec((None, 8, CONV_DIM), per_batch),
    ]
    out_shape = [
        jax.ShapeDtypeStruct((batch, seq, d), F32),
        jax.ShapeDtypeStruct((N_META, d), F32),
        jax.ShapeDtypeStruct((batch, WINDOW, KV_DIM), F32),
        jax.ShapeDtypeStruct((batch, WINDOW, KV_DIM), F32),
        jax.ShapeDtypeStruct((batch, 8, CONV_DIM), F32),
    ]
    scratch_shapes = [
        pltpu.VMEM((2 * N_KV_HEADS, WINDOW + tm, LANES), BF16),
        pltpu.VMEM((N_KV_HEADS, HEAD_DIM, WINDOW + tm), BF16),
        pltpu.VMEM((2 * N_KV_HEADS, WINDOW, LANES), BF16),
        pltpu.VMEM((N_KV_HEADS, HEAD_DIM, WINDOW), BF16),
        pltpu.VMEM((8, CONV_DIM), F32),
        pltpu.VMEM((8, CONV_DIM), F32),
        pltpu.VMEM((2 * WINDOW, WINDOW), F32),
        pltpu.VMEM((tm, ATT_DIM + CONV_DIM), BF16),
        pltpu.VMEM((2 * N_KV_HEADS, 2 * WINDOW, 2 * WINDOW), BF16),
        pltpu.VMEM((tm, ATT_DIM), F32),
        pltpu.VMEM((tm, d), F32),
    ]
    kernel = _prompt_kernel
    if layer + 1 < depth:
        cast_rows = d // total
        assert cast_rows * total == d and cast_rows % 16 == 0
        kernel = _prompt_kernel_casting
        operands += [w_in, w_out]
        in_specs += [pl.BlockSpec(memory_space=pl.ANY)] * 2
        out_specs += [pl.BlockSpec(memory_space=pl.ANY)] * 2
        out_shape += [jax.ShapeDtypeStruct(w_in_b.shape, BF16),
                      jax.ShapeDtypeStruct(w_out_b.shape, BF16)]
        scratch_shapes += [
            pltpu.VMEM((cast_rows, PROJ_DIM), F32),
            pltpu.VMEM((cast_rows, d), F32),
            pltpu.VMEM((cast_rows, PROJ_DIM), BF16),
            pltpu.VMEM((cast_rows, d), BF16),
            pltpu.SemaphoreType.DMA((4,)),
        ]
    return pl.pallas_call(
        functools.partial(kernel, **static),
        grid=(total + 1,),
        in_specs=in_specs,
        out_specs=out_specs,
        out_shape=out_shape,
        scratch_shapes=scratch_shapes,
        compiler_params=pltpu.CompilerParams(
            dimension_semantics=("arbitrary",),
            vmem_limit_bytes=VMEM_LIMIT,
        ),
        name="prompt_layer",
    )(*operands)


SEQ_PER_GROUP = 2
GROUP_ROWS = SEQ_PER_GROUP * SAMPLE_ROWS
GROUPS_PER_STEP = 4
SEQ_PER_STEP = SEQ_PER_GROUP * GROUPS_PER_STEP
N_NEW = 4
TOK0 = SAMPLE_ROWS - N_NEW
LOG_SAMPLE_ROWS = 3
LOG_GROUP_ROWS = 4
LOG_HEAD_DIM = 6
assert (1 << LOG_SAMPLE_ROWS, 1 << LOG_GROUP_ROWS, 1 << LOG_HEAD_DIM) == (
    SAMPLE_ROWS, GROUP_ROWS, HEAD_DIM)


def _sample_kernel(sinks_ref, x_hbm, cw_ref, g_ref, b_ref, ck_ref, cv_ref, st_ref, *refs,
                   depth, n_steps, n_rows, alpha):
    win_hbm, wout_hbm = refs[:depth], refs[depth:2 * depth]
    (y_hbm, kb_ref, vb_ref, u_ref, win_ref, wout_ref, wsem, iosem, xcur,
     qe, knew, vnew, gate_a, bias_ref, sinkcol, mix_ref) = refs[2 * depth:]
    layer = pl.program_id(0) // n_steps
    grp = pl.program_id(0) % n_steps
    n_q = N_HEADS * GROUP_ROWS
    lane_blk = lax.broadcasted_iota(jnp.int32, (n_rows, KV_DIM), 1) >> LOG_HEAD_DIM

    col_groups = COL_GROUPS
    n_win = len(col_groups) - 1

    def win_copy(l, i):
        cols = pl.ds(col_groups[i], col_groups[i + 1] - col_groups[i])
        return pltpu.make_async_copy(win_hbm[l].at[:, cols], win_ref.at[:, cols], wsem.at[i])

    def wout_copy(l):
        return pltpu.make_async_copy(wout_hbm[l], wout_ref, wsem.at[n_win])

    def start_for_layer(next_layer, start):
        for l in range(1, depth):
            @pl.when(next_layer == l)
            def _():
                start(l)

    def proj(xb, i, width):
        win_copy(0, i).wait()
        return _proj(xb, win_ref, col_groups[i], width)

    @pl.when(pl.program_id(0) == 0)
    def _():
        x_in = pltpu.make_async_copy(x_hbm, xcur, iosem.at[0])
        x_in.start()
        for i in range(n_win):
            win_copy(0, i).start()
        wout_copy(0).start()
        x_in.wait()

    @pl.when(grp == 0)
    def _():
        x = xcur[...]
        xb = x.astype(BF16)
        hq = proj(xb, 0, C_GA)
        q = hq[:, C_Q:C_Q + ATT_DIM] * Q_SCALE
        knew[...] = hq[:, C_K:C_K + KV_DIM]
        vnew[...] = hq[:, C_V:C_V + KV_DIM]
        for hd in range(N_HEADS):
            h, g = divmod(hd, GQA_GROUP)
            slab = q[:, h * KV_DIM:(h + 1) * KV_DIM]
            moved = pltpu.roll(slab, ((h - g) % GQA_GROUP) * HEAD_DIM, axis=1)
            qe[hd] = jnp.where(lane_blk == h, moved, 0.0).astype(BF16)
        gate_a[...] = _silu(proj(xb, 1, ATT_DIM))
        start_for_layer(layer + 1, lambda l: [win_copy(l, i).start() for i in (0, 1)])

        bg = proj(xb, 2, CONV_DIM)
        u = proj(xb, 3, CONV_DIM) * proj(xb, 4, CONV_DIM)
        start_for_layer(layer + 1, lambda l: [win_copy(l, i).start() for i in (2, 3, 4)])
        r8 = lax.broadcasted_iota(jnp.int32, (n_rows, CONV_DIM), 0) & (SAMPLE_ROWS - 1)
        is_state = (r8 >= TOK0 - 2) & (r8 < TOK0)
        u = jnp.where(is_state, st_ref[...], u)
        u_ref[...] = u
        cy = (cw_ref[0:1, :] * pltpu.roll(u, 2, axis=0) + cw_ref[1:2, :] * pltpu.roll(u, 1, axis=0)
              + cw_ref[2:3, :] * u)
        gate_c = _silu(proj(xb, 5, CONV_DIM))
        mix_ref[:, ATT_DIM:] = ((bg * cy) * gate_c).astype(BF16)

        qrow = lax.broadcasted_iota(jnp.int32, (n_q, 2 * WINDOW), 0)
        key = lax.broadcasted_iota(jnp.int32, (n_q, 2 * WINDOW), 1)
        q_tok = jnp.maximum((qrow & (SAMPLE_ROWS - 1)) - TOK0, 0)
        q_seq = (qrow >> LOG_SAMPLE_ROWS) & (SEQ_PER_GROUP - 1)
        new = key - WINDOW
        k_tok = (new & (SAMPLE_ROWS - 1)) - TOK0
        k_seq = new >> LOG_SAMPLE_ROWS
        cached = key < WINDOW
        dist = jnp.where(cached, WINDOW + q_tok - key, q_tok - k_tok)
        ok_new = (new >= 0) & (new < GROUP_ROWS) & (k_seq == q_seq) & (k_tok >= 0)
        visible = (dist >= 0) & (dist < WINDOW) & (cached | ok_new)
        slope = jnp.zeros((n_q, 2 * WINDOW), F32)
        sink = jnp.zeros((n_q, LANES), F32)
        srow = lax.broadcasted_iota(jnp.int32, (n_q, LANES), 0)
        for hd in range(N_HEADS):
            slope = jnp.where((qrow >> LOG_GROUP_ROWS) == hd, SLOPES[hd], slope)
            sink = jnp.where((srow >> LOG_GROUP_ROWS) == hd, sinks_ref[layer, hd], sink)
        bias_ref[...] = jnp.where(visible, -(slope * dist.astype(F32)), NEG_INF)
        sinkcol[...] = sink
        start_for_layer(layer + 1, lambda l: win_copy(l, n_win - 1).start())

    q_seq = ((lax.broadcasted_iota(jnp.int32, (n_q, KV_DIM), 0) >> LOG_SAMPLE_ROWS)
             & (SEQ_PER_GROUP - 1))
    newest = lax.broadcasted_iota(jnp.int32, (KV_DIM, WINDOW), 1) >= WINDOW - N_NEW
    blk16 = lax.broadcasted_iota(jnp.int32, (GROUP_ROWS, KV_DIM), 1) >> LOG_HEAD_DIM
    pad_rows = jnp.zeros((WINDOW - GROUP_ROWS, KV_DIM), F32)
    sink = sinkcol[:, 0:1]

    probs, v_new_bs, row0s = [], [], []
    for gi in range(GROUPS_PER_STEP):
        g0 = pl.multiple_of((grp * GROUPS_PER_STEP + gi) * GROUP_ROWS, GROUP_ROWS)
        row0s.append(g0)
        w_g = jnp.concatenate([qe[hd, pl.ds(g0, GROUP_ROWS), :] for hd in range(N_HEADS)], axis=0)
        k_new = jnp.concatenate([knew[pl.ds(g0, GROUP_ROWS), :], pad_rows], axis=0)
        v_new = jnp.concatenate([vnew[pl.ds(g0, GROUP_ROWS), :], pad_rows], axis=0)
        v_new_bs.append(v_new.astype(BF16))
        k_new_t = k_new.T
        v_new_t = v_new.T
        sc_cached = None
        for s in range(SEQ_PER_GROUP):
            n = gi * SEQ_PER_GROUP + s
            ck_t = ck_ref[n]
            sc_s = jnp.dot(w_g, ck_t.astype(BF16), preferred_element_type=F32)
            sc_cached = sc_s if s == 0 else jnp.where(q_seq[:, :WINDOW] == s, sc_s, sc_cached)
            to_tail = WINDOW - N_NEW - (s * SAMPLE_ROWS + TOK0)
            for cache_t, new_t, out_ref in ((ck_t, k_new_t, kb_ref), (cv_ref[n], v_new_t, vb_ref)):
                out_ref[n] = jnp.where(newest, pltpu.roll(new_t, to_tail, axis=1),
                                       pltpu.roll(cache_t, WINDOW - N_NEW, axis=1))
        sc_new = lax.dot_general(w_g, k_new.astype(BF16), _NT, preferred_element_type=F32)
        sc = jnp.concatenate([sc_cached, sc_new], axis=1)
        probs.append(_softmax_rows(sc + bias_ref[...], sink).astype(BF16))

    for gi in range(GROUPS_PER_STEP):
        p = probs[gi]
        o_grp = jnp.dot(p[:, WINDOW:], v_new_bs[gi], preferred_element_type=F32)
        o_cached = None
        for s in range(SEQ_PER_GROUP):
            cv_t = cv_ref[gi * SEQ_PER_GROUP + s]
            o_s = lax.dot_general(p[:, :WINDOW], cv_t.astype(BF16), _NT,
                                  preferred_element_type=F32)
            o_cached = o_s if s == 0 else jnp.where(q_seq == s, o_s, o_cached)
        o_grp = o_grp + o_cached
        for h in range(N_KV_HEADS):
            slab = jnp.zeros((GROUP_ROWS, KV_DIM), F32)
            for g in range(GQA_GROUP):
                hd = h * GQA_GROUP + g
                piece = jnp.where(blk16 == h, o_grp[hd * GROUP_ROWS:(hd + 1) * GROUP_ROWS, :], 0.0)
                slab = slab + pltpu.roll(piece, ((g - h) % GQA_GROUP) * HEAD_DIM, axis=1)
            rows = pl.ds(row0s[gi], GROUP_ROWS)
            cols = slice(h * KV_DIM, (h + 1) * KV_DIM)
            mix_ref[rows, cols] = (slab * gate_a[rows, cols]).astype(BF16)

    @pl.when(grp == n_steps - 1)
    def _():
        wout_copy(0).wait()
        out = jnp.dot(mix_ref[...], wout_ref[...], preferred_element_type=F32)
        xcur[...] = _layer_norm(alpha * xcur[...] + out, g_ref[...], b_ref[...])
        start_for_layer(layer + 1, lambda l: wout_copy(l).start())

        @pl.when(layer == depth - 1)
        def _():
            y_out = pltpu.make_async_copy(xcur, y_hbm, iosem.at[1])
            y_out.start()
            y_out.wait()


def _sample_layers(x8, st_all, ck_all, cv_all, w_in_bs, w_out_bs, conv_w, sinks, ln_g, ln_b, alpha):
    n_rows, d = x8.shape
    depth, n_seq = ck_all.shape[:2]
    n_steps = n_seq // SEQ_PER_STEP
    n_q = N_HEADS * GROUP_ROWS
    by_layer = lambda g: (g // n_steps, 0, 0)
    cache_spec = pl.BlockSpec((None, SEQ_PER_STEP, KV_DIM, WINDOW),
                              lambda g: (g // n_steps, g % n_steps, 0, 0))
    hbm = pl.BlockSpec(memory_space=pl.ANY)
    kernel = functools.partial(_sample_kernel, depth=depth, n_steps=n_steps, n_rows=n_rows,
                               alpha=alpha)
    return pl.pallas_call(
        kernel,
        grid=(depth * n_steps,),
        in_specs=[
            pl.BlockSpec(memory_space=pltpu.SMEM),
            hbm,
            pl.BlockSpec((None, 3, CONV_DIM), by_layer),
            pl.BlockSpec((None, 1, d), by_layer),
            pl.BlockSpec((None, 1, d), by_layer),
            cache_spec,
            cache_spec,
            pl.BlockSpec((None, n_rows, CONV_DIM), by_layer),
        ] + [hbm] * (2 * depth),
        out_specs=[
            hbm,
            cache_spec,
            cache_spec,
            pl.BlockSpec((None, n_rows, CONV_DIM), by_layer),
        ],
        out_shape=[
            jax.ShapeDtypeStruct((n_rows, d), F32),
            jax.ShapeDtypeStruct(ck_all.shape, F32),
            jax.ShapeDtypeStruct(cv_all.shape, F32),
            jax.ShapeDtypeStruct((depth, n_rows, CONV_DIM), F32),
        ],
        scratch_shapes=[
            pltpu.VMEM((d, PROJ_DIM), BF16),
            pltpu.VMEM((ATT_DIM + CONV_DIM, d), BF16),
            pltpu.SemaphoreType.DMA((len(COL_GROUPS),)),
            pltpu.SemaphoreType.DMA((2,)),
            pltpu.VMEM((n_rows, d), F32),
            pltpu.VMEM((N_HEADS, n_rows, KV_DIM), BF16),
            pltpu.VMEM((n_rows, KV_DIM), F32),
            pltpu.VMEM((n_rows, KV_DIM), F32),
            pltpu.VMEM((n_rows, ATT_DIM), F32),
            pltpu.VMEM((n_q, 2 * WINDOW), F32),
            pltpu.VMEM((n_q, LANES), F32),
            pltpu.VMEM((n_rows, ATT_DIM + CONV_DIM), BF16),
        ],
        compiler_params=pltpu.CompilerParams(
            dimension_semantics=("arbitrary",),
            vmem_limit_bytes=DECODE_VMEM_LIMIT,
        ),
        name="sample_layers",
    )(sinks, x8, conv_w, ln_g, ln_b, ck_all, cv_all, st_all, *w_in_bs, *w_out_bs)


def kernel(x_prompt, x_sample, cache_k, cache_v, state_conv, meta_tokens,
           w_in, conv_w, sinks, w_out, ln_g, ln_b):
    depth = w_in.shape[0]
    alpha = float((2 * depth) ** 0.25)
    batch, seq, d = x_prompt.shape
    n_seq, n_tok = x_sample.shape[:2]
    assert d == D_MODEL and seq % PROMPT_TILE == 0 and n_tok == SAMPLE_ROWS - TOK0
    assert meta_tokens.shape[0] == N_META and n_seq % SEQ_PER_STEP == 0
    assert cache_k.shape[2] == WINDOW and state_conv.shape[2] == 2

    w_in_b = w_in[0].astype(BF16)
    w_out_b = w_out[0].astype(BF16)
    ln_g3 = ln_g.reshape(depth, 1, d)
    ln_b3 = ln_b.reshape(depth, 1, d)
    xp = x_prompt
    xh = meta_tokens.astype(F32)
    xs = jnp.pad(x_sample, ((0, 0), (TOK0, 0), (0, 0))).reshape(n_seq * SAMPLE_ROWS, d)
    ck_all = jnp.transpose(cache_k, (0, 1, 3, 4, 2)).reshape(depth, n_seq, KV_DIM, WINDOW)
    cv_all = jnp.transpose(cache_v, (0, 1, 3, 4, 2)).reshape(depth, n_seq, KV_DIM, WINDOW)
    st_all = jnp.pad(state_conv, ((0, 0), (0, 0), (TOK0 - 2, SAMPLE_ROWS - TOK0), (0, 0)))
    st_all = st_all.reshape(depth, n_seq * SAMPLE_ROWS, CONV_DIM)

    kp, vp, cp = [], [], []
    w_in_bs, w_out_bs = [w_in_b], [w_out_b]
    for l in range(depth):
        xp, xh, k_last, v_last, c_last, *next_weights = _prompt_layer(
            l, xp, xh, w_in_bs[l], w_out_bs[l], w_in, w_out, conv_w, sinks, ln_g3, ln_b3, alpha)
        kp.append(k_last.reshape(batch, WINDOW, N_KV_HEADS, HEAD_DIM))
        vp.append(v_last.reshape(batch, WINDOW, N_KV_HEADS, HEAD_DIM))
        cp.append(c_last[:, 6:8, :])
        if next_weights:
            w_in_bs.append(next_weights[0])
            w_out_bs.append(next_weights[1])
    xs, kb_all, vb_all, u_all = _sample_layers(
        xs, st_all, ck_all, cv_all, w_in_bs, w_out_bs, conv_w, sinks, ln_g3, ln_b3, alpha)
    c_sample = u_all.reshape(depth, n_seq, SAMPLE_ROWS, CONV_DIM)[:, :, SAMPLE_ROWS - 2:, :]
    y_sample = xs.reshape(n_seq, SAMPLE_ROWS, d)[:, TOK0:, :]
    kv_shape = (depth, n_seq, N_KV_HEADS, HEAD_DIM, WINDOW)
    k_sample = jnp.transpose(kb_all.reshape(kv_shape), (0, 1, 4, 2, 3))
    v_sample = jnp.transpose(vb_all.reshape(kv_shape), (0, 1, 4, 2, 3))
    return (xp, y_sample, jnp.stack(kp), jnp.stack(vp), jnp.stack(cp),
            k_sample, v_sample, c_sample)
```

```python
import functools

import numpy as np
import jax
import jax.numpy as jnp
from jax import lax
from jax.experimental import pallas as pl
from jax.experimental.pallas import tpu as pltpu

F32 = jnp.float32
BF16 = jnp.bfloat16

D_MODEL = 2048
N_META = 16
ATT_DIM = 1024
CONV_DIM = 1024
HEAD_DIM = 64
N_HEADS = 16
N_KV_HEADS = 4
GQA_GROUP = N_HEADS // N_KV_HEADS
KV_DIM = N_KV_HEADS * HEAD_DIM
WINDOW = 128
PROJ_DIM = 2 * ATT_DIM + 2 * KV_DIM + 4 * CONV_DIM
LN_EPS = 1e-5
NEG_INF = -1e30
Q_SCALE = HEAD_DIM ** -0.5

C_Q = 0
C_K = ATT_DIM
C_V = C_K + KV_DIM
C_GA = C_V + KV_DIM
C_B = C_GA + ATT_DIM
C_C = C_B + CONV_DIM
C_H = C_C + CONV_DIM
C_GC = C_H + CONV_DIM
COL_GROUPS = (C_Q, C_GA, C_B, C_C, C_H, C_GC, PROJ_DIM)

LANES = 128
HEAD_PAD = WINDOW - N_META
PROMPT_TILE = 256
FILL_CHUNK = 256
OUT_CHUNK = 512
SAMPLE_ROWS = 8
VMEM_LIMIT = 58 * 1024 * 1024
DECODE_VMEM_LIMIT = 62 * 1024 * 1024

SLOPES = [float(np.float32(2.0 ** (-8.0 * (h + 1) / N_HEADS))) for h in range(N_HEADS)]

_NT = (((1,), (1,)), ((), ()))


def _silu(g):
    return g * (1.0 / (1.0 + jnp.exp(-g)))


def _softmax_rows(s, sink):
    m = jnp.maximum(jnp.max(s, axis=1, keepdims=True), sink)
    p = jnp.exp(s - m)
    denom = jnp.sum(p, axis=1, keepdims=True) + jnp.exp(sink - m)
    return p * (1.0 / denom)


def _softmax_cols(s, sink):
    m = jnp.maximum(jnp.max(s, axis=0, keepdims=True), sink)
    p = jnp.exp(s - m)
    denom = jnp.sum(p, axis=0, keepdims=True) + jnp.exp(sink - m)
    return p * (1.0 / denom)


def _layer_norm(z, g, b):
    mu = jnp.mean(z, axis=1, keepdims=True)
    zc = z - mu
    var = jnp.mean(zc * zc, axis=1, keepdims=True)
    return zc * lax.rsqrt(var + LN_EPS) * g + b


def _proj(xb, w_ref, c0, width):
    return jnp.dot(xb, w_ref[:, c0:c0 + width], preferred_element_type=F32)


def _store_k_variants(src, dst, r0, rows):
    low = lax.broadcasted_iota(jnp.int32, (rows, LANES), 1) < HEAD_DIM
    for cc in range(KV_DIM // LANES):
        col = src[:, cc * LANES:(cc + 1) * LANES]
        swapped = pltpu.roll(col, HEAD_DIM, axis=1)
        h_even, h_odd = 2 * cc, 2 * cc + 1
        dst[2 * h_even + 0, r0:r0 + rows, :] = jnp.where(low, col, 0.0).astype(BF16)
        dst[2 * h_even + 1, r0:r0 + rows, :] = jnp.where(low, 0.0, swapped).astype(BF16)
        dst[2 * h_odd + 0, r0:r0 + rows, :] = jnp.where(low, swapped, 0.0).astype(BF16)
        dst[2 * h_odd + 1, r0:r0 + rows, :] = jnp.where(low, 0.0, col).astype(BF16)


def _conv_chunk_pieces(xb, win_ref, cw_ref, ucar, mix_ref, rows, lo, width):
    cols = slice(lo, lo + width)
    got = {}

    def project(name, c0):
        got[name] = _proj(xb, win_ref, c0 + lo, width)

    def finish():
        u = got["c"] * got["h"]
        row = lax.broadcasted_iota(jnp.int32, (rows, width), 0)
        prev1 = ucar[7:8, cols]
        prev2 = ucar[6:7, cols]
        u1 = jnp.where(row == 0, prev1, pltpu.roll(u, 1, axis=0))
        u2 = jnp.where(row == 0, prev2, jnp.where(row == 1, prev1, pltpu.roll(u, 2, axis=0)))
        cy = cw_ref[0:1, cols] * u2 + cw_ref[1:2, cols] * u1 + cw_ref[2:3, cols] * u
        ucar[:, cols] = u[rows - 8:rows, :]
        gate_c = _silu(_proj(xb, win_ref, C_GC + lo, width))
        mix_ref[0:rows, ATT_DIM + lo:ATT_DIM + lo + width] = (
            (got["b"] * cy) * gate_c).astype(BF16)

    return [functools.partial(project, "b", C_B), functools.partial(project, "c", C_C),
            functools.partial(project, "h", C_H), finish]


def _store_v_transposed(src, dst, c0, rows):
    vt = src.T
    for h in range(N_KV_HEADS):
        dst[h, :, c0:c0 + rows] = vt[h * HEAD_DIM:(h + 1) * HEAD_DIM, :].astype(BF16)


def _scores_group(layer, qb, kmask, sinks_ref, kvar, bias_ref, p_scr, r0, h):
    q4 = jnp.concatenate(
        [qb[r0:r0 + WINDOW, c * LANES:(c + 1) * LANES] for c in (2 * h, 2 * h + 1)], axis=0)
    for par in range(2):
        keys = kvar[2 * h + par, r0:r0 + 2 * WINDOW, :]
        s2 = lax.dot_general(keys, q4, _NT, preferred_element_type=F32)
        for half in range(2):
            hd = GQA_GROUP * h + 2 * half + par
            lanes = slice(half * LANES, (half + 1) * LANES)
            s = s2[:, lanes] + SLOPES[hd] * bias_ref[...]
            if kmask is not None:
                s = s + kmask
            p_scr[2 * h + par, :, lanes] = _softmax_cols(s, sinks_ref[layer, hd]).astype(BF16)


def _values_group(gate_ref, vtvar, p_scr, mix_ref, r0, h):
    vals_t = vtvar[h, :, r0:r0 + 2 * WINDOW]
    o_t = jnp.concatenate([jnp.dot(vals_t, p_scr[2 * h + par], preferred_element_type=F32)
                           for par in range(2)], axis=0)
    for half in range(2):
        lanes = slice((2 * h + half) * LANES, (2 * h + half + 1) * LANES)
        o = o_t[:, half * LANES:(half + 1) * LANES].T
        mix_ref[r0:r0 + WINDOW, lanes] = (o * gate_ref[r0:r0 + WINDOW, lanes]).astype(BF16)


def _gate_chunk(xb, win_ref, gate_ref, rows, lo, width):
    gate_ref[0:rows, lo:lo + width] = _silu(_proj(xb, win_ref, C_GA + lo, width))


def _prompt_rows(x, rows, kmask0, after_qkv, arrived, layer, sinks_ref, win_ref, wout_ref, cw_ref,
                 kvar, vtvar, ucar, bias_ref, mix_ref, p_scr, gate_ref, z_ref, alpha):
    if arrived is None:
        arrived = lambda i: None
        in_arrival_order = False
    else:
        in_arrival_order = True
    xb = x.astype(BF16)
    arrived(0)
    hq = _proj(xb, win_ref, 0, C_GA)
    after_qkv()
    qb = (hq[:, C_Q:C_Q + ATT_DIM] * Q_SCALE).astype(BF16)
    kf = hq[:, C_K:C_K + KV_DIM]
    vf = hq[:, C_V:C_V + KV_DIM]
    _store_k_variants(kf, kvar, WINDOW, rows)
    _store_v_transposed(vf, vtvar, WINDOW, rows)

    n_blocks = rows // WINDOW
    gates = [(1, functools.partial(_gate_chunk, xb, win_ref, gate_ref, rows, lo, FILL_CHUNK))
             for lo in range(0, ATT_DIM, FILL_CHUNK)]
    convs = [list(zip((2, 3, 4, 5),
                      _conv_chunk_pieces(xb, win_ref, cw_ref, ucar, mix_ref, rows, lo, FILL_CHUNK)))
             for lo in range(0, CONV_DIM, FILL_CHUNK)]
    if in_arrival_order:
        fillers = gates + [conv[k] for k in range(4) for conv in convs]
    else:
        fillers = gates[:2] + convs[0] + gates[2:] + [p for conv in convs[1:] for p in conv]
    n_units = n_blocks * N_KV_HEADS
    cuts = [len(fillers) * u // n_units for u in range(n_units + 1)]
    assert cuts[N_KV_HEADS] >= len(gates) + len(convs[0])
    for blk in range(n_blocks):
        r0 = blk * WINDOW
        for h in range(N_KV_HEADS):
            unit = blk * N_KV_HEADS + h
            for group, filler in fillers[cuts[unit]:cuts[unit + 1]]:
                arrived(group)
                filler()
            _scores_group(layer, qb, kmask0 if blk == 0 else None, sinks_ref, kvar, bias_ref,
                          p_scr, r0, h)
        for h in range(N_KV_HEADS):
            _values_group(gate_ref, vtvar, p_scr, mix_ref, r0, h)
        if blk == 0:
            for i in range(2 * N_KV_HEADS):
                kvar[i, 0:WINDOW, :] = kvar[i, rows:rows + WINDOW, :]
            for i in range(N_KV_HEADS):
                vtvar[i, :, 0:WINDOW] = vtvar[i, :, rows:rows + WINDOW]

    arrived(len(COL_GROUPS) - 1)
    mix = mix_ref[0:rows, :]
    for c0 in range(0, D_MODEL, OUT_CHUNK):
        cols = slice(c0, c0 + OUT_CHUNK)
        out = jnp.dot(mix, wout_ref[:, cols], preferred_element_type=F32)
        z_ref[0:rows, cols] = alpha * x[:, cols] + out
    return kf, vf


def _cast_next_weights(t, last, next_layer, win_f32, wout_f32, win_next, wout_next,
                       stage_in, stage_out, sem):
    rows = stage_in[0].shape[0]
    srcs = (win_f32, wout_f32)
    dsts = (win_next, wout_next)

    def read(k, i):
        return pltpu.make_async_copy(srcs[i].at[next_layer, pl.ds(k * rows, rows), :],
                                     stage_in[i], sem.at[i])

    def write(k, i):
        return pltpu.make_async_copy(stage_out[i], dsts[i].at[pl.ds(k * rows, rows), :],
                                     sem.at[2 + i])

    @pl.when(t >= 1)
    def _():
        for i in range(2):
            read(t - 1, i).wait()

        @pl.when(t >= 2)
        def _():
            for i in range(2):
                write(t - 2, i).wait()

        for i in range(2):
            stage_out[i][...] = stage_in[i][...].astype(BF16)
            write(t - 1, i).start()

    @pl.when(t < last)
    def _():
        for i in range(2):
            read(t, i).start()

    @pl.when(t == last)
    def _():
        for i in range(2):
            write(t - 1, i).wait()


def _prompt_kernel_casting(sinks_ref, x_ref, xh_ref, win_hbm, wout_hbm, cw_ref, g_ref, b_ref,
                           win_f32, wout_f32,
                           y_ref, yh_ref, kl_ref, vl_ref, cs_ref, win_next, wout_next,
                           *scratch, layer, **static):
    cast_in_a, cast_in_b, cast_out_a, cast_out_b, cast_sem = scratch[-5:]
    _cast_next_weights(pl.program_id(0), pl.num_programs(0) - 1, layer + 1, win_f32, wout_f32,
                       win_next, wout_next, (cast_in_a, cast_in_b), (cast_out_a, cast_out_b),
                       cast_sem)
    _prompt_kernel(sinks_ref, x_ref, xh_ref, win_hbm, wout_hbm, cw_ref, g_ref, b_ref,
                   y_ref, yh_ref, kl_ref, vl_ref, cs_ref, *scratch[:-5], layer=layer, **static)


def _prompt_kernel(sinks_ref, x_ref, xh_ref, win_hbm, wout_hbm, cw_ref, g_ref, b_ref,
                   y_ref, yh_ref, kl_ref, vl_ref, cs_ref,
                   kvar, vtvar, khead, vhead, ucar, uhead, bias_ref, mix_ref, p_scr, gate_ref,
                   z_scr, win_ref, wout_ref, wsem, *, layer, tm, n_tiles, alpha):
    t = pl.program_id(0)
    last = pl.num_programs(0) - 1
    j = t % n_tiles
    shared = (layer, sinks_ref, win_ref, wout_ref, cw_ref, kvar, vtvar, ucar, bias_ref, mix_ref,
              p_scr, gate_ref, z_scr, alpha)
    key_row = lax.broadcasted_iota(jnp.int32, (2 * WINDOW, WINDOW), 0)
    n_win = len(COL_GROUPS) - 1

    def weight_copy(i):
        if i == n_win:
            return pltpu.make_async_copy(wout_hbm, wout_ref, wsem.at[i])
        cols = pl.ds(COL_GROUPS[i], COL_GROUPS[i + 1] - COL_GROUPS[i])
        return pltpu.make_async_copy(win_hbm.at[:, cols], win_ref.at[:, cols], wsem.at[i])

    @pl.when(t == 0)
    def _():
        for i in range(n_win + 1):
            weight_copy(i).start()
        waited = set()

        def arrived(i):
            if i not in waited:
                waited.add(i)
                weight_copy(i).wait()

        qi = lax.broadcasted_iota(jnp.int32, (2 * WINDOW, WINDOW), 1)
        dist = WINDOW + qi - key_row
        visible = (dist >= 0) & (dist < WINDOW)
        bias_ref[...] = jnp.where(visible, -dist.astype(F32), NEG_INF)
        z_scr[...] = jnp.zeros(z_scr.shape, F32)
        kvar[:, 0:WINDOW, :] = jnp.zeros((2 * N_KV_HEADS, WINDOW, LANES), BF16)
        vtvar[:, :, 0:WINDOW] = jnp.zeros((N_KV_HEADS, HEAD_DIM, WINDOW), BF16)
        ucar[...] = jnp.zeros(ucar.shape, F32)
        kmask = jnp.where(key_row < WINDOW + HEAD_PAD, NEG_INF, 0.0)
        xh = jnp.concatenate([jnp.zeros((HEAD_PAD, D_MODEL), F32), xh_ref[...]], axis=0)
        _prompt_rows(xh, WINDOW, kmask, lambda: None, arrived, *shared)
        assert len(waited) == n_win + 1
        yh_ref[...] = _layer_norm(z_scr[0:WINDOW, :], g_ref[...], b_ref[...])[HEAD_PAD:, :]
        khead[...] = kvar[:, 0:WINDOW, :]
        vhead[...] = vtvar[:, :, 0:WINDOW]
        uhead[...] = ucar[...]

    def norm_previous_tile():
        y_ref[...] = _layer_norm(z_scr[...], g_ref[...], b_ref[...])

    @pl.when(t < last)
    def _():
        @pl.when(j == 0)
        def _():
            kvar[:, 0:WINDOW, :] = khead[...]
            vtvar[:, :, 0:WINDOW] = vhead[...]
            ucar[...] = uhead[...]

        kmask = jnp.where(key_row < HEAD_PAD, jnp.where(j == 0, NEG_INF, 0.0), 0.0)
        kf, vf = _prompt_rows(x_ref[...], tm, kmask, norm_previous_tile, None, *shared)

        @pl.when(j == n_tiles - 1)
        def _():
            kl_ref[...] = kf[tm - WINDOW:tm, :]
            vl_ref[...] = vf[tm - WINDOW:tm, :]
            cs_ref[...] = ucar[...]

    @pl.when(t == last)
    def _():
        norm_previous_tile()


def _resident(shape, index_map):
    return pl.BlockSpec(shape, index_map, pipeline_mode=pl.Buffered(1))


def _prompt_layer(layer, x, xh, w_in_b, w_out_b, w_in, w_out, conv_w, sinks, ln_g, ln_b, alpha):
    batch, seq, d = x.shape
    depth = w_in.shape[0]
    tm = PROMPT_TILE
    n_tiles = seq // tm
    total = batch * n_tiles
    const2 = lambda t: (0, 0)
    this_layer = lambda t: (layer, 0, 0)

    def tile_block(t):
        t = jnp.minimum(t, total - 1)
        return (t // n_tiles, t % n_tiles, 0)

    def prev_tile_block(t):
        return tile_block(jnp.maximum(t - 1, 0))

    per_batch = lambda t: (jnp.minimum(t, total - 1) // n_tiles, 0, 0)
    static = dict(layer=layer, tm=tm, n_tiles=n_tiles, alpha=alpha)
    operands = [sinks, x, xh, w_in_b, w_out_b, conv_w, ln_g, ln_b]
    in_specs = [
        pl.BlockSpec(memory_space=pltpu.SMEM),
        pl.BlockSpec((None, tm, d), tile_block),
        _resident((N_META, d), const2),
        pl.BlockSpec(memory_space=pl.ANY),
        pl.BlockSpec(memory_space=pl.ANY),
        pl.BlockSpec((None, 3, CONV_DIM), this_layer),
        pl.BlockSpec((None, 1, d), this_layer),
        pl.BlockSpec((None, 1, d), this_layer),
    ]
    out_specs = [
        pl.BlockSpec((None, tm, d), prev_tile_block),
        pl.BlockSpec((N_META, d), const2),
        pl.BlockSpec((None, WINDOW, KV_DIM), per_batch),
        pl.BlockSpec((None, WINDOW, KV_DIM), per_batch),
        pl.BlockSpec((None, 8, CONV_DIM), per_batch),
    ]
    out_shape = [
        jax.ShapeDtypeStruct((batch, seq, d), F32),
        jax.ShapeDtypeStruct((N_META, d), F32),
        jax.ShapeDtypeStruct((batch, WINDOW, KV_DIM), F32),
        jax.ShapeDtypeStruct((batch, WINDOW, KV_DIM), F32),
        jax.ShapeDtypeStruct((batch, 8, CONV_DIM), F32),
    ]
    scratch_shapes = [
        pltpu.VMEM((2 * N_KV_HEADS, WINDOW + tm, LANES), BF16),
        pltpu.VMEM((N_KV_HEADS, HEAD_DIM, WINDOW + tm), BF16),
        pltpu.VMEM((2 * N_KV_HEADS, WINDOW, LANES), BF16),
        pltpu.VMEM((N_KV_HEADS, HEAD_DIM, WINDOW), BF16),
        pltpu.VMEM((8, CONV_DIM), F32),
        pltpu.VMEM((8, CONV_DIM), F32),
        pltpu.VMEM((2 * WINDOW, WINDOW), F32),
        pltpu.VMEM((tm, ATT_DIM + CONV_DIM), BF16),
        pltpu.VMEM((2 * N_KV_HEADS, 2 * WINDOW, 2 * WINDOW), BF16),
        pltpu.VMEM((tm, ATT_DIM), F32),
        pltpu.VMEM((tm, d), F32),
        pltpu.VMEM((d, PROJ_DIM), BF16),
        pltpu.VMEM((ATT_DIM + CONV_DIM, d), BF16),
        pltpu.SemaphoreType.DMA((len(COL_GROUPS),)),
    ]
    kernel = _prompt_kernel
    if layer + 1 < depth:
        cast_rows = d // total
        assert cast_rows * total == d and cast_rows % 16 == 0
        kernel = _prompt_kernel_casting
        operands += [w_in, w_out]
        in_specs += [pl.BlockSpec(memory_space=pl.ANY)] * 2
        out_specs += [pl.BlockSpec(memory_space=pl.ANY)] * 2
        out_shape += [jax.ShapeDtypeStruct(w_in_b.shape, BF16),
                      jax.ShapeDtypeStruct(w_out_b.shape, BF16)]
        scratch_shapes += [
            pltpu.VMEM((cast_rows, PROJ_DIM), F32),
            pltpu.VMEM((cast_rows, d), F32),
            pltpu.VMEM((cast_rows, PROJ_DIM), BF16),
            pltpu.VMEM((cast_rows, d), BF16),
            pltpu.SemaphoreType.DMA((4,)),
        ]
    return pl.pallas_call(
        functools.partial(kernel, **static),
        grid=(total + 1,),
        in_specs=in_specs,
        out_specs=out_specs,
        out_shape=out_shape,
        scratch_shapes=scratch_shapes,
        compiler_params=pltpu.CompilerParams(
            dimension_semantics=("arbitrary",),
            vmem_limit_bytes=VMEM_LIMIT,
        ),
        name="prompt_layer",
    )(*operands)


SEQ_PER_GROUP = 2
GROUP_ROWS = SEQ_PER_GROUP * SAMPLE_ROWS
GROUPS_PER_STEP = 4
SEQ_PER_STEP = SEQ_PER_GROUP * GROUPS_PER_STEP
N_NEW = 4
TOK0 = SAMPLE_ROWS - N_NEW
LOG_SAMPLE_ROWS = 3
LOG_GROUP_ROWS = 4
LOG_HEAD_DIM = 6
assert (1 << LOG_SAMPLE_ROWS, 1 << LOG_GROUP_ROWS, 1 << LOG_HEAD_DIM) == (
    SAMPLE_ROWS, GROUP_ROWS, HEAD_DIM)


def _sample_kernel(sinks_ref, x_hbm, cw_ref, g_ref, b_ref, ck_ref, cv_ref, st_ref, *refs,
                   depth, n_steps, n_rows, alpha):
    win_hbm, wout_hbm = refs[:depth], refs[depth:2 * depth]
    (y_hbm, kb_ref, vb_ref, u_ref, win_ref, wout_ref, wsem, iosem, xcur,
     qe, knew, vnew, gate_a, bias_ref, sinkcol, mix_ref) = refs[2 * depth:]
    layer = pl.program_id(0) // n_steps
    grp = pl.program_id(0) % n_steps
    n_q = N_HEADS * GROUP_ROWS
    lane_blk = lax.broadcasted_iota(jnp.int32, (n_rows, KV_DIM), 1) >> LOG_HEAD_DIM

    col_groups = COL_GROUPS
    n_win = len(col_groups) - 1

    def win_copy(l, i):
        cols = pl.ds(col_groups[i], col_groups[i + 1] - col_groups[i])
        return pltpu.make_async_copy(win_hbm[l].at[:, cols], win_ref.at[:, cols], wsem.at[i])

    def wout_copy(l):
        return pltpu.make_async_copy(wout_hbm[l], wout_ref, wsem.at[n_win])

    def start_for_layer(next_layer, start):
        for l in range(1, depth):
            @pl.when(next_layer == l)
            def _():
                start(l)

    def proj(xb, i, width):
        win_copy(0, i).wait()
        return _proj(xb, win_ref, col_groups[i], width)

    @pl.when(pl.program_id(0) == 0)
    def _():
        x_in = pltpu.make_async_copy(x_hbm, xcur, iosem.at[0])
        x_in.start()
        for i in range(n_win):
            win_copy(0, i).start()
        wout_copy(0).start()
        x_in.wait()

    @pl.when(grp == 0)
    def _():
        x = xcur[...]
        xb = x.astype(BF16)
        hq = proj(xb, 0, C_GA)
        q = hq[:, C_Q:C_Q + ATT_DIM] * Q_SCALE
        knew[...] = hq[:, C_K:C_K + KV_DIM]
        vnew[...] = hq[:, C_V:C_V + KV_DIM]
        for hd in range(N_HEADS):
            h, g = divmod(hd, GQA_GROUP)
            slab = q[:, h * KV_DIM:(h + 1) * KV_DIM]
            moved = pltpu.roll(slab, ((h - g) % GQA_GROUP) * HEAD_DIM, axis=1)
            qe[hd] = jnp.where(lane_blk == h, moved, 0.0).astype(BF16)
        gate_a[...] = _silu(proj(xb, 1, ATT_DIM))
        start_for_layer(layer + 1, lambda l: [win_copy(l, i).start() for i in (0, 1)])

        bg = proj(xb, 2, CONV_DIM)
        u = proj(xb, 3, CONV_DIM) * proj(xb, 4, CONV_DIM)
        start_for_layer(layer + 1, lambda l: [win_copy(l, i).start() for i in (2, 3, 4)])
        r8 = lax.broadcasted_iota(jnp.int32, (n_rows, CONV_DIM), 0) & (SAMPLE_ROWS - 1)
        is_state = (r8 >= TOK0 - 2) & (r8 < TOK0)
        u = jnp.where(is_state, st_ref[...], u)
        u_ref[...] = u
        cy = (cw_ref[0:1, :] * pltpu.roll(u, 2, axis=0) + cw_ref[1:2, :] * pltpu.roll(u, 1, axis=0)
              + cw_ref[2:3, :] * u)
        gate_c = _silu(proj(xb, 5, CONV_DIM))
        mix_ref[:, ATT_DIM:] = ((bg * cy) * gate_c).astype(BF16)

        qrow = lax.broadcasted_iota(jnp.int32, (n_q, 2 * WINDOW), 0)
        key = lax.broadcasted_iota(jnp.int32, (n_q, 2 * WINDOW), 1)
        q_tok = jnp.maximum((qrow & (SAMPLE_ROWS - 1)) - TOK0, 0)
        q_seq = (qrow >> LOG_SAMPLE_ROWS) & (SEQ_PER_GROUP - 1)
        new = key - WINDOW
        k_tok = (new & (SAMPLE_ROWS - 1)) - TOK0
        k_seq = new >> LOG_SAMPLE_ROWS
        cached = key < WINDOW
        dist = jnp.where(cached, WINDOW + q_tok - key, q_tok - k_tok)
        ok_new = (new >= 0) & (new < GROUP_ROWS) & (k_seq == q_seq) & (k_tok >= 0)
        visible = (dist >= 0) & (dist < WINDOW) & (cached | ok_new)
        slope = jnp.zeros((n_q, 2 * WINDOW), F32)
        sink = jnp.zeros((n_q, LANES), F32)
        srow = lax.broadcasted_iota(jnp.int32, (n_q, LANES), 0)
        for hd in range(N_HEADS):
            slope = jnp.where((qrow >> LOG_GROUP_ROWS) == hd, SLOPES[hd], slope)
            sink = jnp.where((srow >> LOG_GROUP_ROWS) == hd, sinks_ref[layer, hd], sink)
        bias_ref[...] = jnp.where(visible, -(slope * dist.astype(F32)), NEG_INF)
        sinkcol[...] = sink
        start_for_layer(layer + 1, lambda l: win_copy(l, n_win - 1).start())

    q_seq = ((lax.broadcasted_iota(jnp.int32, (n_q, KV_DIM), 0) >> LOG_SAMPLE_ROWS)
             & (SEQ_PER_GROUP - 1))
    newest = lax.broadcasted_iota(jnp.int32, (KV_DIM, WINDOW), 1) >= WINDOW - N_NEW
    blk16 = lax.broadcasted_iota(jnp.int32, (GROUP_ROWS, KV_DIM), 1) >> LOG_HEAD_DIM
    pad_rows = jnp.zeros((WINDOW - GROUP_ROWS, KV_DIM), F32)
    sink = sinkcol[:, 0:1]

    probs, v_new_bs, row0s = [], [], []
    for gi in range(GROUPS_PER_STEP):
        g0 = pl.multiple_of((grp * GROUPS_PER_STEP + gi) * GROUP_ROWS, GROUP_ROWS)
        row0s.append(g0)
        w_g = jnp.concatenate([qe[hd, pl.ds(g0, GROUP_ROWS), :] for hd in range(N_HEADS)], axis=0)
        k_new = jnp.concatenate([knew[pl.ds(g0, GROUP_ROWS), :], pad_rows], axis=0)
        v_new = jnp.concatenate([vnew[pl.ds(g0, GROUP_ROWS), :], pad_rows], axis=0)
        v_new_bs.append(v_new.astype(BF16))
        k_new_t = k_new.T
        v_new_t = v_new.T
        sc_cached = None
        for s in range(SEQ_PER_GROUP):
            n = gi * SEQ_PER_GROUP + s
            ck_t = ck_ref[n]
            sc_s = jnp.dot(w_g, ck_t.astype(BF16), preferred_element_type=F32)
            sc_cached = sc_s if s == 0 else jnp.where(q_seq[:, :WINDOW] == s, sc_s, sc_cached)
            to_tail = WINDOW - N_NEW - (s * SAMPLE_ROWS + TOK0)
            for cache_t, new_t, out_ref in ((ck_t, k_new_t, kb_ref), (cv_ref[n], v_new_t, vb_ref)):
                out_ref[n] = jnp.where(newest, pltpu.roll(new_t, to_tail, axis=1),
                                       pltpu.roll(cache_t, WINDOW - N_NEW, axis=1))
        sc_new = lax.dot_general(w_g, k_new.astype(BF16), _NT, preferred_element_type=F32)
        sc = jnp.concatenate([sc_cached, sc_new], axis=1)
        probs.append(_softmax_rows(sc + bias_ref[...], sink).astype(BF16))

    for gi in range(GROUPS_PER_STEP):
        p = probs[gi]
        o_grp = jnp.dot(p[:, WINDOW:], v_new_bs[gi], preferred_element_type=F32)
        o_cached = None
        for s in range(SEQ_PER_GROUP):
            cv_t = cv_ref[gi * SEQ_PER_GROUP + s]
            o_s = lax.dot_general(p[:, :WINDOW], cv_t.astype(BF16), _NT,
                                  preferred_element_type=F32)
            o_cached = o_s if s == 0 else jnp.where(q_seq == s, o_s, o_cached)
        o_grp = o_grp + o_cached
        for h in range(N_KV_HEADS):
            slab = jnp.zeros((GROUP_ROWS, KV_DIM), F32)
            for g in range(GQA_GROUP):
                hd = h * GQA_GROUP + g
                piece = jnp.where(blk16 == h, o_grp[hd * GROUP_ROWS:(hd + 1) * GROUP_ROWS, :], 0.0)
                slab = slab + pltpu.roll(piece, ((g - h) % GQA_GROUP) * HEAD_DIM, axis=1)
            rows = pl.ds(row0s[gi], GROUP_ROWS)
            cols = slice(h * KV_DIM, (h + 1) * KV_DIM)
            mix_ref[rows, cols] = (slab * gate_a[rows, cols]).astype(BF16)

    @pl.when(grp == n_steps - 1)
    def _():
        wout_copy(0).wait()
        out = jnp.dot(mix_ref[...], wout_ref[...], preferred_element_type=F32)
        xcur[...] = _layer_norm(alpha * xcur[...] + out, g_ref[...], b_ref[...])
        start_for_layer(layer + 1, lambda l: wout_copy(l).start())

        @pl.when(layer == depth - 1)
        def _():
            y_out = pltpu.make_async_copy(xcur, y_hbm, iosem.at[1])
            y_out.start()
            y_out.wait()


def _sample_layers(x8, st_all, ck_all, cv_all, w_in_bs, w_out_bs, conv_w, sinks, ln_g, ln_b, alpha):
    n_rows, d = x8.shape
    depth, n_seq = ck_all.shape[:2]
    n_steps = n_seq // SEQ_PER_STEP
    n_q = N_HEADS * GROUP_ROWS
    by_layer = lambda g: (g // n_steps, 0, 0)
    cache_spec = pl.BlockSpec((None, SEQ_PER_STEP, KV_DIM, WINDOW),
                              lambda g: (g // n_steps, g % n_steps, 0, 0))
    hbm = pl.BlockSpec(memory_space=pl.ANY)
    kernel = functools.partial(_sample_kernel, depth=depth, n_steps=n_steps, n_rows=n_rows,
                               alpha=alpha)
    return pl.pallas_call(
        kernel,
        grid=(depth * n_steps,),
        in_specs=[
            pl.BlockSpec(memory_space=pltpu.SMEM),
            hbm,
            pl.BlockSpec((None, 3, CONV_DIM), by_layer),
            pl.BlockSpec((None, 1, d), by_layer),
            pl.BlockSpec((None, 1, d), by_layer),
            cache_spec,
            cache_spec,
            pl.BlockSpec((None, n_rows, CONV_DIM), by_layer),
        ] + [hbm] * (2 * depth),
        out_specs=[
            hbm,
            cache_spec,
            cache_spec,
            pl.BlockSpec((None, n_rows, CONV_DIM), by_layer),
        ],
        out_shape=[
            jax.ShapeDtypeStruct((n_rows, d), F32),
            jax.ShapeDtypeStruct(ck_all.shape, F32),
            jax.ShapeDtypeStruct(cv_all.shape, F32),
            jax.ShapeDtypeStruct((depth, n_rows, CONV_DIM), F32),
        ],
        scratch_shapes=[
            pltpu.VMEM((d, PROJ_DIM), BF16),
            pltpu.VMEM((ATT_DIM + CONV_DIM, d), BF16),
            pltpu.SemaphoreType.DMA((len(COL_GROUPS),)),
            pltpu.SemaphoreType.DMA((2,)),
            pltpu.VMEM((n_rows, d), F32),
            pltpu.VMEM((N_HEADS, n_rows, KV_DIM), BF16),
            pltpu.VMEM((n_rows, KV_DIM), F32),
            pltpu.VMEM((n_rows, KV_DIM), F32),
            pltpu.VMEM((n_rows, ATT_DIM), F32),
            pltpu.VMEM((n_q, 2 * WINDOW), F32),
            pltpu.VMEM((n_q, LANES), F32),
            pltpu.VMEM((n_rows, ATT_DIM + CONV_DIM), BF16),
        ],
        compiler_params=pltpu.CompilerParams(
            dimension_semantics=("arbitrary",),
            vmem_limit_bytes=DECODE_VMEM_LIMIT,
        ),
        name="sample_layers",
    )(sinks, x8, conv_w, ln_g, ln_b, ck_all, cv_all, st_all, *w_in_bs, *w_out_bs)


def kernel(x_prompt, x_sample, cache_k, cache_v, state_conv, meta_tokens,
           w_in, conv_w, sinks, w_out, ln_g, ln_b):
    depth = w_in.shape[0]
    alpha = float((2 * depth) ** 0.25)
    batch, seq, d = x_prompt.shape
    n_seq, n_tok = x_sample.shape[:2]
    assert d == D_MODEL and seq % PROMPT_TILE == 0 and n_tok == SAMPLE_ROWS - TOK0
    assert meta_tokens.shape[0] == N_META and n_seq % SEQ_PER_STEP == 0
    assert cache_k.shape[2] == WINDOW and state_conv.shape[2] == 2

    w_in_b = w_in[0].astype(BF16)
    w_out_b = w_out[0].astype(BF16)
    ln_g3 = ln_g.reshape(depth, 1, d)
    ln_b3 = ln_b.reshape(depth, 1, d)
    xp = x_prompt
    xh = meta_tokens.astype(F32)
    xs = jnp.pad(x_sample, ((0, 0), (TOK0, 0), (0, 0))).reshape(n_seq * SAMPLE_ROWS, d)
    ck_all = jnp.transpose(cache_k, (0, 1, 3, 4, 2)).reshape(depth, n_seq, KV_DIM, WINDOW)
    cv_all = jnp.transpose(cache_v, (0, 1, 3, 4, 2)).reshape(depth, n_seq, KV_DIM, WINDOW)
    st_all = jnp.pad(state_conv, ((0, 0), (0, 0), (TOK0 - 2, SAMPLE_ROWS - TOK0), (0, 0)))
    st_all = st_all.reshape(depth, n_seq * SAMPLE_ROWS, CONV_DIM)

    kp, vp, cp = [], [], []
    w_in_bs, w_out_bs = [w_in_b], [w_out_b]
    for l in range(depth):
        xp, xh, k_last, v_last, c_last, *next_weights = _prompt_layer(
            l, xp, xh, w_in_bs[l], w_out_bs[l], w_in, w_out, conv_w, sinks, ln_g3, ln_b3, alpha)
        kp.append(k_last.reshape(batch, WINDOW, N_KV_HEADS, HEAD_DIM))
        vp.append(v_last.reshape(batch, WINDOW, N_KV_HEADS, HEAD_DIM))
        cp.append(c_last[:, 6:8, :])
        if next_weights:
            w_in_bs.append(next_weights[0])
            w_out_bs.append(next_weights[1])
    xs, kb_all, vb_all, u_all = _sample_layers(
        xs, st_all, ck_all, cv_all, w_in_bs, w_out_bs, conv_w, sinks, ln_g3, ln_b3, alpha)
    c_sample = u_all.reshape(depth, n_seq, SAMPLE_ROWS, CONV_DIM)[:, :, SAMPLE_ROWS - 2:, :]
    y_sample = xs.reshape(n_seq, SAMPLE_ROWS, d)[:, TOK0:, :]
    kv_shape = (depth, n_seq, N_KV_HEADS, HEAD_DIM, WINDOW)
    k_sample = jnp.transpose(kb_all.reshape(kv_shape), (0, 1, 4, 2, 3))
    v_sample = jnp.transpose(vb_all.reshape(kv_shape), (0, 1, 4, 2, 3))
    return (xp, y_sample, jnp.stack(kp), jnp.stack(vp), jnp.stack(cp),
            k_sample, v_sample, c_sample)
```

```python
import functools

import numpy as np
import jax
import jax.numpy as jnp
from jax import lax
from jax.experimental import pallas as pl
from jax.experimental.pallas import tpu as pltpu

F32 = jnp.float32
BF16 = jnp.bfloat16

D_MODEL = 2048
N_META = 16
ATT_DIM = 1024
CONV_DIM = 1024
HEAD_DIM = 64
N_HEADS = 16
N_KV_HEADS = 4
GQA_GROUP = N_HEADS // N_KV_HEADS
KV_DIM = N_KV_HEADS * HEAD_DIM
WINDOW = 128
PROJ_DIM = 2 * ATT_DIM + 2 * KV_DIM + 4 * CONV_DIM
LN_EPS = 1e-5
NEG_INF = -1e30
Q_SCALE = HEAD_DIM ** -0.5

C_Q = 0
C_K = ATT_DIM
C_V = C_K + KV_DIM
C_GA = C_V + KV_DIM
C_B = C_GA + ATT_DIM
C_C = C_B + CONV_DIM
C_H = C_C + CONV_DIM
C_GC = C_H + CONV_DIM
COL_GROUPS = (C_Q, C_GA, C_B, C_C, C_H, C_GC, PROJ_DIM)

LANES = 128
HEAD_PAD = WINDOW - N_META
PROMPT_TILE = 256
FILL_CHUNK = 256
OUT_CHUNK = 512
SAMPLE_ROWS = 8
VMEM_LIMIT = 58 * 1024 * 1024
DECODE_VMEM_LIMIT = 62 * 1024 * 1024

SLOPES = [float(np.float32(2.0 ** (-8.0 * (h + 1) / N_HEADS))) for h in range(N_HEADS)]

_NT = (((1,), (1,)), ((), ()))


def _silu(g):
    return g * (1.0 / (1.0 + jnp.exp(-g)))


def _softmax_rows(s, sink):
    m = jnp.maximum(jnp.max(s, axis=1, keepdims=True), sink)
    p = jnp.exp(s - m)
    denom = jnp.sum(p, axis=1, keepdims=True) + jnp.exp(sink - m)
    return p * (1.0 / denom)


def _softmax_cols(s, sink):
    m = jnp.maximum(jnp.max(s, axis=0, keepdims=True), sink)
    p = jnp.exp(s - m)
    denom = jnp.sum(p, axis=0, keepdims=True) + jnp.exp(sink - m)
    return p * (1.0 / denom)


def _layer_norm(z, g, b):
    mu = jnp.mean(z, axis=1, keepdims=True)
    zc = z - mu
    var = jnp.mean(zc * zc, axis=1, keepdims=True)
    return zc * lax.rsqrt(var + LN_EPS) * g + b


def _proj(xb, w_ref, c0, width):
    return jnp.dot(xb, w_ref[:, c0:c0 + width], preferred_element_type=F32)


def _store_k_variants(src, dst, r0, rows):
    low = lax.broadcasted_iota(jnp.int32, (rows, LANES), 1) < HEAD_DIM
    for cc in range(KV_DIM // LANES):
        col = src[:, cc * LANES:(cc + 1) * LANES]
        swapped = pltpu.roll(col, HEAD_DIM, axis=1)
        h_even, h_odd = 2 * cc, 2 * cc + 1
        dst[2 * h_even + 0, r0:r0 + rows, :] = jnp.where(low, col, 0.0).astype(BF16)
        dst[2 * h_even + 1, r0:r0 + rows, :] = jnp.where(low, 0.0, swapped).astype(BF16)
        dst[2 * h_odd + 0, r0:r0 + rows, :] = jnp.where(low, swapped, 0.0).astype(BF16)
        dst[2 * h_odd + 1, r0:r0 + rows, :] = jnp.where(low, 0.0, col).astype(BF16)


def _conv_chunk_pieces(xb, win_ref, cw_ref, ucar, mix_ref, rows, lo, width):
    cols = slice(lo, lo + width)
    got = {}

    def project(name, c0):
        got[name] = _proj(xb, win_ref, c0 + lo, width)

    def finish():
        u = got["c"] * got["h"]
        row = lax.broadcasted_iota(jnp.int32, (rows, width), 0)
        prev1 = ucar[7:8, cols]
        prev2 = ucar[6:7, cols]
        u1 = jnp.where(row == 0, prev1, pltpu.roll(u, 1, axis=0))
        u2 = jnp.where(row == 0, prev2, jnp.where(row == 1, prev1, pltpu.roll(u, 2, axis=0)))
        cy = cw_ref[0:1, cols] * u2 + cw_ref[1:2, cols] * u1 + cw_ref[2:3, cols] * u
        ucar[:, cols] = u[rows - 8:rows, :]
        gate_c = _silu(_proj(xb, win_ref, C_GC + lo, width))
        mix_ref[0:rows, ATT_DIM + lo:ATT_DIM + lo + width] = (
            (got["b"] * cy) * gate_c).astype(BF16)

    return [functools.partial(project, "b", C_B), functools.partial(project, "c", C_C),
            functools.partial(project, "h", C_H), finish]


def _store_v_transposed(src, dst, c0, rows):
    vt = src.T
    for h in range(N_KV_HEADS):
        dst[h, :, c0:c0 + rows] = vt[h * HEAD_DIM:(h + 1) * HEAD_DIM, :].astype(BF16)


def _scores_group(layer, qb, kmask, sinks_ref, kvar, bias_ref, p_scr, r0, h):
    q4 = jnp.concatenate(
        [qb[r0:r0 + WINDOW, c * LANES:(c + 1) * LANES] for c in (2 * h, 2 * h + 1)], axis=0)
    for par in range(2):
        keys = kvar[2 * h + par, r0:r0 + 2 * WINDOW, :]
        s2 = lax.dot_general(keys, q4, _NT, preferred_element_type=F32)
        for half in range(2):
            hd = GQA_GROUP * h + 2 * half + par
            lanes = slice(half * LANES, (half + 1) * LANES)
            s = s2[:, lanes] + SLOPES[hd] * bias_ref[...]
            if kmask is not None:
                s = s + kmask
            p_scr[2 * h + par, :, lanes] = _softmax_cols(s, sinks_ref[layer, hd]).astype(BF16)


def _values_group(gate_ref, vtvar, p_scr, mix_ref, r0, h):
    vals_t = vtvar[h, :, r0:r0 + 2 * WINDOW]
    o_t = jnp.concatenate([jnp.dot(vals_t, p_scr[2 * h + par], preferred_element_type=F32)
                           for par in range(2)], axis=0)
    for half in range(2):
        lanes = slice((2 * h + half) * LANES, (2 * h + half + 1) * LANES)
        o = o_t[:, half * LANES:(half + 1) * LANES].T
        mix_ref[r0:r0 + WINDOW, lanes] = (o * gate_ref[r0:r0 + WINDOW, lanes]).astype(BF16)


def _gate_chunk(xb, win_ref, gate_ref, rows, lo, width):
    gate_ref[0:rows, lo:lo + width] = _silu(_proj(xb, win_ref, C_GA + lo, width))


def _prompt_rows(x, rows, kmask0, after_qkv, arrived, layer, sinks_ref, win_ref, wout_ref, cw_ref,
                 kvar, vtvar, ucar, bias_ref, mix_ref, p_scr, gate_ref, z_ref, alpha):
    if arrived is None:
        arrived = lambda i: None
        in_arrival_order = False
    else:
        in_arrival_order = True
    xb = x.astype(BF16)
    arrived(0)
    hq = _proj(xb, win_ref, 0, C_GA)
    after_qkv()
    qb = (hq[:, C_Q:C_Q + ATT_DIM] * Q_SCALE).astype(BF16)
    kf = hq[:, C_K:C_K + KV_DIM]
    vf = hq[:, C_V:C_V + KV_DIM]
    _store_k_variants(kf, kvar, WINDOW, rows)
    _store_v_transposed(vf, vtvar, WINDOW, rows)

    n_blocks = rows // WINDOW
    gates = [(1, functools.partial(_gate_chunk, xb, win_ref, gate_ref, rows, lo, FILL_CHUNK))
             for lo in range(0, ATT_DIM, FILL_CHUNK)]
    convs = [list(zip((2, 3, 4, 5),
                      _conv_chunk_pieces(xb, win_ref, cw_ref, ucar, mix_ref, rows, lo, FILL_CHUNK)))
             for lo in range(0, CONV_DIM, FILL_CHUNK)]
    if in_arrival_order:
        fillers = gates + [conv[k] for k in range(4) for conv in convs]
    else:
        fillers = gates[:2] + convs[0] + gates[2:] + [p for conv in convs[1:] for p in conv]
    n_units = n_blocks * N_KV_HEADS
    cuts = [len(fillers) * u // n_units for u in range(n_units + 1)]
    assert cuts[N_KV_HEADS] >= len(gates) + len(convs[0])
    for blk in range(n_blocks):
        r0 = blk * WINDOW
        for h in range(N_KV_HEADS):
            unit = blk * N_KV_HEADS + h
            for group, filler in fillers[cuts[unit]:cuts[unit + 1]]:
                arrived(group)
                filler()
            _scores_group(layer, qb, kmask0 if blk == 0 else None, sinks_ref, kvar, bias_ref,
                          p_scr, r0, h)
        for h in range(N_KV_HEADS):
            _values_group(gate_ref, vtvar, p_scr, mix_ref, r0, h)
        if blk == 0:
            for i in range(2 * N_KV_HEADS):
                kvar[i, 0:WINDOW, :] = kvar[i, rows:rows + WINDOW, :]
            for i in range(N_KV_HEADS):
                vtvar[i, :, 0:WINDOW] = vtvar[i, :, rows:rows + WINDOW]

    arrived(len(COL_GROUPS) - 1)
    mix = mix_ref[0:rows, :]
    for c0 in range(0, D_MODEL, OUT_CHUNK):
        cols = slice(c0, c0 + OUT_CHUNK)
        out = jnp.dot(mix, wout_ref[:, cols], preferred_element_type=F32)
        z_ref[0:rows, cols] = alpha * x[:, cols] + out
    return kf, vf


def _cast_next_weights(t, last, next_layer, win_f32, wout_f32, win_next, wout_next,
                       stage_in, stage_out, sem):
    rows = stage_in[0].shape[0]
    srcs = (win_f32, wout_f32)
    dsts = (win_next, wout_next)

    def read(k, i):
        return pltpu.make_async_copy(srcs[i].at[next_layer, pl.ds(k * rows, rows), :],
                                     stage_in[i], sem.at[i])

    def write(k, i):
        return pltpu.make_async_copy(stage_out[i], dsts[i].at[pl.ds(k * rows, rows), :],
                                     sem.at[2 + i])

    @pl.when(t >= 1)
    def _():
        for i in range(2):
            read(t - 1, i).wait()

        @pl.when(t >= 2)
        def _():
            for i in range(2):
                write(t - 2, i).wait()

        for i in range(2):
            stage_out[i][...] = stage_in[i][...].astype(BF16)
            write(t - 1, i).start()

    @pl.when(t < last)
    def _():
        for i in range(2):
            read(t, i).start()

    @pl.when(t == last)
    def _():
        for i in range(2):
            write(t - 1, i).wait()


def _prompt_kernel_casting(sinks_ref, x_ref, xh_ref, win_hbm, wout_hbm, cw_ref, g_ref, b_ref,
                           win_f32, wout_f32,
                           y_ref, yh_ref, kl_ref, vl_ref, cs_ref, win_next, wout_next,
                           *scratch, layer, **static):
    cast_in_a, cast_in_b, cast_out_a, cast_out_b, cast_sem = scratch[-5:]
    _cast_next_weights(pl.program_id(0), pl.num_programs(0) - 1, layer + 1, win_f32, wout_f32,
                       win_next, wout_next, (cast_in_a, cast_in_b), (cast_out_a, cast_out_b),
                       cast_sem)
    _prompt_kernel(sinks_ref, x_ref, xh_ref, win_hbm, wout_hbm, cw_ref, g_ref, b_ref,
                   y_ref, yh_ref, kl_ref, vl_ref, cs_ref, *scratch[:-5], layer=layer, **static)


def _prompt_kernel(sinks_ref, x_ref, xh_ref, win_hbm, wout_hbm, cw_ref, g_ref, b_ref,
                   y_ref, yh_ref, kl_ref, vl_ref, cs_ref,
                   kvar, vtvar, khead, vhead, ucar, uhead, bias_ref, mix_ref, p_scr, gate_ref,
                   z_scr, win_ref, wout_ref, wsem, *, layer, tm, n_tiles, alpha):
    t = pl.program_id(0)
    last = pl.num_programs(0) - 1
    j = t % n_tiles
    shared = (layer, sinks_ref, win_ref, wout_ref, cw_ref, kvar, vtvar, ucar, bias_ref, mix_ref,
              p_scr, gate_ref, z_scr, alpha)
    key_row = lax.broadcasted_iota(jnp.int32, (2 * WINDOW, WINDOW), 0)
    n_win = len(COL_GROUPS) - 1

    def weight_copy(i):
        if i == n_win:
            return pltpu.make_async_copy(wout_hbm, wout_ref, wsem.at[i])
        cols = pl.ds(COL_GROUPS[i], COL_GROUPS[i + 1] - COL_GROUPS[i])
        return pltpu.make_async_copy(win_hbm.at[:, cols], win_ref.at[:, cols], wsem.at[i])

    @pl.when(t == 0)
    def _():
        for i in range(n_win + 1):
            weight_copy(i).start()
        waited = set()

        def arrived(i):
            if i not in waited:
                waited.add(i)
                weight_copy(i).wait()

        qi = lax.broadcasted_iota(jnp.int32, (2 * WINDOW, WINDOW), 1)
        dist = WINDOW + qi - key_row
        visible = (dist >= 0) & (dist < WINDOW)
        bias_ref[...] = jnp.where(visible, -dist.astype(F32), NEG_INF)
        z_scr[...] = jnp.zeros(z_scr.shape, F32)
        kvar[:, 0:WINDOW, :] = jnp.zeros((2 * N_KV_HEADS, WINDOW, LANES), BF16)
        vtvar[:, :, 0:WINDOW] = jnp.zeros((N_KV_HEADS, HEAD_DIM, WINDOW), BF16)
        ucar[...] = jnp.zeros(ucar.shape, F32)
        kmask = jnp.where(key_row < WINDOW + HEAD_PAD, NEG_INF, 0.0)
        xh = jnp.concatenate([jnp.zeros((HEAD_PAD, D_MODEL), F32), xh_ref[...]], axis=0)
        _prompt_rows(xh, WINDOW, kmask, lambda: None, arrived, *shared)
        assert len(waited) == n_win + 1
        yh_ref[...] = _layer_norm(z_scr[0:WINDOW, :], g_ref[...], b_ref[...])[HEAD_PAD:, :]
        khead[...] = kvar[:, 0:WINDOW, :]
        vhead[...] = vtvar[:, :, 0:WINDOW]
        uhead[...] = ucar[...]

    def norm_previous_tile():
        y_ref[...] = _layer_norm(z_scr[...], g_ref[...], b_ref[...])

    @pl.when(t < last)
    def _():
        @pl.when(j == 0)
        def _():
            kvar[:, 0:WINDOW, :] = khead[...]
            vtvar[:, :, 0:WINDOW] = vhead[...]
            ucar[...] = uhead[...]

        kmask = jnp.where(key_row < HEAD_PAD, jnp.where(j == 0, NEG_INF, 0.0), 0.0)
        kf, vf = _prompt_rows(x_ref[...], tm, kmask, norm_previous_tile, None, *shared)

        @pl.when(j == n_tiles - 1)
        def _():
            kl_ref[...] = kf[tm - WINDOW:tm, :]
            vl_ref[...] = vf[tm - WINDOW:tm, :]
            cs_ref[...] = ucar[...]

    @pl.when(t == last)
    def _():
        norm_previous_tile()


def _prompt_layer(layer, x, xh, w_in_b, w_out_b, w_in, w_out, conv_w, sinks, ln_g, ln_b, alpha):
    batch, seq, d = x.shape
    depth = w_in.shape[0]
    tm = PROMPT_TILE
    n_tiles = seq // tm
    total = batch * n_tiles
    const2 = lambda t: (0, 0)
    this_layer = lambda t: (layer, 0, 0)

    def tile_block(t):
        t = jnp.minimum(t, total - 1)
        return (t // n_tiles, t % n_tiles, 0)

    def prev_tile_block(t):
        return tile_block(jnp.maximum(t - 1, 0))

    per_batch = lambda t: (jnp.minimum(t, total - 1) // n_tiles, 0, 0)
    static = dict(layer=layer, tm=tm, n_tiles=n_tiles, alpha=alpha)
    operands = [sinks, x, xh, w_in_b, w_out_b, conv_w, ln_g, ln_b]
    hbm = pl.BlockSpec(memory_space=pl.ANY)
    streamed_in = [
        pl.BlockSpec((None, tm, d), tile_block),
        pl.BlockSpec((N_META, d), const2),
        pl.BlockSpec((None, 3, CONV_DIM), this_layer),
        pl.BlockSpec((None, 1, d), this_layer),
        pl.BlockSpec((None, 1, d), this_layer),
    ]
    in_specs = [pl.BlockSpec(memory_space=pltpu.SMEM)] + [hbm] * (len(operands) - 1)
    streamed_out = [
        pl.BlockSpec((None, tm, d), prev_tile_block),
        pl.BlockSpec((N_META, d), const2),
        pl.BlockSpec((None, WINDOW, KV_DIM), per_batch),
        pl.BlockSpec((None, WINDOW, KV_DIM), per_batch),
        pl.BlockSpec((None, 8, CONV_DIM), per_batch),
    ]
    out_shape = [
        jax.ShapeDtypeStruct((batch, seq, d), F32),
        jax.ShapeDtypeStruct((N_META, d), F32),
        jax.ShapeDtypeStruct((batch, WINDOW, KV_DIM), F32),
        jax.ShapeDtypeStruct((batch, WINDOW, KV_DIM), F32),
        jax.ShapeDtypeStruct((batch, 8, CONV_DIM), F32),
    ]
    scratch_shapes = [
        pltpu.VMEM((2 * N_KV_HEADS, WINDOW + tm, LANES), BF16),
        pltpu.VMEM((N_KV_HEADS, HEAD_DIM, WINDOW + tm), BF16),
        pltpu.VMEM((2 * N_KV_HEADS, WINDOW, LANES), BF16),
        pltpu.VMEM((N_KV_HEADS, HEAD_DIM, WINDOW), BF16),
        pltpu.VMEM((8, CONV_DIM), F32),
        pltpu.VMEM((8, CONV_DIM), F32),
        pltpu.VMEM((2 * WINDOW, WINDOW), F32),
        pltpu.VMEM((tm, ATT_DIM + CONV_DIM), BF16),
        pltpu.VMEM((2 * N_KV_HEADS, 2 * WINDOW, 2 * WINDOW), BF16),
        pltpu.VMEM((tm, ATT_DIM), F32),
        pltpu.VMEM((tm, d), F32),
        pltpu.VMEM((d, PROJ_DIM), BF16),
        pltpu.VMEM((ATT_DIM + CONV_DIM, d), BF16),
        pltpu.SemaphoreType.DMA((len(COL_GROUPS),)),
    ]
    step_kernel = _prompt_kernel
    casting = layer + 1 < depth
    if casting:
        cast_rows = d // total
        assert cast_rows * total == d and cast_rows % 16 == 0
        step_kernel = _prompt_kernel_casting
        operands += [w_in, w_out]
        in_specs += [hbm] * 2
        out_shape += [jax.ShapeDtypeStruct(w_in_b.shape, BF16),
                      jax.ShapeDtypeStruct(w_out_b.shape, BF16)]
        scratch_shapes += [
            pltpu.VMEM((cast_rows, PROJ_DIM), F32),
            pltpu.VMEM((cast_rows, d), F32),
            pltpu.VMEM((cast_rows, PROJ_DIM), BF16),
            pltpu.VMEM((cast_rows, d), BF16),
            pltpu.SemaphoreType.DMA((4,)),
        ]
    n_streamed_out = len(streamed_out)

    def call_body(sinks_ref, x_hbm, xh_hbm, win_hbm, wout_hbm, cw_hbm, g_hbm, b_hbm, *rest):
        f32_weights, rest = (rest[:2], rest[2:]) if casting else ((), rest)
        outs, rest = rest[:n_streamed_out], rest[n_streamed_out:]
        next_weights, scratch = (rest[:2], rest[2:]) if casting else ((), rest)

        def step(x_ref, xh_ref, cw_ref, g_ref, b_ref, y_ref, yh_ref, kl_ref, vl_ref, cs_ref):
            step_kernel(sinks_ref, x_ref, xh_ref, win_hbm, wout_hbm, cw_ref, g_ref, b_ref,
                        *f32_weights, y_ref, yh_ref, kl_ref, vl_ref, cs_ref, *next_weights,
                        *scratch, **static)

        pltpu.emit_pipeline(step, grid=(total + 1,), in_specs=streamed_in,
                            out_specs=streamed_out)(x_hbm, xh_hbm, cw_hbm, g_hbm, b_hbm, *outs)

    return pl.pallas_call(
        call_body,
        in_specs=in_specs,
        out_specs=[hbm] * len(out_shape),
        out_shape=out_shape,
        scratch_shapes=scratch_shapes,
        compiler_params=pltpu.CompilerParams(vmem_limit_bytes=VMEM_LIMIT),
        name="prompt_layer",
    )(*operands)


SEQ_PER_GROUP = 2
GROUP_ROWS = SEQ_PER_GROUP * SAMPLE_ROWS
GROUPS_PER_STEP = 4
SEQ_PER_STEP = SEQ_PER_GROUP * GROUPS_PER_STEP
N_NEW = 4
TOK0 = SAMPLE_ROWS - N_NEW
LOG_SAMPLE_ROWS = 3
LOG_GROUP_ROWS = 4
LOG_HEAD_DIM = 6
assert (1 << LOG_SAMPLE_ROWS, 1 << LOG_GROUP_ROWS, 1 << LOG_HEAD_DIM) == (
    SAMPLE_ROWS, GROUP_ROWS, HEAD_DIM)


def _sample_kernel(sinks_ref, x_hbm, cw_ref, g_ref, b_ref, ck_ref, cv_ref, st_ref, *refs,
                   depth, n_steps, n_rows, alpha):
    win_hbm, wout_hbm = refs[:depth], refs[depth:2 * depth]
    (y_hbm, kb_ref, vb_ref, u_ref, win_ref, wout_ref, wsem, iosem, xcur,
     qe, knew, vnew, gate_a, bias_ref, sinkcol, mix_ref) = refs[2 * depth:]
    layer = pl.program_id(0) // n_steps
    grp = pl.program_id(0) % n_steps
    n_q = N_HEADS * GROUP_ROWS
    lane_blk = lax.broadcasted_iota(jnp.int32, (n_rows, KV_DIM), 1) >> LOG_HEAD_DIM

    col_groups = COL_GROUPS
    n_win = len(col_groups) - 1

    def win_copy(l, i):
        cols = pl.ds(col_groups[i], col_groups[i + 1] - col_groups[i])
        return pltpu.make_async_copy(win_hbm[l].at[:, cols], win_ref.at[:, cols], wsem.at[i])

    def wout_copy(l):
        return pltpu.make_async_copy(wout_hbm[l], wout_ref, wsem.at[n_win])

    def start_for_layer(next_layer, start):
        for l in range(1, depth):
            @pl.when(next_layer == l)
            def _():
                start(l)

    def proj(xb, i, width):
        win_copy(0, i).wait()
        return _proj(xb, win_ref, col_groups[i], width)

    @pl.when(pl.program_id(0) == 0)
    def _():
        x_in = pltpu.make_async_copy(x_hbm, xcur, iosem.at[0])
        x_in.start()
        for i in range(n_win):
            win_copy(0, i).start()
        wout_copy(0).start()
        x_in.wait()

    @pl.when(grp == 0)
    def _():
        x = xcur[...]
        xb = x.astype(BF16)
        hq = proj(xb, 0, C_GA)
        q = hq[:, C_Q:C_Q + ATT_DIM] * Q_SCALE
        knew[...] = hq[:, C_K:C_K + KV_DIM]
        vnew[...] = hq[:, C_V:C_V + KV_DIM]
        for hd in range(N_HEADS):
            h, g = divmod(hd, GQA_GROUP)
            slab = q[:, h * KV_DIM:(h + 1) * KV_DIM]
            moved = pltpu.roll(slab, ((h - g) % GQA_GROUP) * HEAD_DIM, axis=1)
            qe[hd] = jnp.where(lane_blk == h, moved, 0.0).astype(BF16)
        gate_a[...] = _silu(proj(xb, 1, ATT_DIM))
        start_for_layer(layer + 1, lambda l: [win_copy(l, i).start() for i in (0, 1)])

        bg = proj(xb, 2, CONV_DIM)
        u = proj(xb, 3, CONV_DIM) * proj(xb, 4, CONV_DIM)
        start_for_layer(layer + 1, lambda l: [win_copy(l, i).start() for i in (2, 3, 4)])
        r8 = lax.broadcasted_iota(jnp.int32, (n_rows, CONV_DIM), 0) & (SAMPLE_ROWS - 1)
        is_state = (r8 >= TOK0 - 2) & (r8 < TOK0)
        u = jnp.where(is_state, st_ref[...], u)
        u_ref[...] = u
        cy = (cw_ref[0:1, :] * pltpu.roll(u, 2, axis=0) + cw_ref[1:2, :] * pltpu.roll(u, 1, axis=0)
              + cw_ref[2:3, :] * u)
        gate_c = _silu(proj(xb, 5, CONV_DIM))
        mix_ref[:, ATT_DIM:] = ((bg * cy) * gate_c).astype(BF16)

        qrow = lax.broadcasted_iota(jnp.int32, (n_q, 2 * WINDOW), 0)
        key = lax.broadcasted_iota(jnp.int32, (n_q, 2 * WINDOW), 1)
        q_tok = jnp.maximum((qrow & (SAMPLE_ROWS - 1)) - TOK0, 0)
        q_seq = (qrow >> LOG_SAMPLE_ROWS) & (SEQ_PER_GROUP - 1)
        new = key - WINDOW
        k_tok = (new & (SAMPLE_ROWS - 1)) - TOK0
        k_seq = new >> LOG_SAMPLE_ROWS
        cached = key < WINDOW
        dist = jnp.where(cached, WINDOW + q_tok - key, q_tok - k_tok)
        ok_new = (new >= 0) & (new < GROUP_ROWS) & (k_seq == q_seq) & (k_tok >= 0)
        visible = (dist >= 0) & (dist < WINDOW) & (cached | ok_new)
        slope = jnp.zeros((n_q, 2 * WINDOW), F32)
        sink = jnp.zeros((n_q, LANES), F32)
        srow = lax.broadcasted_iota(jnp.int32, (n_q, LANES), 0)
        for hd in range(N_HEADS):
            slope = jnp.where((qrow >> LOG_GROUP_ROWS) == hd, SLOPES[hd], slope)
            sink = jnp.where((srow >> LOG_GROUP_ROWS) == hd, sinks_ref[layer, hd], sink)
        bias_ref[...] = jnp.where(visible, -(slope * dist.astype(F32)), NEG_INF)
        sinkcol[...] = sink
        start_for_layer(layer + 1, lambda l: win_copy(l, n_win - 1).start())

    q_seq = ((lax.broadcasted_iota(jnp.int32, (n_q, KV_DIM), 0) >> LOG_SAMPLE_ROWS)
             & (SEQ_PER_GROUP - 1))
    newest = lax.broadcasted_iota(jnp.int32, (KV_DIM, WINDOW), 1) >= WINDOW - N_NEW
    blk16 = lax.broadcasted_iota(jnp.int32, (GROUP_ROWS, KV_DIM), 1) >> LOG_HEAD_DIM
    pad_rows = jnp.zeros((WINDOW - GROUP_ROWS, KV_DIM), F32)
    sink = sinkcol[:, 0:1]

    probs, v_new_bs, row0s = [], [], []
    for gi in range(GROUPS_PER_STEP):
        g0 = pl.multiple_of((grp * GROUPS_PER_STEP + gi) * GROUP_ROWS, GROUP_ROWS)
        row0s.append(g0)
        w_g = jnp.concatenate([qe[hd, pl.ds(g0, GROUP_ROWS), :] for hd in range(N_HEADS)], axis=0)
        k_new = jnp.concatenate([knew[pl.ds(g0, GROUP_ROWS), :], pad_rows], axis=0)
        v_new = jnp.concatenate([vnew[pl.ds(g0, GROUP_ROWS), :], pad_rows], axis=0)
        v_new_bs.append(v_new.astype(BF16))
        k_new_t = k_new.T
        v_new_t = v_new.T
        sc_cached = None
        for s in range(SEQ_PER_GROUP):
            n = gi * SEQ_PER_GROUP + s
            ck_t = ck_ref[n]
            sc_s = jnp.dot(w_g, ck_t.astype(BF16), preferred_element_type=F32)
            sc_cached = sc_s if s == 0 else jnp.where(q_seq[:, :WINDOW] == s, sc_s, sc_cached)
            to_tail = WINDOW - N_NEW - (s * SAMPLE_ROWS + TOK0)
            for cache_t, new_t, out_ref in ((ck_t, k_new_t, kb_ref), (cv_ref[n], v_new_t, vb_ref)):
                out_ref[n] = jnp.where(newest, pltpu.roll(new_t, to_tail, axis=1),
                                       pltpu.roll(cache_t, WINDOW - N_NEW, axis=1))
        sc_new = lax.dot_general(w_g, k_new.astype(BF16), _NT, preferred_element_type=F32)
        sc = jnp.concatenate([sc_cached, sc_new], axis=1)
        probs.append(_softmax_rows(sc + bias_ref[...], sink).astype(BF16))

    for gi in range(GROUPS_PER_STEP):
        p = probs[gi]
        o_grp = jnp.dot(p[:, WINDOW:], v_new_bs[gi], preferred_element_type=F32)
        o_cached = None
        for s in range(SEQ_PER_GROUP):
            cv_t = cv_ref[gi * SEQ_PER_GROUP + s]
            o_s = lax.dot_general(p[:, :WINDOW], cv_t.astype(BF16), _NT,
                                  preferred_element_type=F32)
            o_cached = o_s if s == 0 else jnp.where(q_seq == s, o_s, o_cached)
        o_grp = o_grp + o_cached
        for h in range(N_KV_HEADS):
            slab = jnp.zeros((GROUP_ROWS, KV_DIM), F32)
            for g in range(GQA_GROUP):
                hd = h * GQA_GROUP + g
                piece = jnp.where(blk16 == h, o_grp[hd * GROUP_ROWS:(hd + 1) * GROUP_ROWS, :], 0.0)
                slab = slab + pltpu.roll(piece, ((g - h) % GQA_GROUP) * HEAD_DIM, axis=1)
            rows = pl.ds(row0s[gi], GROUP_ROWS)
            cols = slice(h * KV_DIM, (h + 1) * KV_DIM)
            mix_ref[rows, cols] = (slab * gate_a[rows, cols]).astype(BF16)

    @pl.when(grp == n_steps - 1)
    def _():
        wout_copy(0).wait()
        out = jnp.dot(mix_ref[...], wout_ref[...], preferred_element_type=F32)
        xcur[...] = _layer_norm(alpha * xcur[...] + out, g_ref[...], b_ref[...])
        start_for_layer(layer + 1, lambda l: wout_copy(l).start())

        @pl.when(layer == depth - 1)
        def _():
            y_out = pltpu.make_async_copy(xcur, y_hbm, iosem.at[1])
            y_out.start()
            y_out.wait()


def _sample_layers(x8, st_all, ck_all, cv_all, w_in_bs, w_out_bs, conv_w, sinks, ln_g, ln_b, alpha):
    n_rows, d = x8.shape
    depth, n_seq = ck_all.shape[:2]
    n_steps = n_seq // SEQ_PER_STEP
    n_q = N_HEADS * GROUP_ROWS
    by_layer = lambda g: (g // n_steps, 0, 0)
    cache_spec = pl.BlockSpec((None, SEQ_PER_STEP, KV_DIM, WINDOW),
                              lambda g: (g // n_steps, g % n_steps, 0, 0))
    hbm = pl.BlockSpec(memory_space=pl.ANY)
    kernel = functools.partial(_sample_kernel, depth=depth, n_steps=n_steps, n_rows=n_rows,
                               alpha=alpha)
    return pl.pallas_call(
        kernel,
        grid=(depth * n_steps,),
        in_specs=[
            pl.BlockSpec(memory_space=pltpu.SMEM),
            hbm,
            pl.BlockSpec((None, 3, CONV_DIM), by_layer),
            pl.BlockSpec((None, 1, d), by_layer),
            pl.BlockSpec((None, 1, d), by_layer),
            cache_spec,
            cache_spec,
            pl.BlockSpec((None, n_rows, CONV_DIM), by_layer),
        ] + [hbm] * (2 * depth),
        out_specs=[
            hbm,
            cache_spec,
            cache_spec,
            pl.BlockSpec((None, n_rows, CONV_DIM), by_layer),
        ],
        out_shape=[
            jax.ShapeDtypeStruct((n_rows, d), F32),
            jax.ShapeDtypeStruct(ck_all.shape, F32),
            jax.ShapeDtypeStruct(cv_all.shape, F32),
            jax.ShapeDtypeStruct((depth, n_rows, CONV_DIM), F32),
        ],
        scratch_shapes=[
            pltpu.VMEM((d, PROJ_DIM), BF16),
            pltpu.VMEM((ATT_DIM + CONV_DIM, d), BF16),
            pltpu.SemaphoreType.DMA((len(COL_GROUPS),)),
            pltpu.SemaphoreType.DMA((2,)),
            pltpu.VMEM((n_rows, d), F32),
            pltpu.VMEM((N_HEADS, n_rows, KV_DIM), BF16),
            pltpu.VMEM((n_rows, KV_DIM), F32),
            pltpu.VMEM((n_rows, KV_DIM), F32),
            pltpu.VMEM((n_rows, ATT_DIM), F32),
            pltpu.VMEM((n_q, 2 * WINDOW), F32),
            pltpu.VMEM((n_q, LANES), F32),
            pltpu.VMEM((n_rows, ATT_DIM + CONV_DIM), BF16),
        ],
        compiler_params=pltpu.CompilerParams(
            dimension_semantics=("arbitrary",),
            vmem_limit_bytes=DECODE_VMEM_LIMIT,
        ),
        name="sample_layers",
    )(sinks, x8, conv_w, ln_g, ln_b, ck_all, cv_all, st_all, *w_in_bs, *w_out_bs)


def kernel(x_prompt, x_sample, cache_k, cache_v, state_conv, meta_tokens,
           w_in, conv_w, sinks, w_out, ln_g, ln_b):
    depth = w_in.shape[0]
    alpha = float((2 * depth) ** 0.25)
    batch, seq, d = x_prompt.shape
    n_seq, n_tok = x_sample.shape[:2]
    assert d == D_MODEL and seq % PROMPT_TILE == 0 and n_tok == SAMPLE_ROWS - TOK0
    assert meta_tokens.shape[0] == N_META and n_seq % SEQ_PER_STEP == 0
    assert cache_k.shape[2] == WINDOW and state_conv.shape[2] == 2

    w_in_b = w_in[0].astype(BF16)
    w_out_b = w_out[0].astype(BF16)
    ln_g3 = ln_g.reshape(depth, 1, d)
    ln_b3 = ln_b.reshape(depth, 1, d)
    xp = x_prompt
    xh = meta_tokens.astype(F32)
    xs = jnp.pad(x_sample, ((0, 0), (TOK0, 0), (0, 0))).reshape(n_seq * SAMPLE_ROWS, d)
    ck_all = jnp.transpose(cache_k, (0, 1, 3, 4, 2)).reshape(depth, n_seq, KV_DIM, WINDOW)
    cv_all = jnp.transpose(cache_v, (0, 1, 3, 4, 2)).reshape(depth, n_seq, KV_DIM, WINDOW)
    st_all = jnp.pad(state_conv, ((0, 0), (0, 0), (TOK0 - 2, SAMPLE_ROWS - TOK0), (0, 0)))
    st_all = st_all.reshape(depth, n_seq * SAMPLE_ROWS, CONV_DIM)

    kp, vp, cp = [], [], []
    w_in_bs, w_out_bs = [w_in_b], [w_out_b]
    for l in range(depth):
        xp, xh, k_last, v_last, c_last, *next_weights = _prompt_layer(
            l, xp, xh, w_in_bs[l], w_out_bs[l], w_in, w_out, conv_w, sinks, ln_g3, ln_b3, alpha)
        kp.append(k_last.reshape(batch, WINDOW, N_KV_HEADS, HEAD_DIM))
        vp.append(v_last.reshape(batch, WINDOW, N_KV_HEADS, HEAD_DIM))
        cp.append(c_last[:, 6:8, :])
        if next_weights:
            w_in_bs.append(next_weights[0])
            w_out_bs.append(next_weights[1])
    xs, kb_all, vb_all, u_all = _sample_layers(
        xs, st_all, ck_all, cv_all, w_in_bs, w_out_bs, conv_w, sinks, ln_g3, ln_b3, alpha)
    c_sample = u_all.reshape(depth, n_seq, SAMPLE_ROWS, CONV_DIM)[:, :, SAMPLE_ROWS - 2:, :]
    y_sample = xs.reshape(n_seq, SAMPLE_ROWS, d)[:, TOK0:, :]
    kv_shape = (depth, n_seq, N_KV_HEADS, HEAD_DIM, WINDOW)
    k_sample = jnp.transpose(kb_all.reshape(kv_shape), (0, 1, 4, 2, 3))
    v_sample = jnp.transpose(vb_all.reshape(kv_shape), (0, 1, 4, 2, 3))
    return (xp, y_sample, jnp.stack(kp), jnp.stack(vp), jnp.stack(cp),
            k_sample, v_sample, c_sample)
```

```python
import functools

import numpy as np
import jax
import jax.numpy as jnp
from jax import lax
from jax.experimental import pallas as pl
from jax.experimental.pallas import tpu as pltpu

F32 = jnp.float32
BF16 = jnp.bfloat16

D_MODEL = 2048
N_META = 16
ATT_DIM = 1024
CONV_DIM = 1024
HEAD_DIM = 64
N_HEADS = 16
N_KV_HEADS = 4
GQA_GROUP = N_HEADS // N_KV_HEADS
KV_DIM = N_KV_HEADS * HEAD_DIM
WINDOW = 128
PROJ_DIM = 2 * ATT_DIM + 2 * KV_DIM + 4 * CONV_DIM
LN_EPS = 1e-5
NEG_INF = -1e30
Q_SCALE = HEAD_DIM ** -0.5

C_Q = 0
C_K = ATT_DIM
C_V = C_K + KV_DIM
C_GA = C_V + KV_DIM
C_B = C_GA + ATT_DIM
C_C = C_B + CONV_DIM
C_H = C_C + CONV_DIM
C_GC = C_H + CONV_DIM
COL_GROUPS = (C_Q, C_GA, C_B, C_C, C_H, C_GC, PROJ_DIM)

LANES = 128
HEAD_PAD = WINDOW - N_META
PROMPT_TILE = 256
FILL_CHUNK = 256
OUT_CHUNK = 512
SAMPLE_ROWS = 8
VMEM_LIMIT = 58 * 1024 * 1024
DECODE_VMEM_LIMIT = 62 * 1024 * 1024

SLOPES = [float(np.float32(2.0 ** (-8.0 * (h + 1) / N_HEADS))) for h in range(N_HEADS)]

_NT = (((1,), (1,)), ((), ()))


def _silu(g):
    return g * (1.0 / (1.0 + jnp.exp(-g)))


def _softmax_rows(s, sink):
    m = jnp.maximum(jnp.max(s, axis=1, keepdims=True), sink)
    p = jnp.exp(s - m)
    denom = jnp.sum(p, axis=1, keepdims=True) + jnp.exp(sink - m)
    return p * (1.0 / denom)


def _softmax_cols(s, sink):
    m = jnp.maximum(jnp.max(s, axis=0, keepdims=True), sink)
    p = jnp.exp(s - m)
    denom = jnp.sum(p, axis=0, keepdims=True) + jnp.exp(sink - m)
    return p * (1.0 / denom)


def _layer_norm(z, g, b):
    mu = jnp.mean(z, axis=1, keepdims=True)
    zc = z - mu
    var = jnp.mean(zc * zc, axis=1, keepdims=True)
    return zc * lax.rsqrt(var + LN_EPS) * g + b


def _proj(xb, w_ref, c0, width):
    return jnp.dot(xb, w_ref[:, c0:c0 + width], preferred_element_type=F32)


def _store_k_variants(src, dst, r0, rows):
    low = lax.broadcasted_iota(jnp.int32, (rows, LANES), 1) < HEAD_DIM
    for cc in range(KV_DIM // LANES):
        col = src[:, cc * LANES:(cc + 1) * LANES]
        swapped = pltpu.roll(col, HEAD_DIM, axis=1)
        h_even, h_odd = 2 * cc, 2 * cc + 1
        dst[2 * h_even + 0, r0:r0 + rows, :] = jnp.where(low, col, 0.0).astype(BF16)
        dst[2 * h_even + 1, r0:r0 + rows, :] = jnp.where(low, 0.0, swapped).astype(BF16)
        dst[2 * h_odd + 0, r0:r0 + rows, :] = jnp.where(low, swapped, 0.0).astype(BF16)
        dst[2 * h_odd + 1, r0:r0 + rows, :] = jnp.where(low, 0.0, col).astype(BF16)


def _conv_chunk_pieces(xb, win_ref, cw_ref, ucar, mix_ref, rows, lo, width):
    cols = slice(lo, lo + width)
    got = {}

    def project(name, c0):
        got[name] = _proj(xb, win_ref, c0 + lo, width)

    def finish():
        u = got["c"] * got["h"]
        row = lax.broadcasted_iota(jnp.int32, (rows, width), 0)
        prev1 = ucar[7:8, cols]
        prev2 = ucar[6:7, cols]
        u1 = jnp.where(row == 0, prev1, pltpu.roll(u, 1, axis=0))
        u2 = jnp.where(row == 0, prev2, jnp.where(row == 1, prev1, pltpu.roll(u, 2, axis=0)))
        cy = cw_ref[0:1, cols] * u2 + cw_ref[1:2, cols] * u1 + cw_ref[2:3, cols] * u
        ucar[:, cols] = u[rows - 8:rows, :]
        gate_c = _silu(_proj(xb, win_ref, C_GC + lo, width))
        mix_ref[0:rows, ATT_DIM + lo:ATT_DIM + lo + width] = (
            (got["b"] * cy) * gate_c).astype(BF16)

    return [functools.partial(project, "b", C_B), functools.partial(project, "c", C_C),
            functools.partial(project, "h", C_H), finish]


def _store_v_transposed(src, dst, c0, rows):
    vt = src.T
    for h in range(N_KV_HEADS):
        dst[h, :, c0:c0 + rows] = vt[h * HEAD_DIM:(h + 1) * HEAD_DIM, :].astype(BF16)


def _scores_group(layer, qb, kmask, sinks_ref, kvar, bias_ref, p_scr, r0, h):
    q4 = jnp.concatenate(
        [qb[r0:r0 + WINDOW, c * LANES:(c + 1) * LANES] for c in (2 * h, 2 * h + 1)], axis=0)
    for par in range(2):
        keys = kvar[2 * h + par, r0:r0 + 2 * WINDOW, :]
        s2 = lax.dot_general(keys, q4, _NT, preferred_element_type=F32)
        for half in range(2):
            hd = GQA_GROUP * h + 2 * half + par
            lanes = slice(half * LANES, (half + 1) * LANES)
            s = s2[:, lanes] + SLOPES[hd] * bias_ref[...]
            if kmask is not None:
                s = s + kmask
            p_scr[2 * h + par, :, lanes] = _softmax_cols(s, sinks_ref[layer, hd]).astype(BF16)


def _values_group(gate_ref, vtvar, p_scr, mix_ref, r0, h):
    vals_t = vtvar[h, :, r0:r0 + 2 * WINDOW]
    o_t = jnp.concatenate([jnp.dot(vals_t, p_scr[2 * h + par], preferred_element_type=F32)
                           for par in range(2)], axis=0)
    for half in range(2):
        lanes = slice((2 * h + half) * LANES, (2 * h + half + 1) * LANES)
        o = o_t[:, half * LANES:(half + 1) * LANES].T
        mix_ref[r0:r0 + WINDOW, lanes] = (o * gate_ref[r0:r0 + WINDOW, lanes]).astype(BF16)


def _gate_chunk(xb, win_ref, gate_ref, rows, lo, width):
    gate_ref[0:rows, lo:lo + width] = _silu(_proj(xb, win_ref, C_GA + lo, width))


def _prompt_rows(x, rows, kmask0, after_qkv, arrived, layer, sinks_ref, win_ref, wout_ref, cw_ref,
                 kvar, vtvar, ucar, bias_ref, mix_ref, p_scr, gate_ref, z_ref, alpha):
    if arrived is None:
        arrived = lambda i: None
        in_arrival_order = False
    else:
        in_arrival_order = True
    xb = x.astype(BF16)
    arrived(0)
    hq = _proj(xb, win_ref, 0, C_GA)
    after_qkv()
    qb = (hq[:, C_Q:C_Q + ATT_DIM] * Q_SCALE).astype(BF16)
    kf = hq[:, C_K:C_K + KV_DIM]
    vf = hq[:, C_V:C_V + KV_DIM]
    _store_k_variants(kf, kvar, WINDOW, rows)
    _store_v_transposed(vf, vtvar, WINDOW, rows)

    n_blocks = rows // WINDOW
    gates = [(1, functools.partial(_gate_chunk, xb, win_ref, gate_ref, rows, lo, FILL_CHUNK))
             for lo in range(0, ATT_DIM, FILL_CHUNK)]
    convs = [list(zip((2, 3, 4, 5),
                      _conv_chunk_pieces(xb, win_ref, cw_ref, ucar, mix_ref, rows, lo, FILL_CHUNK)))
             for lo in range(0, CONV_DIM, FILL_CHUNK)]
    if in_arrival_order:
        fillers = gates + [conv[k] for k in range(4) for conv in convs]
    else:
        fillers = gates[:2] + convs[0] + gates[2:] + [p for conv in convs[1:] for p in conv]
    n_units = n_blocks * N_KV_HEADS
    cuts = [len(fillers) * u // n_units for u in range(n_units + 1)]
    assert cuts[N_KV_HEADS] >= len(gates) + len(convs[0])
    for blk in range(n_blocks):
        r0 = blk * WINDOW
        for h in range(N_KV_HEADS):
            unit = blk * N_KV_HEADS + h
            for group, filler in fillers[cuts[unit]:cuts[unit + 1]]:
                arrived(group)
                filler()
            _scores_group(layer, qb, kmask0 if blk == 0 else None, sinks_ref, kvar, bias_ref,
                          p_scr, r0, h)
        for h in range(N_KV_HEADS):
            _values_group(gate_ref, vtvar, p_scr, mix_ref, r0, h)
        if blk == 0:
            for i in range(2 * N_KV_HEADS):
                kvar[i, 0:WINDOW, :] = kvar[i, rows:rows + WINDOW, :]
            for i in range(N_KV_HEADS):
                vtvar[i, :, 0:WINDOW] = vtvar[i, :, rows:rows + WINDOW]

    arrived(len(COL_GROUPS) - 1)
    mix = mix_ref[0:rows, :]
    for c0 in range(0, D_MODEL, OUT_CHUNK):
        cols = slice(c0, c0 + OUT_CHUNK)
        out = jnp.dot(mix, wout_ref[:, cols], preferred_element_type=F32)
        z_ref[0:rows, cols] = alpha * x[:, cols] + out
    return kf, vf


def _cast_next_weights(t, last, next_layer, win_f32, wout_f32, win_next, wout_next,
                       stage_in, stage_out, sem):
    rows = stage_in[0].shape[0]
    srcs = (win_f32, wout_f32)
    dsts = (win_next, wout_next)

    def read(k, i):
        return pltpu.make_async_copy(srcs[i].at[next_layer, pl.ds(k * rows, rows), :],
                                     stage_in[i], sem.at[i])

    def write(k, i):
        return pltpu.make_async_copy(stage_out[i], dsts[i].at[pl.ds(k * rows, rows), :],
                                     sem.at[2 + i])

    @pl.when(t >= 1)
    def _():
        for i in range(2):
            read(t - 1, i).wait()

        @pl.when(t >= 2)
        def _():
            for i in range(2):
                write(t - 2, i).wait()

        for i in range(2):
            stage_out[i][...] = stage_in[i][...].astype(BF16)
            write(t - 1, i).start()

    @pl.when(t < last)
    def _():
        for i in range(2):
            read(t, i).start()

    @pl.when(t == last)
    def _():
        for i in range(2):
            write(t - 1, i).wait()


def _prompt_kernel_casting(sinks_ref, x_ref, xh_ref, win_hbm, wout_hbm, cw_ref, g_ref, b_ref,
                           win_f32, wout_f32,
                           y_ref, yh_ref, kl_ref, vl_ref, cs_ref, win_next, wout_next,
                           *scratch, layer, **static):
    cast_in_a, cast_in_b, cast_out_a, cast_out_b, cast_sem = scratch[-5:]
    _cast_next_weights(pl.program_id(0), pl.num_programs(0) - 1, layer + 1, win_f32, wout_f32,
                       win_next, wout_next, (cast_in_a, cast_in_b), (cast_out_a, cast_out_b),
                       cast_sem)
    _prompt_kernel(sinks_ref, x_ref, xh_ref, win_hbm, wout_hbm, cw_ref, g_ref, b_ref,
                   y_ref, yh_ref, kl_ref, vl_ref, cs_ref, *scratch[:-5], layer=layer, **static)


def _prompt_kernel(sinks_ref, x_ref, xh_ref, win_hbm, wout_hbm, cw_ref, g_ref, b_ref,
                   y_ref, yh_ref, kl_ref, vl_ref, cs_ref,
                   kvar, vtvar, khead, vhead, ucar, uhead, bias_ref, mix_ref, p_scr, gate_ref,
                   z_scr, win_ref, wout_ref, wsem, *, layer, tm, n_tiles, alpha):
    t = pl.program_id(0)
    last = pl.num_programs(0) - 1
    j = t % n_tiles
    shared = (layer, sinks_ref, win_ref, wout_ref, cw_ref, kvar, vtvar, ucar, bias_ref, mix_ref,
              p_scr, gate_ref, z_scr, alpha)
    key_row = lax.broadcasted_iota(jnp.int32, (2 * WINDOW, WINDOW), 0)
    n_win = len(COL_GROUPS) - 1

    def weight_copy(i):
        if i == n_win:
            return pltpu.make_async_copy(wout_hbm, wout_ref, wsem.at[i])
        cols = pl.ds(COL_GROUPS[i], COL_GROUPS[i + 1] - COL_GROUPS[i])
        return pltpu.make_async_copy(win_hbm.at[:, cols], win_ref.at[:, cols], wsem.at[i])

    @pl.when(t == 0)
    def _():
        in_flight = 2
        for i in range(in_flight):
            weight_copy(i).start()
        waited = set()

        def arrived(i):
            if i not in waited:
                assert len(waited) == i
                waited.add(i)
                weight_copy(i).wait()
                if i + in_flight <= n_win:
                    weight_copy(i + in_flight).start()

        qi = lax.broadcasted_iota(jnp.int32, (2 * WINDOW, WINDOW), 1)
        dist = WINDOW + qi - key_row
        visible = (dist >= 0) & (dist < WINDOW)
        bias_ref[...] = jnp.where(visible, -dist.astype(F32), NEG_INF)
        z_scr[...] = jnp.zeros(z_scr.shape, F32)
        kvar[:, 0:WINDOW, :] = jnp.zeros((2 * N_KV_HEADS, WINDOW, LANES), BF16)
        vtvar[:, :, 0:WINDOW] = jnp.zeros((N_KV_HEADS, HEAD_DIM, WINDOW), BF16)
        ucar[...] = jnp.zeros(ucar.shape, F32)
        kmask = jnp.where(key_row < WINDOW + HEAD_PAD, NEG_INF, 0.0)
        xh = jnp.concatenate([jnp.zeros((HEAD_PAD, D_MODEL), F32), xh_ref[...]], axis=0)
        _prompt_rows(xh, WINDOW, kmask, lambda: None, arrived, *shared)
        assert len(waited) == n_win + 1
        yh_ref[...] = _layer_norm(z_scr[0:WINDOW, :], g_ref[...], b_ref[...])[HEAD_PAD:, :]
        khead[...] = kvar[:, 0:WINDOW, :]
        vhead[...] = vtvar[:, :, 0:WINDOW]
        uhead[...] = ucar[...]

    def norm_previous_tile():
        y_ref[...] = _layer_norm(z_scr[...], g_ref[...], b_ref[...])

    @pl.when(t < last)
    def _():
        @pl.when(j == 0)
        def _():
            kvar[:, 0:WINDOW, :] = khead[...]
            vtvar[:, :, 0:WINDOW] = vhead[...]
            ucar[...] = uhead[...]

        kmask = jnp.where(key_row < HEAD_PAD, jnp.where(j == 0, NEG_INF, 0.0), 0.0)
        kf, vf = _prompt_rows(x_ref[...], tm, kmask, norm_previous_tile, None, *shared)

        @pl.when(j == n_tiles - 1)
        def _():
            kl_ref[...] = kf[tm - WINDOW:tm, :]
            vl_ref[...] = vf[tm - WINDOW:tm, :]
            cs_ref[...] = ucar[...]

    @pl.when(t == last)
    def _():
        norm_previous_tile()


def _resident(shape, index_map):
    return pl.BlockSpec(shape, index_map, pipeline_mode=pl.Buffered(1))


def _prompt_layer(layer, x, xh, w_in_b, w_out_b, w_in, w_out, conv_w, sinks, ln_g, ln_b, alpha):
    batch, seq, d = x.shape
    depth = w_in.shape[0]
    tm = PROMPT_TILE
    n_tiles = seq // tm
    total = batch * n_tiles
    const2 = lambda t: (0, 0)
    this_layer = lambda t: (layer, 0, 0)

    def tile_block(t):
        t = jnp.minimum(t, total - 1)
        return (t // n_tiles, t % n_tiles, 0)

    def prev_tile_block(t):
        return tile_block(jnp.maximum(t - 1, 0))

    per_batch = lambda t: (jnp.minimum(t, total - 1) // n_tiles, 0, 0)
    static = dict(layer=layer, tm=tm, n_tiles=n_tiles, alpha=alpha)
    operands = [sinks, x, xh, w_in_b, w_out_b, conv_w, ln_g, ln_b]
    in_specs = [
        pl.BlockSpec(memory_space=pltpu.SMEM),
        pl.BlockSpec((None, tm, d), tile_block),
        _resident((N_META, d), const2),
        pl.BlockSpec(memory_space=pl.ANY),
        pl.BlockSpec(memory_space=pl.ANY),
        pl.BlockSpec((None, 3, CONV_DIM), this_layer),
        pl.BlockSpec((None, 1, d), this_layer),
        pl.BlockSpec((None, 1, d), this_layer),
    ]
    out_specs = [
        pl.BlockSpec((None, tm, d), prev_tile_block),
        pl.BlockSpec((N_META, d), const2),
        pl.BlockSpec((None, WINDOW, KV_DIM), per_batch),
        pl.BlockSpec((None, WINDOW, KV_DIM), per_batch),
        pl.BlockSpec((None, 8, CONV_DIM), per_batch),
    ]
    out_shape = [
        jax.ShapeDtypeStruct((batch, seq, d), F32),
        jax.ShapeDtypeStruct((N_META, d), F32),
        jax.ShapeDtypeStruct((batch, WINDOW, KV_DIM), F32),
        jax.ShapeDtypeStruct((batch, WINDOW, KV_DIM), F32),
        jax.ShapeDtypeStruct((batch, 8, CONV_DIM), F32),
    ]
    scratch_shapes = [
        pltpu.VMEM((2 * N_KV_HEADS, WINDOW + tm, LANES), BF16),
        pltpu.VMEM((N_KV_HEADS, HEAD_DIM, WINDOW + tm), BF16),
        pltpu.VMEM((2 * N_KV_HEADS, WINDOW, LANES), BF16),
        pltpu.VMEM((N_KV_HEADS, HEAD_DIM, WINDOW), BF16),
        pltpu.VMEM((8, CONV_DIM), F32),
        pltpu.VMEM((8, CONV_DIM), F32),
        pltpu.VMEM((2 * WINDOW, WINDOW), F32),
        pltpu.VMEM((tm, ATT_DIM + CONV_DIM), BF16),
        pltpu.VMEM((2 * N_KV_HEADS, 2 * WINDOW, 2 * WINDOW), BF16),
        pltpu.VMEM((tm, ATT_DIM), F32),
        pltpu.VMEM((tm, d), F32),
        pltpu.VMEM((d, PROJ_DIM), BF16),
        pltpu.VMEM((ATT_DIM + CONV_DIM, d), BF16),
        pltpu.SemaphoreType.DMA((len(COL_GROUPS),)),
    ]
    kernel = _prompt_kernel
    if layer + 1 < depth:
        cast_rows = d // total
        assert cast_rows * total == d and cast_rows % 16 == 0
        kernel = _prompt_kernel_casting
        operands += [w_in, w_out]
        in_specs += [pl.BlockSpec(memory_space=pl.ANY)] * 2
        out_specs += [pl.BlockSpec(memory_space=pl.ANY)] * 2
        out_shape += [jax.ShapeDtypeStruct(w_in_b.shape, BF16),
                      jax.ShapeDtypeStruct(w_out_b.shape, BF16)]
        scratch_shapes += [
            pltpu.VMEM((cast_rows, PROJ_DIM), F32),
            pltpu.VMEM((cast_rows, d), F32),
            pltpu.VMEM((cast_rows, PROJ_DIM), BF16),
            pltpu.VMEM((cast_rows, d), BF16),
            pltpu.SemaphoreType.DMA((4,)),
        ]
    return pl.pallas_call(
        functools.partial(kernel, **static),
        grid=(total + 1,),
        in_specs=in_specs,
        out_specs=out_specs,
        out_shape=out_shape,
        scratch_shapes=scratch_shapes,
        compiler_params=pltpu.CompilerParams(
            dimension_semantics=("arbitrary",),
            vmem_limit_bytes=VMEM_LIMIT,
        ),
        name="prompt_layer",
    )(*operands)


SEQ_PER_GROUP = 2
GROUP_ROWS = SEQ_PER_GROUP * SAMPLE_ROWS
GROUPS_PER_STEP = 4
SEQ_PER_STEP = SEQ_PER_GROUP * GROUPS_PER_STEP
N_NEW = 4
TOK0 = SAMPLE_ROWS - N_NEW
LOG_SAMPLE_ROWS = 3
LOG_GROUP_ROWS = 4
LOG_HEAD_DIM = 6
assert (1 << LOG_SAMPLE_ROWS, 1 << LOG_GROUP_ROWS, 1 << LOG_HEAD_DIM) == (
    SAMPLE_ROWS, GROUP_ROWS, HEAD_DIM)


def _sample_kernel(sinks_ref, x_hbm, cw_ref, g_ref, b_ref, ck_ref, cv_ref, st_ref, *refs,
                   depth, n_steps, n_rows, alpha):
    win_hbm, wout_hbm = refs[:depth], refs[depth:2 * depth]
    (y_hbm, kb_ref, vb_ref, u_ref, win_ref, wout_ref, wsem, iosem, xcur,
     qe, knew, vnew, gate_a, bias_ref, sinkcol, mix_ref) = refs[2 * depth:]
    layer = pl.program_id(0) // n_steps
    grp = pl.program_id(0) % n_steps
    n_q = N_HEADS * GROUP_ROWS
    lane_blk = lax.broadcasted_iota(jnp.int32, (n_rows, KV_DIM), 1) >> LOG_HEAD_DIM

    col_groups = COL_GROUPS
    n_win = len(col_groups) - 1

    def win_copy(l, i):
        cols = pl.ds(col_groups[i], col_groups[i + 1] - col_groups[i])
        return pltpu.make_async_copy(win_hbm[l].at[:, cols], win_ref.at[:, cols], wsem.at[i])

    def wout_copy(l):
        return pltpu.make_async_copy(wout_hbm[l], wout_ref, wsem.at[n_win])

    def start_for_layer(next_layer, start):
        for l in range(1, depth):
            @pl.when(next_layer == l)
            def _():
                start(l)

    def proj(xb, i, width):
        win_copy(0, i).wait()
        return _proj(xb, win_ref, col_groups[i], width)

    @pl.when(pl.program_id(0) == 0)
    def _():
        x_in = pltpu.make_async_copy(x_hbm, xcur, iosem.at[0])
        x_in.start()
        for i in range(n_win):
            win_copy(0, i).start()
        wout_copy(0).start()
        x_in.wait()

    @pl.when(grp == 0)
    def _():
        x = xcur[...]
        xb = x.astype(BF16)
        hq = proj(xb, 0, C_GA)
        q = hq[:, C_Q:C_Q + ATT_DIM] * Q_SCALE
        knew[...] = hq[:, C_K:C_K + KV_DIM]
        vnew[...] = hq[:, C_V:C_V + KV_DIM]
        for hd in range(N_HEADS):
            h, g = divmod(hd, GQA_GROUP)
            slab = q[:, h * KV_DIM:(h + 1) * KV_DIM]
            moved = pltpu.roll(slab, ((h - g) % GQA_GROUP) * HEAD_DIM, axis=1)
            qe[hd] = jnp.where(lane_blk == h, moved, 0.0).astype(BF16)
        gate_a[...] = _silu(proj(xb, 1, ATT_DIM))
        start_for_layer(layer + 1, lambda l: [win_copy(l, i).start() for i in (0, 1)])

        bg = proj(xb, 2, CONV_DIM)
        u = proj(xb, 3, CONV_DIM) * proj(xb, 4, CONV_DIM)
        start_for_layer(layer + 1, lambda l: [win_copy(l, i).start() for i in (2, 3, 4)])
        r8 = lax.broadcasted_iota(jnp.int32, (n_rows, CONV_DIM), 0) & (SAMPLE_ROWS - 1)
        is_state = (r8 >= TOK0 - 2) & (r8 < TOK0)
        u = jnp.where(is_state, st_ref[...], u)
        u_ref[...] = u
        cy = (cw_ref[0:1, :] * pltpu.roll(u, 2, axis=0) + cw_ref[1:2, :] * pltpu.roll(u, 1, axis=0)
              + cw_ref[2:3, :] * u)
        gate_c = _silu(proj(xb, 5, CONV_DIM))
        mix_ref[:, ATT_DIM:] = ((bg * cy) * gate_c).astype(BF16)

        qrow = lax.broadcasted_iota(jnp.int32, (n_q, 2 * WINDOW), 0)
        key = lax.broadcasted_iota(jnp.int32, (n_q, 2 * WINDOW), 1)
        q_tok = jnp.maximum((qrow & (SAMPLE_ROWS - 1)) - TOK0, 0)
        q_seq = (qrow >> LOG_SAMPLE_ROWS) & (SEQ_PER_GROUP - 1)
        new = key - WINDOW
        k_tok = (new & (SAMPLE_ROWS - 1)) - TOK0
        k_seq = new >> LOG_SAMPLE_ROWS
        cached = key < WINDOW
        dist = jnp.where(cached, WINDOW + q_tok - key, q_tok - k_tok)
        ok_new = (new >= 0) & (new < GROUP_ROWS) & (k_seq == q_seq) & (k_tok >= 0)
        visible = (dist >= 0) & (dist < WINDOW) & (cached | ok_new)
        slope = jnp.zeros((n_q, 2 * WINDOW), F32)
        sink = jnp.zeros((n_q, LANES), F32)
        srow = lax.broadcasted_iota(jnp.int32, (n_q, LANES), 0)
        for hd in range(N_HEADS):
            slope = jnp.where((qrow >> LOG_GROUP_ROWS) == hd, SLOPES[hd], slope)
            sink = jnp.where((srow >> LOG_GROUP_ROWS) == hd, sinks_ref[layer, hd], sink)
        bias_ref[...] = jnp.where(visible, -(slope * dist.astype(F32)), NEG_INF)
        sinkcol[...] = sink
        start_for_layer(layer + 1, lambda l: win_copy(l, n_win - 1).start())

    q_seq = ((lax.broadcasted_iota(jnp.int32, (n_q, KV_DIM), 0) >> LOG_SAMPLE_ROWS)
             & (SEQ_PER_GROUP - 1))
    newest = lax.broadcasted_iota(jnp.int32, (KV_DIM, WINDOW), 1) >= WINDOW - N_NEW
    blk16 = lax.broadcasted_iota(jnp.int32, (GROUP_ROWS, KV_DIM), 1) >> LOG_HEAD_DIM
    pad_rows = jnp.zeros((WINDOW - GROUP_ROWS, KV_DIM), F32)
    sink = sinkcol[:, 0:1]

    probs, v_new_bs, row0s = [], [], []
    for gi in range(GROUPS_PER_STEP):
        g0 = pl.multiple_of((grp * GROUPS_PER_STEP + gi) * GROUP_ROWS, GROUP_ROWS)
        row0s.append(g0)
        w_g = jnp.concatenate([qe[hd, pl.ds(g0, GROUP_ROWS), :] for hd in range(N_HEADS)], axis=0)
        k_new = jnp.concatenate([knew[pl.ds(g0, GROUP_ROWS), :], pad_rows], axis=0)
        v_new = jnp.concatenate([vnew[pl.ds(g0, GROUP_ROWS), :], pad_rows], axis=0)
        v_new_bs.append(v_new.astype(BF16))
        k_new_t = k_new.T
        v_new_t = v_new.T
        sc_cached = None
        for s in range(SEQ_PER_GROUP):
            n = gi * SEQ_PER_GROUP + s
            ck_t = ck_ref[n]
            sc_s = jnp.dot(w_g, ck_t.astype(BF16), preferred_element_type=F32)
            sc_cached = sc_s if s == 0 else jnp.where(q_seq[:, :WINDOW] == s, sc_s, sc_cached)
            to_tail = WINDOW - N_NEW - (s * SAMPLE_ROWS + TOK0)
            for cache_t, new_t, out_ref in ((ck_t, k_new_t, kb_ref), (cv_ref[n], v_new_t, vb_ref)):
                out_ref[n] = jnp.where(newest, pltpu.roll(new_t, to_tail, axis=1),
                                       pltpu.roll(cache_t, WINDOW - N_NEW, axis=1))
        sc_new = lax.dot_general(w_g, k_new.astype(BF16), _NT, preferred_element_type=F32)
        sc = jnp.concatenate([sc_cached, sc_new], axis=1)
        probs.append(_softmax_rows(sc + bias_ref[...], sink).astype(BF16))

    for gi in range(GROUPS_PER_STEP):
        p = probs[gi]
        o_grp = jnp.dot(p[:, WINDOW:], v_new_bs[gi], preferred_element_type=F32)
        o_cached = None
        for s in range(SEQ_PER_GROUP):
            cv_t = cv_ref[gi * SEQ_PER_GROUP + s]
            o_s = lax.dot_general(p[:, :WINDOW], cv_t.astype(BF16), _NT,
                                  preferred_element_type=F32)
            o_cached = o_s if s == 0 else jnp.where(q_seq == s, o_s, o_cached)
        o_grp = o_grp + o_cached
        for h in range(N_KV_HEADS):
            slab = jnp.zeros((GROUP_ROWS, KV_DIM), F32)
            for g in range(GQA_GROUP):
                hd = h * GQA_GROUP + g
                piece = jnp.where(blk16 == h, o_grp[hd * GROUP_ROWS:(hd + 1) * GROUP_ROWS, :], 0.0)
                slab = slab + pltpu.roll(piece, ((g - h) % GQA_GROUP) * HEAD_DIM, axis=1)
            rows = pl.ds(row0s[gi], GROUP_ROWS)
            cols = slice(h * KV_DIM, (h + 1) * KV_DIM)
            mix_ref[rows, cols] = (slab * gate_a[rows, cols]).astype(BF16)

    @pl.when(grp == n_steps - 1)
    def _():
        wout_copy(0).wait()
        out = jnp.dot(mix_ref[...], wout_ref[...], preferred_element_type=F32)
        xcur[...] = _layer_norm(alpha * xcur[...] + out, g_ref[...], b_ref[...])
        start_for_layer(layer + 1, lambda l: wout_copy(l).start())

        @pl.when(layer == depth - 1)
        def _():
            y_out = pltpu.make_async_copy(xcur, y_hbm, iosem.at[1])
            y_out.start()
            y_out.wait()


def _sample_layers(x8, st_all, ck_all, cv_all, w_in_bs, w_out_bs, conv_w, sinks, ln_g, ln_b, alpha):
    n_rows, d = x8.shape
    depth, n_seq = ck_all.shape[:2]
    n_steps = n_seq // SEQ_PER_STEP
    n_q = N_HEADS * GROUP_ROWS
    by_layer = lambda g: (g // n_steps, 0, 0)
    cache_spec = pl.BlockSpec((None, SEQ_PER_STEP, KV_DIM, WINDOW),
                              lambda g: (g // n_steps, g % n_steps, 0, 0))
    hbm = pl.BlockSpec(memory_space=pl.ANY)
    kernel = functools.partial(_sample_kernel, depth=depth, n_steps=n_steps, n_rows=n_rows,
                               alpha=alpha)
    return pl.pallas_call(
        kernel,
        grid=(depth * n_steps,),
        in_specs=[
            pl.BlockSpec(memory_space=pltpu.SMEM),
            hbm,
            pl.BlockSpec((None, 3, CONV_DIM), by_layer),
            pl.BlockSpec((None, 1, d), by_layer),
            pl.BlockSpec((None, 1, d), by_layer),
            cache_spec,
            cache_spec,
            pl.BlockSpec((None, n_rows, CONV_DIM), by_layer),
        ] + [hbm] * (2 * depth),
        out_specs=[
            hbm,
            cache_spec,
            cache_spec,
            pl.BlockSpec((None, n_rows, CONV_DIM), by_layer),
        ],
        out_shape=[
            jax.ShapeDtypeStruct((n_rows, d), F32),
            jax.ShapeDtypeStruct(ck_all.shape, F32),
            jax.ShapeDtypeStruct(cv_all.shape, F32),
            jax.ShapeDtypeStruct((depth, n_rows, CONV_DIM), F32),
        ],
        scratch_shapes=[
            pltpu.VMEM((d, PROJ_DIM), BF16),
            pltpu.VMEM((ATT_DIM + CONV_DIM, d), BF16),
            pltpu.SemaphoreType.DMA((len(COL_GROUPS),)),
            pltpu.SemaphoreType.DMA((2,)),
            pltpu.VMEM((n_rows, d), F32),
            pltpu.VMEM((N_HEADS, n_rows, KV_DIM), BF16),
            pltpu.VMEM((n_rows, KV_DIM), F32),
            pltpu.VMEM((n_rows, KV_DIM), F32),
            pltpu.VMEM((n_rows, ATT_DIM), F32),
            pltpu.VMEM((n_q, 2 * WINDOW), F32),
            pltpu.VMEM((n_q, LANES), F32),
            pltpu.VMEM((n_rows, ATT_DIM + CONV_DIM), BF16),
        ],
        compiler_params=pltpu.CompilerParams(
            dimension_semantics=("arbitrary",),
            vmem_limit_bytes=DECODE_VMEM_LIMIT,
        ),
        name="sample_layers",
    )(sinks, x8, conv_w, ln_g, ln_b, ck_all, cv_all, st_all, *w_in_bs, *w_out_bs)


def kernel(x_prompt, x_sample, cache_k, cache_v, state_conv, meta_tokens,
           w_in, conv_w, sinks, w_out, ln_g, ln_b):
    depth = w_in.shape[0]
    alpha = float((2 * depth) ** 0.25)
    batch, seq, d = x_prompt.shape
    n_seq, n_tok = x_sample.shape[:2]
    assert d == D_MODEL and seq % PROMPT_TILE == 0 and n_tok == SAMPLE_ROWS - TOK0
    assert meta_tokens.shape[0] == N_META and n_seq % SEQ_PER_STEP == 0
    assert cache_k.shape[2] == WINDOW and state_conv.shape[2] == 2

    w_in_b = w_in[0].astype(BF16)
    w_out_b = w_out[0].astype(BF16)
    ln_g3 = ln_g.reshape(depth, 1, d)
    ln_b3 = ln_b.reshape(depth, 1, d)
    xp = x_prompt
    xh = meta_tokens.astype(F32)
    xs = jnp.pad(x_sample, ((0, 0), (TOK0, 0), (0, 0))).reshape(n_seq * SAMPLE_ROWS, d)
    ck_all = jnp.transpose(cache_k, (0, 1, 3, 4, 2)).reshape(depth, n_seq, KV_DIM, WINDOW)
    cv_all = jnp.transpose(cache_v, (0, 1, 3, 4, 2)).reshape(depth, n_seq, KV_DIM, WINDOW)
    st_all = jnp.pad(state_conv, ((0, 0), (0, 0), (TOK0 - 2, SAMPLE_ROWS - TOK0), (0, 0)))
    st_all = st_all.reshape(depth, n_seq * SAMPLE_ROWS, CONV_DIM)

    kp, vp, cp = [], [], []
    w_in_bs, w_out_bs = [w_in_b], [w_out_b]
    for l in range(depth):
        xp, xh, k_last, v_last, c_last, *next_weights = _prompt_layer(
            l, xp, xh, w_in_bs[l], w_out_bs[l], w_in, w_out, conv_w, sinks, ln_g3, ln_b3, alpha)
        kp.append(k_last.reshape(batch, WINDOW, N_KV_HEADS, HEAD_DIM))
        vp.append(v_last.reshape(batch, WINDOW, N_KV_HEADS, HEAD_DIM))
        cp.append(c_last[:, 6:8, :])
        if next_weights:
            w_in_bs.append(next_weights[0])
            w_out_bs.append(next_weights[1])
    xs, kb_all, vb_all, u_all = _sample_layers(
        xs, st_all, ck_all, cv_all, w_in_bs, w_out_bs, conv_w, sinks, ln_g3, ln_b3, alpha)
    c_sample = u_all.reshape(depth, n_seq, SAMPLE_ROWS, CONV_DIM)[:, :, SAMPLE_ROWS - 2:, :]
    y_sample = xs.reshape(n_seq, SAMPLE_ROWS, d)[:, TOK0:, :]
    kv_shape = (depth, n_seq, N_KV_HEADS, HEAD_DIM, WINDOW)
    k_sample = jnp.transpose(kb_all.reshape(kv_shape), (0, 1, 4, 2, 3))
    v_sample = jnp.transpose(vb_all.reshape(kv_shape), (0, 1, 4, 2, 3))
    return (xp, y_sample, jnp.stack(kp), jnp.stack(vp), jnp.stack(cp),
            k_sample, v_sample, c_sample)
```
